```python
import jax, jax.numpy as jnp
from jax import lax
import numpy as np

D_MODEL = 2048
BATCH = 4
SEQ = 2048
DEPTH = 2
DEC_BATCH = 128
DEC_SEQ = 1
PAST_LEN = 16384
PAGE_SIZE = 128

BRANCH_W = D_MODEL // 2
N_BRANCH = 3
R_HEAD = 64
R_HEADS = BRANCH_W // R_HEAD
R_LORA_W = 64
R_LORA_A = 64
R_LORA_G = 64
R_GN_EPS = 64e-5
R_COLS = 3 * BRANCH_W + R_LORA_W + R_LORA_A + R_LORA_G
R_SPLITS = [BRANCH_W, 2 * BRANCH_W, 3 * BRANCH_W, 3 * BRANCH_W + R_LORA_W, 3 * BRANCH_W + R_LORA_W + R_LORA_A]
M_HEADS = 4
M_DK = BRANCH_W // M_HEADS
M_DV = BRANCH_W // M_HEADS
CONV_W = 4
M_COLS = 4 * BRANCH_W + 2 * M_HEADS
M_SPLITS = [2 * BRANCH_W, 3 * BRANCH_W, 3 * BRANCH_W + M_HEADS, 3 * BRANCH_W + 2 * M_HEADS]
G_HEADS = 4
G_DK = BRANCH_W // (2 * G_HEADS)
G_DV = BRANCH_W // G_HEADS
G_LR = 16
G_GATE_NORM = 16.0
G_COLS = 2 * G_HEADS * G_DK + 2 * BRANCH_W + G_LR
G_SPLITS = [G_HEADS * G_DK, 2 * G_HEADS * G_DK, 2 * G_HEADS * G_DK + BRANCH_W, 2 * G_HEADS * G_DK + BRANCH_W + G_LR]
GATE_COLS = N_BRANCH * D_MODEL
N_IN = R_COLS + M_COLS + G_COLS + GATE_COLS
IN_SPLITS = [R_COLS, R_COLS + M_COLS, R_COLS + M_COLS + G_COLS]
D_FF = -((-8 * D_MODEL) // (3 * 256)) * 256
CHUNK = 64
EPS = 1e-6

kernel_name = 'hybrid_rwkv7_mlstm_gla_step'


def rmsnorm(x, g):
    xf = x.astype(jnp.float32)
    y = xf * lax.rsqrt(jnp.mean(xf * xf, axis=-1, keepdims=True) + EPS)
    return (y * g).astype(x.dtype)


def _heads(t, n_heads):
    return t.reshape(*t.shape[:-1], n_heads, -1)


def head_rmsnorm(h, g):
    h = h * lax.rsqrt(jnp.mean(h * h, axis=-1, keepdims=True) + EPS)
    return h.reshape(*h.shape[:-2], -1) * g


def _chunk_len(T):
    return CHUNK if T % CHUNK == 0 else T


def _to_chunks(t, L):
    B, H, T = t.shape[:3]
    t = t.reshape(B, H, T // L, L, *t.shape[3:])
    return jnp.moveaxis(t, 2, 0)


def _from_chunks(t):
    t = jnp.moveaxis(t, 0, 2)
    return t.reshape(t.shape[0], t.shape[1], -1, *t.shape[4:])


def rwkv7_branch(p, prev, S0, mu, w0, w2, a0, a2, g2, k_k, k_a, r_k, ln_g, ln_b):
    B, T, _ = p.shape
    f32 = jnp.float32
    p_prev = jnp.concatenate([prev.astype(p.dtype)[:, None, :], p[:, :-1, :]], axis=1)
    xs = p + (p_prev - p) * mu
    r, k, v, xw, xa, xg = jnp.split(xs, R_SPLITS, axis=-1)
    w = -jax.nn.softplus(-(w0 + jnp.tanh(xw) @ w2)) - 0.5
    log_decay = -jnp.exp(w.astype(f32))
    a = jax.nn.sigmoid(a0 + xa @ a2)
    g = jax.nn.sigmoid(xg) @ g2
    kk = _heads((k * k_k).astype(f32), R_HEADS)
    kk = kk / jnp.maximum(jnp.linalg.norm(kk, axis=-1, keepdims=True), 1e-12)
    k = k * (1.0 + (a - 1.0) * k_a)
    rh, kh, vh, ah, ldh = [_heads(t.astype(f32), R_HEADS) for t in (r, k, v, a, log_decay)]

    def step(S, inp):
        r_t, k_t, v_t, kk_t, a_t, ld_t = inp
        S = (S * jnp.exp(ld_t)[:, :, None, :]
             - jnp.einsum('bhvk,bhk->bhv', S, kk_t)[..., None] * (kk_t * a_t)[:, :, None, :]
             + v_t[..., :, None] * k_t[..., None, :])
        return S, jnp.einsum('bhvk,bhk->bhv', S, r_t)

    seq = tuple(jnp.moveaxis(t, 1, 0) for t in (rh, kh, vh, kk, ah, ldh))
    S_T, out = lax.scan(step, S0.astype(f32), seq)
    out = jnp.moveaxis(out, 0, 1)
    mean = jnp.mean(out, axis=-1, keepdims=True)
    var = jnp.mean(jnp.square(out - mean), axis=-1, keepdims=True)
    out = ((out - mean) * lax.rsqrt(var + R_GN_EPS)).reshape(B, T, BRANCH_W) * ln_g + ln_b
    bonus = jnp.sum(rh * kh * r_k, axis=-1, keepdims=True) * vh
    out = (out + bonus.reshape(B, T, BRANCH_W)) * g
    return out.astype(p.dtype), S_T, p[:, -1, :]


def mlstm_chunks(q, k, v, ig, lf, C0, n0, m0):
    L = _chunk_len(q.shape[2])
    mask = jnp.tril(jnp.ones((L, L), dtype=bool))

    def step(carry, inp):
        C, n, m = carry
        qc, kc, vc, ic, fc = inp
        F = jnp.cumsum(fc, axis=-1)
        D = jnp.where(mask, F[..., :, None] - F[..., None, :] + ic[..., None, :], -jnp.inf)
        inter = F + m[..., None]
        m_t = jnp.maximum(inter, jnp.max(D, axis=-1))
        w_inter = jnp.exp(inter - m_t)
        S = jnp.einsum('bhtd,bhsd->bhts', qc, kc) * jnp.exp(D - m_t[..., None])
        num = w_inter[..., None] * jnp.einsum('bhtd,bhde->bhte', qc, C) + jnp.einsum('bhts,bhse->bhte', S, vc)
        den = w_inter * jnp.einsum('bhtd,bhd->bht', qc, n) + jnp.sum(S, axis=-1)
        h = num / jnp.maximum(jnp.abs(den), jnp.exp(-m_t))[..., None]
        FL = F[..., -1]
        g_s = FL[..., None] - F + ic
        m_new = jnp.maximum(FL + m, jnp.max(g_s, axis=-1))
        a_c = jnp.exp(FL + m - m_new)
        w_s = jnp.exp(g_s - m_new[..., None])
        C = a_c[..., None, None] * C + jnp.einsum('bhs,bhsd,bhse->bhde', w_s, kc, vc)
        n = a_c[..., None] * n + jnp.einsum('bhs,bhsd->bhd', w_s, kc)
        return (C, n, m_new), h

    f32 = jnp.float32
    (C, n, m), h = lax.scan(step, (C0.astype(f32), n0.astype(f32), m0.astype(f32)),
                            tuple(_to_chunks(t, L) for t in (q, k, v, ig, lf)))
    return _from_chunks(h), C, n, m


def mlstm_branch(p, conv_buf, C0, n0, m0, conv_w, conv_b, i_b, f_b, norm_g):
    B, T, _ = p.shape
    f32 = jnp.float32
    qk_raw, v, i_pre, f_pre, o = jnp.split(p, M_SPLITS, axis=-1)
    full = jnp.concatenate([conv_buf.astype(p.dtype), qk_raw], axis=1)
    conv = conv_b + sum(full[:, j:j + T, :] * conv_w[j] for j in range(CONV_W))
    q, k = jnp.split(jax.nn.silu(conv), 2, axis=-1)
    qh = _heads(q.astype(f32), M_HEADS).transpose(0, 2, 1, 3)
    kh = _heads(k.astype(f32), M_HEADS).transpose(0, 2, 1, 3) * (M_DK ** -0.5)
    vh = _heads(v.astype(f32), M_HEADS).transpose(0, 2, 1, 3)
    ig = (i_pre + i_b).astype(f32).transpose(0, 2, 1)
    lf = jax.nn.log_sigmoid((f_pre + f_b).astype(f32)).transpose(0, 2, 1)
    h, C, n, m = mlstm_chunks(qh, kh, vh, ig, lf, C0, n0, m0)
    h = head_rmsnorm(h.transpose(0, 2, 1, 3), norm_g)
    out = jax.nn.sigmoid(o) * h
    return out.astype(p.dtype), full[:, -(CONV_W - 1):, :], C, n, m


def gla_branch(p, S0, a2, a_b, norm_g):
    B, T, _ = p.shape
    f32 = jnp.float32
    q, k, v, xa, og = jnp.split(p, G_SPLITS, axis=-1)
    lg = jax.nn.log_sigmoid((xa @ a2 + a_b).astype(f32)) / G_GATE_NORM
    qh = _heads(q.astype(f32), G_HEADS).transpose(0, 2, 1, 3) * (G_DK ** -0.5)
    kh = _heads(k.astype(f32), G_HEADS).transpose(0, 2, 1, 3)
    vh = _heads(v.astype(f32), G_HEADS).transpose(0, 2, 1, 3)
    gh = _heads(lg, G_HEADS).transpose(0, 2, 1, 3)
    L = _chunk_len(T)
    mask = jnp.tril(jnp.ones((L, L), dtype=bool))[..., None]

    def step(S, inp):
        qc, kc, vc, gc = inp
        b = jnp.cumsum(gc, axis=2)
        decay = jnp.exp(jnp.where(mask, b[:, :, :, None, :] - b[:, :, None, :, :], -jnp.inf))
        A = jnp.einsum('bhtd,bhsd,bhtsd->bhts', qc, kc, decay)
        o = jnp.einsum('bhtd,bhde->bhte', qc * jnp.exp(b), S) + jnp.einsum('bhts,bhse->bhte', A, vc)
        bL = b[:, :, -1:, :]
        S = jnp.exp(bL[:, :, 0, :])[..., None] * S + jnp.einsum('bhsd,bhse->bhde', kc * jnp.exp(bL - b), vc)
        return S, o

    S_T, o = lax.scan(step, S0.astype(f32), tuple(_to_chunks(t, L) for t in (qh, kh, vh, gh)))
    o = _from_chunks(o).transpose(0, 2, 1, 3)
    out = head_rmsnorm(o, norm_g) * jax.nn.silu(og)
    return out.astype(p.dtype), S_T


def decoder_layer(x, rw_prev, rw_S, m_conv, m_C, m_n, m_m, g_S,
                  norm1_g, w_in, gate_b, rwkv_mu, rwkv_w0, rwkv_w2, rwkv_a0, rwkv_a2, rwkv_g2,
                  rwkv_k_k, rwkv_k_a, rwkv_r_k, rwkv_ln_g, rwkv_ln_b,
                  mlstm_conv_w, mlstm_conv_b, mlstm_i_b, mlstm_f_b, mlstm_norm_g,
                  gla_a2, gla_a_b, gla_norm_g, w_branch, w_out, norm2_g, ffn_w_gu, ffn_w_down):
    B, T, _ = x.shape
    h = rmsnorm(x, norm1_g)
    proj = h @ w_in
    p_r, p_m, p_g, p_gate = jnp.split(proj, IN_SPLITS, axis=-1)
    o_r, rw_S, rw_prev = rwkv7_branch(p_r, rw_prev, rw_S, rwkv_mu, rwkv_w0, rwkv_w2, rwkv_a0, rwkv_a2,
                                      rwkv_g2, rwkv_k_k, rwkv_k_a, rwkv_r_k, rwkv_ln_g, rwkv_ln_b)
    o_m, m_conv, m_C, m_n, m_m = mlstm_branch(p_m, m_conv, m_C, m_n, m_m, mlstm_conv_w, mlstm_conv_b,
                                              mlstm_i_b, mlstm_f_b, mlstm_norm_g)
    o_g, g_S = gla_branch(p_g, g_S, gla_a2, gla_a_b, gla_norm_g)
    branches = jnp.stack([o_r, o_m, o_g], axis=2)
    gates = jax.nn.sigmoid(p_gate.reshape(B, T, N_BRANCH, D_MODEL) + gate_b)
    widened = jnp.einsum('btjc,jcd->btjd', branches, w_branch)
    x = x + jnp.sum(gates * widened, axis=2) @ w_out
    gg, uu = jnp.split(rmsnorm(x, norm2_g) @ ffn_w_gu, 2, axis=-1)
    x = x + (jax.nn.silu(gg) * uu) @ ffn_w_down
    return x, (rw_prev, rw_S, m_conv, m_C, m_n, m_m, g_S)


def run_trunk(x, states, layer_params, final_norm_g):
    per_layer = []
    for l in range(DEPTH):
        x, new = decoder_layer(x, *[s[l] for s in states], *[w[l] for w in layer_params])
        per_layer.append(new)
    new_states = [jnp.stack([st[i] for st in per_layer], axis=0).astype(x.dtype) for i in range(len(states))]
    return rmsnorm(x, final_norm_g), new_states


def zero_states(batch, dtype):
    return (jnp.zeros((DEPTH, batch, R_COLS), dtype),
            jnp.zeros((DEPTH, batch, R_HEADS, R_HEAD, R_HEAD), dtype),
            jnp.zeros((DEPTH, batch, CONV_W - 1, 2 * BRANCH_W), dtype),
            jnp.zeros((DEPTH, batch, M_HEADS, M_DK, M_DV), dtype),
            jnp.zeros((DEPTH, batch, M_HEADS, M_DK), dtype),
            jnp.zeros((DEPTH, batch, M_HEADS), dtype),
            jnp.zeros((DEPTH, batch, G_HEADS, G_DK, G_DV), dtype))


def setup_inputs(seed: int = 0) -> dict:
    key = jax.random.key(seed)
    ks = iter(jax.random.split(key, 64))
    f32 = jnp.float32
    L = DEPTH

    def nrm(shape, scale):
        return scale * jax.random.normal(next(ks), shape, f32)

    def unif(shape):
        return jax.random.uniform(next(ks), shape, f32)

    return {
        'x_prompt': nrm((BATCH, SEQ, D_MODEL), 1.0),
        'x_sample': nrm((DEC_BATCH, DEC_SEQ, D_MODEL), 1.0),
        'state_rwkv_shift': nrm((L, DEC_BATCH, R_COLS), 1.0),
        'state_rwkv_wkv': nrm((L, DEC_BATCH, R_HEADS, R_HEAD, R_HEAD), 0.5),
        'state_mlstm_conv': nrm((L, DEC_BATCH, CONV_W - 1, 2 * BRANCH_W), 1.0),
        'state_mlstm_C': nrm((L, DEC_BATCH, M_HEADS, M_DK, M_DV), 0.5),
        'state_mlstm_n': nrm((L, DEC_BATCH, M_HEADS, M_DK), 0.5),
        'state_mlstm_m': nrm((L, DEC_BATCH, M_HEADS), 1.0),
        'state_gla_S': nrm((L, DEC_BATCH, G_HEADS, G_DK, G_DV), 0.5),
        'norm1_g': 1.0 + nrm((L, D_MODEL), 0.02),
        'w_in': nrm((L, D_MODEL, N_IN), D_MODEL ** -0.5),
        'gate_b': nrm((L, N_BRANCH, D_MODEL), 0.01),
        'rwkv_mu': unif((L, R_COLS)),
        'rwkv_w0': jnp.linspace(-6.5, -1.5, BRANCH_W, dtype=f32) + nrm((L, BRANCH_W), 0.1),
        'rwkv_w2': nrm((L, R_LORA_W, BRANCH_W), 0.1),
        'rwkv_a0': nrm((L, BRANCH_W), 0.1),
        'rwkv_a2': nrm((L, R_LORA_A, BRANCH_W), 0.1),
        'rwkv_g2': nrm((L, R_LORA_G, BRANCH_W), 0.25),
        'rwkv_k_k': 0.85 + nrm((L, BRANCH_W), 0.05),
        'rwkv_k_a': 1.0 + nrm((L, BRANCH_W), 0.05),
        'rwkv_r_k': -0.04 + nrm((L, R_HEADS, R_HEAD), 0.1),
        'rwkv_ln_g': 1.0 + nrm((L, BRANCH_W), 0.02),
        'rwkv_ln_b': nrm((L, BRANCH_W), 0.02),
        'mlstm_conv_w': nrm((L, CONV_W, 2 * BRANCH_W), CONV_W ** -0.5),
        'mlstm_conv_b': nrm((L, 2 * BRANCH_W), 0.02),
        'mlstm_i_b': nrm((L, M_HEADS), 0.1),
        'mlstm_f_b': jnp.linspace(3.0, 6.0, M_HEADS, dtype=f32) + nrm((L, M_HEADS), 0.1),
        'mlstm_norm_g': 1.0 + nrm((L, BRANCH_W), 0.02),
        'gla_a2': nrm((L, G_LR, G_HEADS * G_DK), G_LR ** -0.5),
        'gla_a_b': nrm((L, G_HEADS * G_DK), 0.1),
        'gla_norm_g': 1.0 + nrm((L, BRANCH_W), 0.02),
        'w_branch': nrm((L, N_BRANCH, BRANCH_W, D_MODEL), BRANCH_W ** -0.5),
        'w_out': nrm((L, D_MODEL, D_MODEL), D_MODEL ** -0.5),
        'norm2_g': 1.0 + nrm((L, D_MODEL), 0.02),
        'ffn_w_gu': nrm((L, D_MODEL, 2 * D_FF), D_MODEL ** -0.5),
        'ffn_w_down': nrm((L, D_FF, D_MODEL), D_FF ** -0.5),
        'final_norm_g': 1.0 + nrm((D_MODEL,), 0.02),
    }


def reference(x_prompt, x_sample, state_rwkv_shift, state_rwkv_wkv, state_mlstm_conv, state_mlstm_C,
              state_mlstm_n, state_mlstm_m, state_gla_S,
              norm1_g, w_in, gate_b, rwkv_mu, rwkv_w0, rwkv_w2, rwkv_a0, rwkv_a2, rwkv_g2,
              rwkv_k_k, rwkv_k_a, rwkv_r_k, rwkv_ln_g, rwkv_ln_b,
              mlstm_conv_w, mlstm_conv_b, mlstm_i_b, mlstm_f_b, mlstm_norm_g,
              gla_a2, gla_a_b, gla_norm_g, w_branch, w_out, norm2_g, ffn_w_gu, ffn_w_down,
              final_norm_g):
    layer_params = (norm1_g, w_in, gate_b, rwkv_mu, rwkv_w0, rwkv_w2, rwkv_a0, rwkv_a2, rwkv_g2,
                    rwkv_k_k, rwkv_k_a, rwkv_r_k, rwkv_ln_g, rwkv_ln_b,
                    mlstm_conv_w, mlstm_conv_b, mlstm_i_b, mlstm_f_b, mlstm_norm_g,
                    gla_a2, gla_a_b, gla_norm_g, w_branch, w_out, norm2_g, ffn_w_gu, ffn_w_down)
    y_prompt, p_states = run_trunk(x_prompt, zero_states(x_prompt.shape[0], x_prompt.dtype),
                                   layer_params, final_norm_g)
    p_shift, p_wkv, p_conv, p_C, p_n, p_m, p_gla = p_states
    sample_states = (state_rwkv_shift, state_rwkv_wkv, state_mlstm_conv, state_mlstm_C,
                     state_mlstm_n, state_mlstm_m, state_gla_S)
    y_sample, s_states = run_trunk(x_sample, sample_states, layer_params, final_norm_g)
    s_shift, s_wkv, s_conv, s_C, s_n, s_m, s_gla = s_states
    return (y_prompt, y_sample, p_shift, p_wkv, p_conv, p_C, p_n, p_m, p_gla,
            s_shift, s_wkv, s_conv, s_C, s_n, s_m, s_gla)
```

```python
import functools

import jax
import jax.numpy as jnp
from jax import lax
from jax.experimental import pallas as pl
from jax.experimental.pallas import tpu as pltpu

F32 = jnp.float32
BF16 = jnp.bfloat16

D_MODEL = 2048
DEPTH = 2
BRANCH_W = 1024
R_HEADS, R_HEAD = 16, 64
R_PAIRS = R_HEADS // 2
R_LORA = 64
R_COLS = 3 * BRANCH_W + 3 * R_LORA
R_GN_EPS = 64e-5
M_HEADS, M_DK = 4, 256
CONV_W = 4
G_HEADS, G_DK, G_DV = 4, 128, 256
G_LR = 16
G_GATE_NORM = 16.0
D_FF = 5632
EPS = 1e-6
NEG = -1e30

C_RWKV = 0
C_MQK = 3072
C_MV = 5120
C_MO = 6144
C_GQ = 7168
C_GV = 8192
C_GOG = 9216
C_GATE = 10240
C_SMALL = 16384
SMALL_W = 256
S_I, S_F, S_GXA = 192, 196, 200
N_PACKED = 16896

VMEM_LIMIT = 56 * 1024 * 1024


def _cparams(n_axes):
    return pltpu.CompilerParams(dimension_semantics=("arbitrary",) * n_axes,
                                vmem_limit_bytes=VMEM_LIMIT)


def _dot(a, b):
    return jnp.dot(a.astype(BF16), b.astype(BF16), preferred_element_type=F32)


def _dot_nt(a, b):
    return lax.dot_general(a.astype(BF16), b.astype(BF16), (((1,), (1,)), ((), ())),
                           preferred_element_type=F32)


def _dot_tn(a, b):
    return lax.dot_general(a.astype(BF16), b.astype(BF16), (((0,), (0,)), ((), ())),
                           preferred_element_type=F32)


def _dot_hi(a, b):
    return jnp.dot(a, b, precision=lax.Precision.HIGHEST, preferred_element_type=F32)


def _segsum(y, ones_blockdiag):
    hi = y.astype(BF16)
    lo = (y - hi.astype(F32)).astype(BF16)
    return (jnp.dot(hi, ones_blockdiag, preferred_element_type=F32)
            + jnp.dot(lo, ones_blockdiag, preferred_element_type=F32))


def _sigmoid(x):
    return 1.0 / (1.0 + jnp.exp(-x))


def _silu(x):
    return x * _sigmoid(x)


def _log_sigmoid(x):
    return -_softplus(-x)


def _softplus(x):
    return jnp.maximum(x, 0.0) + jnp.log(1.0 + jnp.exp(-jnp.abs(x)))


def _rms_mm_kernel(x_ref, g_ref, w_ref, o_ref, h_scr):
    @pl.when(pl.program_id(1) == 0)
    def _():
        x = x_ref[...]
        ms = jnp.mean(x * x, axis=-1, keepdims=True)
        h_scr[...] = (x * lax.rsqrt(ms + EPS) * g_ref[...]).astype(BF16)

    o_ref[...] = jnp.dot(h_scr[...], w_ref[...].astype(BF16), preferred_element_type=F32)


def _rms_matmul(x, g, w, *, tm, tn):
    m, k = x.shape
    n = w.shape[1]
    return pl.pallas_call(
        _rms_mm_kernel,
        out_shape=jax.ShapeDtypeStruct((m, n), F32),
        grid=(m // tm, n // tn),
        in_specs=[pl.BlockSpec((tm, k), lambda i, j: (i, 0)),
                  pl.BlockSpec((1, k), lambda i, j: (0, 0)),
                  pl.BlockSpec((k, tn), lambda i, j: (0, j))],
        out_specs=pl.BlockSpec((tm, tn), lambda i, j: (i, j)),
        scratch_shapes=[pltpu.VMEM((tm, k), BF16)],
        compiler_params=_cparams(2),
        name="rms_in_proj",
    )(x, g.reshape(1, k), w)


def _rms_swiglu_kernel(x_ref, g_ref, wg_ref, wu_ref, o_ref, h_scr):
    @pl.when(pl.program_id(1) == 0)
    def _():
        x = x_ref[...]
        ms = jnp.mean(x * x, axis=-1, keepdims=True)
        h_scr[...] = (x * lax.rsqrt(ms + EPS) * g_ref[...]).astype(BF16)

    h = h_scr[...]
    gg = jnp.dot(h, wg_ref[...].astype(BF16), preferred_element_type=F32)
    uu = jnp.dot(h, wu_ref[...].astype(BF16), preferred_element_type=F32)
    o_ref[...] = (_silu(gg) * uu).astype(BF16)


def _rms_swiglu(x, g, w_gu, *, tm, tn):
    m, k = x.shape
    nj = D_FF // tn
    return pl.pallas_call(
        _rms_swiglu_kernel,
        out_shape=jax.ShapeDtypeStruct((m, D_FF), BF16),
        grid=(m // tm, nj),
        in_specs=[pl.BlockSpec((tm, k), lambda i, j: (i, 0)),
                  pl.BlockSpec((1, k), lambda i, j: (0, 0)),
                  pl.BlockSpec((k, tn), lambda i, j: (0, j)),
                  pl.BlockSpec((k, tn), lambda i, j: (0, j + nj))],
        out_specs=pl.BlockSpec((tm, tn), lambda i, j: (i, j)),
        scratch_shapes=[pltpu.VMEM((tm, k), BF16)],
        compiler_params=_cparams(2),
        name="rms_ffn_swiglu",
    )(x, g.reshape(1, k), w_gu, w_gu)


def _mm_res_kernel(a_ref, w_ref, res_ref, o_ref, acc_ref, *, nk):
    kk = pl.program_id(2)

    @pl.when(kk == 0)
    def _():
        acc_ref[...] = jnp.zeros_like(acc_ref)

    acc_ref[...] += jnp.dot(a_ref[...], w_ref[...].astype(BF16), preferred_element_type=F32)

    @pl.when(kk == nk - 1)
    def _():
        o_ref[...] = acc_ref[...] + res_ref[...]


def _matmul_residual(a, w, res, *, tm, tn, tk):
    m, k = a.shape
    n = w.shape[1]
    nk = k // tk
    return pl.pallas_call(
        functools.partial(_mm_res_kernel, nk=nk),
        out_shape=jax.ShapeDtypeStruct((m, n), F32),
        grid=(m // tm, n // tn, nk),
        in_specs=[pl.BlockSpec((tm, tk), lambda i, j, kk: (i, kk)),
                  pl.BlockSpec((tk, tn), lambda i, j, kk: (kk, j)),
                  pl.BlockSpec((tm, tn), lambda i, j, kk: (i, j))],
        out_specs=pl.BlockSpec((tm, tn), lambda i, j, kk: (i, j)),
        scratch_shapes=[pltpu.VMEM((tm, tn), F32)],
        compiler_params=_cparams(3),
        name="matmul_residual",
    )(a, w, res)


def _merge_kernel(or_ref, om_ref, og_ref, wr_ref, wm_ref, wg_ref,
                  gr_ref, gm_ref, gg_ref, br_ref, bm_ref, bg_ref, o_ref):
    acc = _sigmoid(gr_ref[...] + br_ref[0]) * jnp.dot(
        or_ref[...], wr_ref[0].astype(BF16), preferred_element_type=F32)
    acc += _sigmoid(gm_ref[...] + bm_ref[0]) * jnp.dot(
        om_ref[...], wm_ref[0].astype(BF16), preferred_element_type=F32)
    acc += _sigmoid(gg_ref[...] + bg_ref[0]) * jnp.dot(
        og_ref[...], wg_ref[0].astype(BF16), preferred_element_type=F32)
    o_ref[...] = acc.astype(BF16)


def _merge(o_r, o_m, o_g, w_branch, proj, gate_b, *, tm, tn):
    m = o_r.shape[0]
    gate_blk = C_GATE // tn
    per = D_MODEL // tn
    o_spec = pl.BlockSpec((tm, BRANCH_W), lambda i, j: (i, 0))

    def w_spec(b):
        return pl.BlockSpec((1, BRANCH_W, tn), lambda i, j: (b, 0, j))

    def g_spec(b):
        return pl.BlockSpec((tm, tn), lambda i, j: (i, gate_blk + b * per + j))

    def b_spec(b):
        return pl.BlockSpec((1, 1, tn), lambda i, j: (b, 0, j))

    gate_b = gate_b.reshape(3, 1, D_MODEL)

    return pl.pallas_call(
        _merge_kernel,
        out_shape=jax.ShapeDtypeStruct((m, D_MODEL), BF16),
        grid=(m // tm, per),
        in_specs=[o_spec, o_spec, o_spec, w_spec(0), w_spec(1), w_spec(2),
                  g_spec(0), g_spec(1), g_spec(2), b_spec(0), b_spec(1), b_spec(2)],
        out_specs=pl.BlockSpec((tm, tn), lambda i, j: (i, j)),
        compiler_params=_cparams(2),
        name="gated_merge",
    )(o_r, o_m, o_g, w_branch, w_branch, w_branch, proj, proj, proj, gate_b, gate_b, gate_b)


def _rmsnorm_kernel(x_ref, g_ref, o_ref):
    x = x_ref[...]
    ms = jnp.mean(x * x, axis=-1, keepdims=True)
    o_ref[...] = x * lax.rsqrt(ms + EPS) * g_ref[...]


def _rmsnorm(x, g, *, tm):
    m, k = x.shape
    return pl.pallas_call(
        _rmsnorm_kernel,
        out_shape=jax.ShapeDtypeStruct((m, k), F32),
        grid=(m // tm,),
        in_specs=[pl.BlockSpec((tm, k), lambda i: (i, 0)),
                  pl.BlockSpec((1, k), lambda i: (0, 0))],
        out_specs=pl.BlockSpec((tm, k), lambda i: (i, 0)),
        compiler_params=_cparams(1),
        name="final_rmsnorm",
    )(x, g.reshape(1, k))


def _rwkv_prep_kernel(pr_ref, pk_ref, pv_ref, ps_ref, qr_ref, qk_ref, qv_ref, qs_ref,
                      mur_ref, muk_ref, muv_ref, mus_ref,
                      w0_ref, a0_ref, kk_ref, ka_ref, rk_ref,
                      w2_ref, a2_ref, g2_ref, j_ref,
                      r_out, k_out, v_out, kkn_out, b_out, ld_out, g_out, bon_out):
    def shift(p_ref, q_ref, mu_ref):
        p = p_ref[...]
        return p + (q_ref[...] - p) * mu_ref[...]

    xr = shift(pr_ref, qr_ref, mur_ref)
    xk = shift(pk_ref, qk_ref, muk_ref)
    xv = shift(pv_ref, qv_ref, muv_ref)
    xs = shift(ps_ref, qs_ref, mus_ref)

    w = -_softplus(-(w0_ref[...] + _dot(jnp.tanh(xs), w2_ref[...]))) - 0.5
    ld_out[...] = -jnp.exp(w)
    a = _sigmoid(a0_ref[...] + _dot(xs, a2_ref[...]))
    g_out[...] = _dot(_sigmoid(xs), g2_ref[...])

    ones_bd = j_ref[...]
    kkr = xk * kk_ref[...]
    k2 = xk * (1.0 + (a - 1.0) * ka_ref[...])
    rkk = xr * k2 * rk_ref[...]
    for p in range(R_PAIRS):
        sl = slice(p * 128, (p + 1) * 128)
        kb = kkr[:, sl]
        nrm = jnp.sqrt(_segsum(kb * kb, ones_bd))
        kn = kb / jnp.maximum(nrm, 1e-12)
        kkn_out[:, sl] = kn
        b_out[:, sl] = kn * a[:, sl]
        bon_out[:, sl] = _segsum(rkk[:, sl], ones_bd) * xv[:, sl]
    r_out[...] = xr
    k_out[...] = k2
    v_out[...] = xv


def _rwkv_prep(proj, pprev, lw, ones_bd, *, tm):
    m = proj.shape[0]
    big = lambda blk: pl.BlockSpec((tm, BRANCH_W), lambda i: (i, blk))
    vec = lambda blk: pl.BlockSpec((1, BRANCH_W), lambda i: (0, blk))
    full = lambda shape: pl.BlockSpec(shape, lambda i: (0, 0))
    out = jax.ShapeDtypeStruct((m, BRANCH_W), F32)
    return pl.pallas_call(
        _rwkv_prep_kernel,
        out_shape=[out] * 8,
        grid=(m // tm,),
        in_specs=[big(0), big(1), big(2),
                  pl.BlockSpec((tm, SMALL_W), lambda i: (i, C_SMALL // SMALL_W)),
                  big(0), big(1), big(2),
                  pl.BlockSpec((tm, SMALL_W), lambda i: (i, 3 * BRANCH_W // SMALL_W)),
                  vec(0), vec(1), vec(2),
                  pl.BlockSpec((1, SMALL_W), lambda i: (0, 3 * BRANCH_W // SMALL_W)),
                  vec(0), vec(0), vec(0), vec(0), vec(0),
                  full((SMALL_W, BRANCH_W)), full((SMALL_W, BRANCH_W)), full((SMALL_W, BRANCH_W)),
                  full((128, 128))],
        out_specs=[pl.BlockSpec((tm, BRANCH_W), lambda i: (i, 0))] * 8,
        compiler_params=_cparams(1),
        name="rwkv_prep",
    )(proj, proj, proj, proj, pprev, pprev, pprev, pprev,
      lw["mu_p"], lw["mu_p"], lw["mu_p"], lw["mu_p"],
      lw["w0"], lw["a0"], lw["k_k"], lw["k_a"], lw["r_k"],
      lw["w2p"], lw["a2p"], lw["g2p"], ones_bd)


def _rwkv_scan_kernel(r_ref, k_ref, v_ref, kk_ref, b_ref, ld_ref, g_ref, bon_ref,
                      lng_ref, lnb_ref, j_ref, s0_ref, o_ref, s_ref, *, L):
    c_id = pl.program_id(2)

    @pl.when(c_id == 0)
    def _():
        s_ref[0, 0] = s0_ref[0, 0]

    r = r_ref[0]
    k = k_ref[0]
    v = v_ref[0]
    kk = kk_ref[0]
    b = b_ref[0]
    ld = ld_ref[0]
    S = s_ref[0, 0]

    ti = lax.broadcasted_iota(jnp.int32, (L, L), 0)
    si = lax.broadcasted_iota(jnp.int32, (L, L), 1)
    cs = _dot_hi((si <= ti).astype(F32), ld)
    ec = jnp.exp(cs)
    enc = jnp.exp(-cs)
    c_last = cs[L - 1:L, :]
    e_tail = jnp.exp(c_last - cs)

    lane = lax.broadcasted_iota(jnp.int32, (L, 128), 1)
    head_a = lane < R_HEAD

    def stack(x):
        return jnp.concatenate([jnp.where(head_a, x, 0.0), jnp.where(head_a, 0.0, x)], axis=0)

    Rs = stack(r * ec)
    Bs = stack(kk * jnp.exp(cs - ld))
    Ks = stack(k * enc)
    As = stack(-(b * enc))
    Vs = stack(v)
    Kt = stack(k * e_tail)
    At = stack(-(b * e_tail))

    P2 = 2 * L
    ri = lax.broadcasted_iota(jnp.int32, (P2, P2), 0)
    ci = lax.broadcasted_iota(jnp.int32, (P2, P2), 1)
    strict = ri > ci
    incl = ri >= ci
    Nm = jnp.where(strict, _dot_nt(Bs, As), 0.0)
    Mbk = jnp.where(strict, _dot_nt(Bs, Ks), 0.0)
    Mra = jnp.where(incl, _dot_nt(Rs, As), 0.0)
    Mrk = jnp.where(incl, _dot_nt(Rs, Ks), 0.0)

    Tm = jnp.where(ri == ci, 1.0, 0.0) + Nm
    Pw = Nm
    span = 2
    while span < L:
        Pw = _dot(Pw, Pw)
        Tm = Tm + _dot(Tm, Pw)
        span *= 2

    Ub = _dot(Tm, Bs)
    Uv = _dot(Tm, _dot(Mbk, Vs))
    Ro = Rs + _dot(Mra, Ub)
    Oc = _dot(Mrk, Vs) + _dot(Mra, Uv)

    U = _dot_nt(Ub, S) + Uv
    O = _dot_nt(Ro, S) + Oc
    s_ref[0, 0] = S * jnp.exp(c_last) + _dot_tn(U, At) + _dot_tn(Vs, Kt)

    out = O[:L] + O[L:]
    ones_bd = j_ref[...]
    inv_n = 1.0 / R_HEAD
    mean = _segsum(out, ones_bd) * inv_n
    d = out - mean
    var = _segsum(d * d, ones_bd) * inv_n
    y = d * lax.rsqrt(var + R_GN_EPS) * lng_ref[...] + lnb_ref[...]
    o_ref[0] = ((y + bon_ref[0]) * g_ref[0]).astype(BF16)


def _rwkv_scan(seqs, g, bonus, ln_g, ln_b, ones_bd, s0_pairs, *, L):
    bsz, t, _ = seqs[0].shape
    seq_spec = pl.BlockSpec((1, L, 128), lambda bi, p, c: (bi, c, p))
    vec_spec = pl.BlockSpec((1, 128), lambda bi, p, c: (0, p))
    st_spec = pl.BlockSpec((1, 1, 128, 128), lambda bi, p, c: (bi, p, 0, 0))
    return pl.pallas_call(
        functools.partial(_rwkv_scan_kernel, L=L),
        out_shape=[jax.ShapeDtypeStruct((bsz, t, BRANCH_W), BF16),
                   jax.ShapeDtypeStruct((bsz, R_PAIRS, 128, 128), F32)],
        grid=(bsz, R_PAIRS, t // L),
        in_specs=[seq_spec] * 8 + [vec_spec, vec_spec,
                                   pl.BlockSpec((128, 128), lambda bi, p, c: (0, 0)), st_spec],
        out_specs=[seq_spec, st_spec],
        compiler_params=_cparams(3),
        name="rwkv_scan",
    )(*seqs, g, bonus, ln_g, ln_b, ones_bd, s0_pairs)


def _pairs_from_heads(s):
    bsz = s.shape[0]
    s = s.reshape(bsz, R_PAIRS, 2, R_HEAD, R_HEAD)
    z = jnp.zeros_like(s[:, :, 0])
    top = jnp.concatenate([s[:, :, 0], z], axis=-1)
    bot = jnp.concatenate([z, s[:, :, 1]], axis=-1)
    return jnp.concatenate([top, bot], axis=-2)


def _heads_from_pairs(sp):
    bsz = sp.shape[0]
    a = sp[:, :, :R_HEAD, :R_HEAD]
    b = sp[:, :, R_HEAD:, R_HEAD:]
    return jnp.stack([a, b], axis=2).reshape(bsz, R_HEADS, R_HEAD, R_HEAD)


def _rwkv_mixer(proj3, prev, s0, lw, ones_bd, *, L, tm):
    bsz, t, _ = proj3.shape
    rkv = proj3[:, :, :3 * BRANCH_W]
    small = proj3[:, :, C_SMALL:C_SMALL + SMALL_W]
    cur = jnp.concatenate([rkv, small], axis=-1)
    prev_row = jnp.concatenate(
        [prev, jnp.zeros((bsz, 3 * BRANCH_W + SMALL_W - R_COLS), F32)], axis=-1)[:, None, :]
    pprev = jnp.concatenate([prev_row, cur[:, :-1]], axis=1)
    new_shift = cur[:, -1, :R_COLS]

    m = bsz * t
    outs = _rwkv_prep(proj3.reshape(m, N_PACKED), pprev.reshape(m, -1), lw, ones_bd, tm=tm)
    outs = [o.reshape(bsz, t, BRANCH_W) for o in outs]
    tp = -(-t // L) * L
    if tp != t:
        outs = [jnp.pad(o, ((0, 0), (0, tp - t), (0, 0))) for o in outs]
    r, k2, v, kkn, b, ld, g, bonus = outs
    o_r, s_pairs = _rwkv_scan((r, k2, v, kkn, b, ld), g, bonus, lw["ln_g"], lw["ln_b"],
                              ones_bd, _pairs_from_heads(s0), L=L)
    return o_r[:, :t].reshape(m, BRANCH_W), _heads_from_pairs(s_pairs), new_shift


def _mlstm_kernel(q_ref, k_ref, v_ref, o_ref, sm_ref, cq_ref, ck_ref, wq_ref, wk_ref,
                  bq_ref, bk_ref, ib_ref, fb_ref, ng_ref, c0_ref, n0_ref, m0_ref,
                  out_ref, c_ref, n_ref, m_ref, qs_ref, ks_ref, *, L, t_valid):
    h_id = pl.program_id(1)
    c_id = pl.program_id(2)

    @pl.when(c_id == 0)
    def _():
        c_ref[0, 0] = c0_ref[0, 0]
        n_ref[0, 0] = n0_ref[0, 0]
        m_ref[0, 0] = m0_ref[0, 0]
        qs_ref[5:8, :] = cq_ref[0]
        ks_ref[5:8, :] = ck_ref[0]

    qs_ref[8:8 + L, :] = q_ref[0]
    ks_ref[8:8 + L, :] = k_ref[0]
    conv_q = bq_ref[...]
    conv_k = bk_ref[...]
    for j in range(CONV_W):
        conv_q = conv_q + qs_ref[5 + j:5 + j + L, :] * wq_ref[j:j + 1, :]
        conv_k = conv_k + ks_ref[5 + j:5 + j + L, :] * wk_ref[j:j + 1, :]
    if L >= CONV_W - 1:
        tail_q = qs_ref[5 + L:8 + L, :]
        tail_k = ks_ref[5 + L:8 + L, :]
        qs_ref[5:8, :] = tail_q
        ks_ref[5:8, :] = tail_k

    row = lax.broadcasted_iota(jnp.int32, (L, 1), 0)
    valid = (c_id * L + row) < t_valid
    q = jnp.where(valid, _silu(conv_q), 0.0)
    k = jnp.where(valid, _silu(conv_k) * (M_DK ** -0.5), 0.0)
    v = jnp.where(valid, v_ref[0], 0.0)

    sm = sm_ref[0]
    lane = lax.broadcasted_iota(jnp.int32, (L, SMALL_W), 1)
    i_pre = jnp.sum(jnp.where(lane == S_I + h_id, sm, 0.0), axis=-1, keepdims=True)
    f_pre = jnp.sum(jnp.where(lane == S_F + h_id, sm, 0.0), axis=-1, keepdims=True)
    hl = lax.broadcasted_iota(jnp.int32, (1, M_HEADS), 1)
    i_b = jnp.sum(jnp.where(hl == h_id, ib_ref[...], 0.0), axis=-1, keepdims=True)
    f_b = jnp.sum(jnp.where(hl == h_id, fb_ref[...], 0.0), axis=-1, keepdims=True)
    ig = jnp.where(valid, i_pre + i_b, NEG)
    lf = jnp.where(valid, _log_sigmoid(f_pre + f_b), 0.0)

    ti = lax.broadcasted_iota(jnp.int32, (L, L), 0)
    si = lax.broadcasted_iota(jnp.int32, (L, L), 1)
    causal = si <= ti
    ones_l = jnp.ones((L, L), F32)
    F = _dot_hi(causal.astype(F32), jnp.broadcast_to(lf, (L, L)))[:, 0:1]
    g_row = _dot_hi(ones_l, jnp.where(ti == si, jnp.broadcast_to(ig - F, (L, L)), 0.0))
    Dm = jnp.where(causal, F + g_row, NEG)

    C = c_ref[0, 0]
    n = n_ref[0, 0]
    m_prev = m_ref[0, 0]
    inter = F + m_prev
    m_t = jnp.maximum(inter, jnp.max(Dm, axis=-1, keepdims=True))
    w_inter = jnp.exp(inter - m_t)
    Sm = _dot_nt(q, k) * jnp.exp(Dm - m_t)
    num = w_inter * _dot(q, C) + _dot(Sm, v)
    den = w_inter * jnp.sum(q * n, axis=-1, keepdims=True) + jnp.sum(Sm, axis=-1, keepdims=True)
    hh = num / jnp.maximum(jnp.abs(den), jnp.exp(-m_t))

    FL = F[L - 1:L, :]
    g_s = FL - F + ig
    m_new = jnp.maximum(FL + m_prev, jnp.max(g_s, axis=0, keepdims=True))
    a_c = jnp.exp(FL + m_prev - m_new)
    kw = k * jnp.exp(g_s - m_new)
    c_ref[0, 0] = a_c * C + _dot_tn(kw, v)
    n_ref[0, 0] = a_c * n + jnp.sum(kw, axis=0, keepdims=True)
    m_ref[0, 0] = m_new

    hn = hh * lax.rsqrt(jnp.mean(hh * hh, axis=-1, keepdims=True) + EPS) * ng_ref[...]
    out_ref[0] = (_sigmoid(o_ref[0]) * hn).astype(BF16)


def _mlstm_mixer(proj3, conv_buf, c0, n0, m0, lw, *, L, t_valid):
    bsz, tp, _ = proj3.shape
    blk = lambda col: (lambda bi, h, c: (bi, c, col // M_DK + h))
    seq = lambda col: pl.BlockSpec((1, L, M_DK), blk(col))
    st_c = pl.BlockSpec((1, 1, M_DK, M_DK), lambda bi, h, c: (bi, h, 0, 0))
    st_n = pl.BlockSpec((1, 1, 1, M_DK), lambda bi, h, c: (bi, h, 0, 0))
    st_m = pl.BlockSpec((1, 1, 1, 1), lambda bi, h, c: (bi, h, 0, 0))
    conv_q = pl.BlockSpec((1, CONV_W - 1, M_DK), lambda bi, h, c: (bi, 0, h))
    conv_k = pl.BlockSpec((1, CONV_W - 1, M_DK), lambda bi, h, c: (bi, 0, M_HEADS + h))
    wq = pl.BlockSpec((CONV_W, M_DK), lambda bi, h, c: (0, h))
    wk = pl.BlockSpec((CONV_W, M_DK), lambda bi, h, c: (0, M_HEADS + h))
    bq = pl.BlockSpec((1, M_DK), lambda bi, h, c: (0, h))
    bk = pl.BlockSpec((1, M_DK), lambda bi, h, c: (0, M_HEADS + h))
    hb = pl.BlockSpec((1, M_HEADS), lambda bi, h, c: (0, 0))
    out, c_new, n_new, m_new = pl.pallas_call(
        functools.partial(_mlstm_kernel, L=L, t_valid=t_valid),
        out_shape=[jax.ShapeDtypeStruct((bsz, tp, BRANCH_W), BF16),
                   jax.ShapeDtypeStruct((bsz, M_HEADS, M_DK, M_DK), F32),
                   jax.ShapeDtypeStruct((bsz, M_HEADS, 1, M_DK), F32),
                   jax.ShapeDtypeStruct((bsz, M_HEADS, 1, 1), F32)],
        grid=(bsz, M_HEADS, tp // L),
        in_specs=[seq(C_MQK), seq(C_MQK + BRANCH_W), seq(C_MV), seq(C_MO),
                  pl.BlockSpec((1, L, SMALL_W), lambda bi, h, c: (bi, c, C_SMALL // SMALL_W)),
                  conv_q, conv_k, wq, wk, bq, bk, hb, hb, bq, st_c, st_n, st_m],
        out_specs=[pl.BlockSpec((1, L, M_DK), lambda bi, h, c: (bi, c, h)), st_c, st_n, st_m],
        scratch_shapes=[pltpu.VMEM((L + 8, M_DK), F32), pltpu.VMEM((L + 8, M_DK), F32)],
        compiler_params=_cparams(3),
        name="mlstm_scan",
    )(proj3, proj3, proj3, proj3, proj3, conv_buf, conv_buf,
      lw["conv_w"], lw["conv_w"], lw["conv_b"], lw["conv_b"], lw["i_b"], lw["f_b"],
      lw["m_norm_g"], c0, n0.reshape(bsz, M_HEADS, 1, M_DK), m0.reshape(bsz, M_HEADS, 1, 1))
    return out, c_new, n_new.reshape(bsz, M_HEADS, M_DK), m_new.reshape(bsz, M_HEADS)


def _gla_kernel(q_ref, k_ref, v_ref, og_ref, sm_ref, a2_ref, ab_ref, ng_ref, s0_ref,
                out_ref, s_ref, b_scr, qk_scr, *, L, t_valid):
    c_id = pl.program_id(2)

    @pl.when(c_id == 0)
    def _():
        s_ref[0, 0] = s0_ref[0, 0]

    row = lax.broadcasted_iota(jnp.int32, (L, 1), 0)
    valid = (c_id * L + row) < t_valid
    q = jnp.where(valid, q_ref[0] * (G_DK ** -0.5), 0.0)
    k = jnp.where(valid, k_ref[0], 0.0)
    v = jnp.where(valid, v_ref[0], 0.0)
    lg = _log_sigmoid(_dot(sm_ref[0], a2_ref[...]) + ab_ref[...]) * (1.0 / G_GATE_NORM)
    lg = jnp.where(valid, lg, 0.0)

    ti = lax.broadcasted_iota(jnp.int32, (L, L), 0)
    si = lax.broadcasted_iota(jnp.int32, (L, L), 1)
    b = _dot_hi((si <= ti).astype(F32), lg)
    b_scr[...] = b
    qk_scr[...] = q

    s_col = lax.broadcasted_iota(jnp.int32, (L, 1), 0)
    t_lane = lax.broadcasted_iota(jnp.int32, (L, L), 1)

    def body(t, at):
        bt = b_scr[pl.ds(t, 1), :]
        qt = qk_scr[pl.ds(t, 1), :]
        e = jnp.exp(jnp.where(s_col <= t, bt - b, NEG))
        col = jnp.sum(qt * k * e, axis=-1, keepdims=True)
        return jnp.where(t_lane == t, col, at)

    at = lax.fori_loop(0, L, body, jnp.zeros((L, L), F32))

    St = s_ref[0, 0]
    o = _dot_nt(q * jnp.exp(b), St) + _dot_tn(at, v)
    b_last = b[L - 1:L, :]
    s_ref[0, 0] = St * jnp.exp(b_last) + _dot_tn(v, k * jnp.exp(b_last - b))

    on = o * lax.rsqrt(jnp.mean(o * o, axis=-1, keepdims=True) + EPS) * ng_ref[...]
    out_ref[0] = (on * _silu(og_ref[0])).astype(BF16)


def _gla_mixer(proj3, s0, lw, *, L, t_valid):
    bsz, tp, _ = proj3.shape
    st = pl.BlockSpec((1, 1, G_DV, G_DK), lambda bi, h, c: (bi, h, 0, 0))
    out, s_t = pl.pallas_call(
        functools.partial(_gla_kernel, L=L, t_valid=t_valid),
        out_shape=[jax.ShapeDtypeStruct((bsz, tp, BRANCH_W), BF16),
                   jax.ShapeDtypeStruct((bsz, G_HEADS, G_DV, G_DK), F32)],
        grid=(bsz, G_HEADS, tp // L),
        in_specs=[pl.BlockSpec((1, L, G_DK), lambda bi, h, c: (bi, c, C_GQ // G_DK + h)),
                  pl.BlockSpec((1, L, G_DK), lambda bi, h, c: (bi, c, C_GQ // G_DK + G_HEADS + h)),
                  pl.BlockSpec((1, L, G_DV), lambda bi, h, c: (bi, c, C_GV // G_DV + h)),
                  pl.BlockSpec((1, L, G_DV), lambda bi, h, c: (bi, c, C_GOG // G_DV + h)),
                  pl.BlockSpec((1, L, SMALL_W), lambda bi, h, c: (bi, c, C_SMALL // SMALL_W)),
                  pl.BlockSpec((SMALL_W, G_DK), lambda bi, h, c: (0, h)),
                  pl.BlockSpec((1, G_DK), lambda bi, h, c: (0, h)),
                  pl.BlockSpec((1, G_DV), lambda bi, h, c: (0, h)),
                  st],
        out_specs=[pl.BlockSpec((1, L, G_DV), lambda bi, h, c: (bi, c, h)), st],
        scratch_shapes=[pltpu.VMEM((L, G_DK), F32), pltpu.VMEM((L, G_DK), F32)],
        compiler_params=_cparams(3),
        name="gla_scan",
    )(proj3, proj3, proj3, proj3, proj3, lw["g_a2p"], lw["g_a_b"], lw["g_norm_g"],
      jnp.swapaxes(s0, -1, -2))
    return out, jnp.swapaxes(s_t, -1, -2)


def _pack_w_in(w):
    r0, m0, g0, t0 = 0, R_COLS, R_COLS + 4104, R_COLS + 4104 + 3088
    pieces = [
        w[:, r0:r0 + 3072],
        w[:, m0:m0 + 2048], w[:, m0 + 2048:m0 + 3072], w[:, m0 + 3080:m0 + 4104],
        w[:, g0:g0 + 512], w[:, g0 + 512:g0 + 1024], w[:, g0 + 1024:g0 + 2048],
        w[:, g0 + 2064:g0 + 3088],
        w[:, t0:t0 + 3 * D_MODEL],
        w[:, r0 + 3072:r0 + 3264], w[:, m0 + 3072:m0 + 3080], w[:, g0 + 2048:g0 + 2064],
    ]
    used = sum(p.shape[1] for p in pieces)
    pieces.append(jnp.zeros((w.shape[0], N_PACKED - used), w.dtype))
    return jnp.concatenate(pieces, axis=1)


def _rows_padded(w, row0, total):
    return jnp.pad(w, ((row0, total - row0 - w.shape[0]), (0, 0)))


def _layer_weights(l, P):
    mu = P["rwkv_mu"][l]
    mu_p = jnp.concatenate([mu, jnp.zeros((3 * BRANCH_W + SMALL_W - R_COLS,), F32)]).reshape(1, -1)
    row = lambda a: a.reshape(1, -1)
    return {
        "norm1_g": P["norm1_g"][l], "w_in": _pack_w_in(P["w_in"][l]), "gate_b": P["gate_b"][l],
        "mu_p": mu_p, "w0": row(P["rwkv_w0"][l]), "a0": row(P["rwkv_a0"][l]),
        "k_k": row(P["rwkv_k_k"][l]), "k_a": row(P["rwkv_k_a"][l]), "r_k": row(P["rwkv_r_k"][l]),
        "w2p": _rows_padded(P["rwkv_w2"][l], 0, SMALL_W),
        "a2p": _rows_padded(P["rwkv_a2"][l], R_LORA, SMALL_W),
        "g2p": _rows_padded(P["rwkv_g2"][l], 2 * R_LORA, SMALL_W),
        "ln_g": row(P["rwkv_ln_g"][l]), "ln_b": row(P["rwkv_ln_b"][l]),
        "conv_w": P["mlstm_conv_w"][l], "conv_b": row(P["mlstm_conv_b"][l]),
        "i_b": row(P["mlstm_i_b"][l]), "f_b": row(P["mlstm_f_b"][l]),
        "m_norm_g": row(P["mlstm_norm_g"][l]),
        "g_a2p": _rows_padded(P["gla_a2"][l], S_GXA, SMALL_W), "g_a_b": row(P["gla_a_b"][l]),
        "g_norm_g": row(P["gla_norm_g"][l]),
        "w_branch": P["w_branch"][l], "w_out": P["w_out"][l], "norm2_g": P["norm2_g"][l],
        "w_gu": P["ffn_w_gu"][l], "w_down": P["ffn_w_down"][l],
    }


def _layer(x2, bsz, t, states, lw, ones_bd, cfg):
    rw_prev, rw_s, m_conv, m_c, m_n, m_m, g_s = states
    m = bsz * t
    L, tm = cfg["L"], cfg["tm"]
    proj = _rms_matmul(x2, lw["norm1_g"], lw["w_in"], tm=tm, tn=512)
    proj3 = proj.reshape(bsz, t, N_PACKED)

    o_r, rw_s_new, rw_prev_new = _rwkv_mixer(proj3, rw_prev, rw_s, lw, ones_bd, L=L,
                                             tm=cfg["tm_prep"])

    tp = -(-t // L) * L
    proj3p = proj3 if tp == t else jnp.pad(proj3, ((0, 0), (0, tp - t), (0, 0)))
    o_m, m_c_new, m_n_new, m_m_new = _mlstm_mixer(proj3p, m_conv, m_c, m_n, m_m, lw, L=L, t_valid=t)
    qk_raw = proj3[:, :, C_MQK:C_MQK + 2 * BRANCH_W]
    m_conv_new = jnp.concatenate([m_conv, qk_raw], axis=1)[:, -(CONV_W - 1):]
    o_g, g_s_new = _gla_mixer(proj3p, g_s, lw, L=L, t_valid=t)
    o_m = o_m[:, :t].reshape(m, BRANCH_W)
    o_g = o_g[:, :t].reshape(m, BRANCH_W)

    merged = _merge(o_r, o_m, o_g, lw["w_branch"], proj, lw["gate_b"], tm=cfg["tm_merge"], tn=512)
    x2 = _matmul_residual(merged, lw["w_out"], x2, tm=tm, tn=512, tk=D_MODEL)
    hidden = _rms_swiglu(x2, lw["norm2_g"], lw["w_gu"], tm=tm, tn=512)
    x2 = _matmul_residual(hidden, lw["w_down"], x2, tm=tm, tn=512, tk=1408)
    return x2, (rw_prev_new, rw_s_new, m_conv_new, m_c_new, m_n_new, m_m_new, g_s_new)


def _trunk(x, states, layer_ws, final_g, ones_bd, cfg):
    bsz, t, d = x.shape
    x2 = x.reshape(bsz * t, d)
    per_layer = []
    for l in range(DEPTH):
        x2, new = _layer(x2, bsz, t, [s[l] for s in states], layer_ws[l], ones_bd, cfg)
        per_layer.append(new)
    new_states = [jnp.stack([st[i] for st in per_layer], axis=0) for i in range(len(states))]
    y = _rmsnorm(x2, final_g, tm=cfg["tm_norm"]).reshape(bsz, t, d)
    return y, new_states


PROMPT_CFG = dict(L=64, tm=1024, tm_prep=256, tm_merge=512, tm_norm=512)
SAMPLE_CFG = dict(L=16,tm=128, tm_prep=128, tm_merge=128, tm_norm=128)


def _zero_states(bsz):
    return (jnp.zeros((DEPTH, bsz, R_COLS), F32),
            jnp.zeros((DEPTH, bsz, R_HEADS, R_HEAD, R_HEAD), F32),
            jnp.zeros((DEPTH, bsz, CONV_W - 1, 2 * BRANCH_W), F32),
            jnp.zeros((DEPTH, bsz, M_HEADS, M_DK, M_DK), F32),
            jnp.zeros((DEPTH, bsz, M_HEADS, M_DK), F32),
            jnp.zeros((DEPTH, bsz, M_HEADS), F32),
            jnp.zeros((DEPTH, bsz, G_HEADS, G_DK, G_DV), F32))


def kernel(x_prompt, x_sample, state_rwkv_shift, state_rwkv_wkv, state_mlstm_conv, state_mlstm_C, state_mlstm_n, state_mlstm_m, state_gla_S, norm1_g, w_in, gate_b, rwkv_mu, rwkv_w0, rwkv_w2, rwkv_a0, rwkv_a2, rwkv_g2, rwkv_k_k, rwkv_k_a, rwkv_r_k, rwkv_ln_g, rwkv_ln_b, mlstm_conv_w, mlstm_conv_b, mlstm_i_b, mlstm_f_b, mlstm_norm_g, gla_a2, gla_a_b, gla_norm_g, w_branch, w_out, norm2_g, ffn_w_gu, ffn_w_down, final_norm_g):
    P = dict(norm1_g=norm1_g, w_in=w_in, gate_b=gate_b, rwkv_mu=rwkv_mu, rwkv_w0=rwkv_w0,
             rwkv_w2=rwkv_w2, rwkv_a0=rwkv_a0, rwkv_a2=rwkv_a2, rwkv_g2=rwkv_g2,
             rwkv_k_k=rwkv_k_k, rwkv_k_a=rwkv_k_a, rwkv_r_k=rwkv_r_k, rwkv_ln_g=rwkv_ln_g,
             rwkv_ln_b=rwkv_ln_b, mlstm_conv_w=mlstm_conv_w, mlstm_conv_b=mlstm_conv_b,
             mlstm_i_b=mlstm_i_b, mlstm_f_b=mlstm_f_b, mlstm_norm_g=mlstm_norm_g,
             gla_a2=gla_a2, gla_a_b=gla_a_b, gla_norm_g=gla_norm_g, w_branch=w_branch,
             w_out=w_out, norm2_g=norm2_g, ffn_w_gu=ffn_w_gu, ffn_w_down=ffn_w_down)
    layer_ws = [_layer_weights(l, P) for l in range(DEPTH)]
    head_of_lane = jnp.arange(128) // R_HEAD
    ones_bd = (head_of_lane[:, None] == head_of_lane[None, :]).astype(BF16)

    y_p, p_states = _trunk(x_prompt, _zero_states(x_prompt.shape[0]), layer_ws, final_norm_g,
                           ones_bd, PROMPT_CFG)
    s_states = (state_rwkv_shift, state_rwkv_wkv, state_mlstm_conv, state_mlstm_C,
                state_mlstm_n, state_mlstm_m, state_gla_S)
    y_s, s_states = _trunk(x_sample, s_states, layer_ws, final_norm_g, ones_bd, SAMPLE_CFG)
    return (y_p, y_s, *p_states, *s_states)
```

```python
import functools

import jax
import jax.numpy as jnp
from jax import lax
from jax.experimental import pallas as pl
from jax.experimental.pallas import tpu as pltpu

F32 = jnp.float32
BF16 = jnp.bfloat16

D_MODEL = 2048
DEPTH = 2
BRANCH_W = 1024
R_HEADS, R_HEAD = 16, 64
R_PAIRS = R_HEADS // 2
R_LORA = 64
R_COLS = 3 * BRANCH_W + 3 * R_LORA
R_GN_EPS = 64e-5
M_HEADS, M_DK = 4, 256
CONV_W = 4
G_HEADS, G_DK, G_DV = 4, 128, 256
G_LR = 16
G_GATE_NORM = 16.0
D_FF = 5632
EPS = 1e-6
NEG = -1e30

C_RWKV = 0
C_MQK = 3072
C_MV = 5120
C_MO = 6144
C_GQ = 7168
C_GV = 8192
C_GOG = 9216
C_GATE = 10240
C_SMALL = 16384
SMALL_W = 256
S_I, S_F, S_GXA = 192, 196, 200
N_PACKED = 16896

VMEM_LIMIT = 56 * 1024 * 1024


def _cparams(n_axes):
    return pltpu.CompilerParams(dimension_semantics=("arbitrary",) * n_axes,
                                vmem_limit_bytes=VMEM_LIMIT)


def _dot(a, b):
    return jnp.dot(a.astype(BF16), b.astype(BF16), preferred_element_type=F32)


def _dot_nt(a, b):
    return lax.dot_general(a.astype(BF16), b.astype(BF16), (((1,), (1,)), ((), ())),
                           preferred_element_type=F32)


def _dot_tn(a, b):
    return lax.dot_general(a.astype(BF16), b.astype(BF16), (((0,), (0,)), ((), ())),
                           preferred_element_type=F32)


def _dot_hi(a, b):
    return jnp.dot(a, b, precision=lax.Precision.HIGHEST, preferred_element_type=F32)


def _segsum(y, ones_blockdiag):
    hi = y.astype(BF16)
    lo = (y - hi.astype(F32)).astype(BF16)
    return (jnp.dot(hi, ones_blockdiag, preferred_element_type=F32)
            + jnp.dot(lo, ones_blockdiag, preferred_element_type=F32))


def _sigmoid(x):
    return 1.0 / (1.0 + jnp.exp(-x))


def _silu(x):
    return x * _sigmoid(x)


def _log_sigmoid(x):
    return -_softplus(-x)


def _softplus(x):
    return jnp.maximum(x, 0.0) + jnp.log(1.0 + jnp.exp(-jnp.abs(x)))


def _rms_mm_kernel(x_ref, g_ref, w_ref, o_ref, h_scr):
    @pl.when(pl.program_id(1) == 0)
    def _():
        x = x_ref[...]
        ms = jnp.mean(x * x, axis=-1, keepdims=True)
        h_scr[...] = (x * lax.rsqrt(ms + EPS) * g_ref[...]).astype(BF16)

    o_ref[...] = jnp.dot(h_scr[...], w_ref[...].astype(BF16), preferred_element_type=F32)


def _rms_matmul(x, g, w, *, tm, tn):
    m, k = x.shape
    n = w.shape[1]
    return pl.pallas_call(
        _rms_mm_kernel,
        out_shape=jax.ShapeDtypeStruct((m, n), F32),
        grid=(m // tm, n // tn),
        in_specs=[pl.BlockSpec((tm, k), lambda i, j: (i, 0)),
                  pl.BlockSpec((1, k), lambda i, j: (0, 0)),
                  pl.BlockSpec((k, tn), lambda i, j: (0, j))],
        out_specs=pl.BlockSpec((tm, tn), lambda i, j: (i, j)),
        scratch_shapes=[pltpu.VMEM((tm, k), BF16)],
        compiler_params=_cparams(2),
        name="rms_in_proj",
    )(x, g.reshape(1, k), w)


def _rms_swiglu_kernel(x_ref, g_ref, wg_ref, wu_ref, o_ref, h_scr):
    @pl.when(pl.program_id(1) == 0)
    def _():
        x = x_ref[...]
        ms = jnp.mean(x * x, axis=-1, keepdims=True)
        h_scr[...] = (x * lax.rsqrt(ms + EPS) * g_ref[...]).astype(BF16)

    h = h_scr[...]
    gg = jnp.dot(h, wg_ref[...].astype(BF16), preferred_element_type=F32)
    uu = jnp.dot(h, wu_ref[...].astype(BF16), preferred_element_type=F32)
    o_ref[...] = (_silu(gg) * uu).astype(BF16)


def _rms_swiglu(x, g, w_gu, *, tm, tn):
    m, k = x.shape
    nj = D_FF // tn
    return pl.pallas_call(
        _rms_swiglu_kernel,
        out_shape=jax.ShapeDtypeStruct((m, D_FF), BF16),
        grid=(m // tm, nj),
        in_specs=[pl.BlockSpec((tm, k), lambda i, j: (i, 0)),
                  pl.BlockSpec((1, k), lambda i, j: (0, 0)),
                  pl.BlockSpec((k, tn), lambda i, j: (0, j)),
                  pl.BlockSpec((k, tn), lambda i, j: (0, j + nj))],
        out_specs=pl.BlockSpec((tm, tn), lambda i, j: (i, j)),
        scratch_shapes=[pltpu.VMEM((tm, k), BF16)],
        compiler_params=_cparams(2),
        name="rms_ffn_swiglu",
    )(x, g.reshape(1, k), w_gu, w_gu)


def _mm_res_kernel(a_ref, w_ref, res_ref, o_ref, acc_ref, *, nk):
    kk = pl.program_id(2)

    @pl.when(kk == 0)
    def _():
        acc_ref[...] = jnp.zeros_like(acc_ref)

    acc_ref[...] += jnp.dot(a_ref[...], w_ref[...].astype(BF16), preferred_element_type=F32)

    @pl.when(kk == nk - 1)
    def _():
        o_ref[...] = acc_ref[...] + res_ref[...]


def _matmul_residual(a, w, res, *, tm, tn, tk):
    m, k = a.shape
    n = w.shape[1]
    nk = k // tk
    return pl.pallas_call(
        functools.partial(_mm_res_kernel, nk=nk),
        out_shape=jax.ShapeDtypeStruct((m, n), F32),
        grid=(m // tm, n // tn, nk),
        in_specs=[pl.BlockSpec((tm, tk), lambda i, j, kk: (i, kk)),
                  pl.BlockSpec((tk, tn), lambda i, j, kk: (kk, j)),
                  pl.BlockSpec((tm, tn), lambda i, j, kk: (i, j))],
        out_specs=pl.BlockSpec((tm, tn), lambda i, j, kk: (i, j)),
        scratch_shapes=[pltpu.VMEM((tm, tn), F32)],
        compiler_params=_cparams(3),
        name="matmul_residual",
    )(a, w, res)


def _merge_kernel(or_ref, om_ref, og_ref, wr_ref, wm_ref, wg_ref,
                  gr_ref, gm_ref, gg_ref, br_ref, bm_ref, bg_ref, o_ref):
    acc = _sigmoid(gr_ref[...] + br_ref[0]) * jnp.dot(
        or_ref[...], wr_ref[0].astype(BF16), preferred_element_type=F32)
    acc += _sigmoid(gm_ref[...] + bm_ref[0]) * jnp.dot(
        om_ref[...], wm_ref[0].astype(BF16), preferred_element_type=F32)
    acc += _sigmoid(gg_ref[...] + bg_ref[0]) * jnp.dot(
        og_ref[...], wg_ref[0].astype(BF16), preferred_element_type=F32)
    o_ref[...] = acc.astype(BF16)


def _merge(o_r, o_m, o_g, w_branch, proj, gate_b, *, tm, tn):
    m = o_r.shape[0]
    gate_blk = C_GATE // tn
    per = D_MODEL // tn
    o_spec = pl.BlockSpec((tm, BRANCH_W), lambda i, j: (i, 0))

    def w_spec(b):
        return pl.BlockSpec((1, BRANCH_W, tn), lambda i, j: (b, 0, j))

    def g_spec(b):
        return pl.BlockSpec((tm, tn), lambda i, j: (i, gate_blk + b * per + j))

    def b_spec(b):
        return pl.BlockSpec((1, 1, tn), lambda i, j: (b, 0, j))

    gate_b = gate_b.reshape(3, 1, D_MODEL)

    return pl.pallas_call(
        _merge_kernel,
        out_shape=jax.ShapeDtypeStruct((m, D_MODEL), BF16),
        grid=(m // tm, per),
        in_specs=[o_spec, o_spec, o_spec, w_spec(0), w_spec(1), w_spec(2),
                  g_spec(0), g_spec(1), g_spec(2), b_spec(0), b_spec(1), b_spec(2)],
        out_specs=pl.BlockSpec((tm, tn), lambda i, j: (i, j)),
        compiler_params=_cparams(2),
        name="gated_merge",
    )(o_r, o_m, o_g, w_branch, w_branch, w_branch, proj, proj, proj, gate_b, gate_b, gate_b)


def _rmsnorm_kernel(x_ref, g_ref, o_ref):
    x = x_ref[...]
    ms = jnp.mean(x * x, axis=-1, keepdims=True)
    o_ref[...] = x * lax.rsqrt(ms + EPS) * g_ref[...]


def _rmsnorm(x, g, *, tm):
    m, k = x.shape
    return pl.pallas_call(
        _rmsnorm_kernel,
        out_shape=jax.ShapeDtypeStruct((m, k), F32),
        grid=(m // tm,),
        in_specs=[pl.BlockSpec((tm, k), lambda i: (i, 0)),
                  pl.BlockSpec((1, k), lambda i: (0, 0))],
        out_specs=pl.BlockSpec((tm, k), lambda i: (i, 0)),
        compiler_params=_cparams(1),
        name="final_rmsnorm",
    )(x, g.reshape(1, k))


def _rwkv_prep_kernel(pr_ref, pk_ref, pv_ref, ps_ref, qr_ref, qk_ref, qv_ref, qs_ref,
                      mur_ref, muk_ref, muv_ref, mus_ref,
                      w0_ref, a0_ref, kk_ref, ka_ref, rk_ref,
                      w2_ref, a2_ref, g2_ref, j_ref,
                      r_out, k_out, v_out, kkn_out, b_out, ld_out, g_out, bon_out):
    def shift(p_ref, q_ref, mu_ref):
        p = p_ref[...]
        return p + (q_ref[...] - p) * mu_ref[...]

    xr = shift(pr_ref, qr_ref, mur_ref)
    xk = shift(pk_ref, qk_ref, muk_ref)
    xv = shift(pv_ref, qv_ref, muv_ref)
    xs = shift(ps_ref, qs_ref, mus_ref)

    w = -_softplus(-(w0_ref[...] + _dot(jnp.tanh(xs), w2_ref[...]))) - 0.5
    ld_out[...] = -jnp.exp(w)
    a = _sigmoid(a0_ref[...] + _dot(xs, a2_ref[...]))
    g_out[...] = _dot(_sigmoid(xs), g2_ref[...])

    ones_bd = j_ref[...]
    kkr = xk * kk_ref[...]
    k2 = xk * (1.0 + (a - 1.0) * ka_ref[...])
    rkk = xr * k2 * rk_ref[...]
    for p in range(R_PAIRS):
        sl = slice(p * 128, (p + 1) * 128)
        kb = kkr[:, sl]
        nrm = jnp.sqrt(_segsum(kb * kb, ones_bd))
        kn = kb / jnp.maximum(nrm, 1e-12)
        kkn_out[:, sl] = kn
        b_out[:, sl] = kn * a[:, sl]
        bon_out[:, sl] = _segsum(rkk[:, sl], ones_bd) * xv[:, sl]
    r_out[...] = xr
    k_out[...] = k2
    v_out[...] = xv


def _rwkv_prep(proj, pprev, lw, ones_bd, *, tm):
    m = proj.shape[0]
    big = lambda blk: pl.BlockSpec((tm, BRANCH_W), lambda i: (i, blk))
    vec = lambda blk: pl.BlockSpec((1, BRANCH_W), lambda i: (0, blk))
    full = lambda shape: pl.BlockSpec(shape, lambda i: (0, 0))
    out = jax.ShapeDtypeStruct((m, BRANCH_W), F32)
    return pl.pallas_call(
        _rwkv_prep_kernel,
        out_shape=[out] * 8,
        grid=(m // tm,),
        in_specs=[big(0), big(1), big(2),
                  pl.BlockSpec((tm, SMALL_W), lambda i: (i, C_SMALL // SMALL_W)),
                  big(0), big(1), big(2),
                  pl.BlockSpec((tm, SMALL_W), lambda i: (i, 3 * BRANCH_W // SMALL_W)),
                  vec(0), vec(1), vec(2),
                  pl.BlockSpec((1, SMALL_W), lambda i: (0, 3 * BRANCH_W // SMALL_W)),
                  vec(0), vec(0), vec(0), vec(0), vec(0),
                  full((SMALL_W, BRANCH_W)), full((SMALL_W, BRANCH_W)), full((SMALL_W, BRANCH_W)),
                  full((128, 128))],
        out_specs=[pl.BlockSpec((tm, BRANCH_W), lambda i: (i, 0))] * 8,
        compiler_params=_cparams(1),
        name="rwkv_prep",
    )(proj, proj, proj, proj, pprev, pprev, pprev, pprev,
      lw["mu_p"], lw["mu_p"], lw["mu_p"], lw["mu_p"],
      lw["w0"], lw["a0"], lw["k_k"], lw["k_a"], lw["r_k"],
      lw["w2p"], lw["a2p"], lw["g2p"], ones_bd)


def _rwkv_scan_kernel(r_ref, k_ref, v_ref, kk_ref, b_ref, ld_ref, g_ref, bon_ref,
                      lng_ref, lnb_ref, j_ref, s0_ref, o_ref, s_ref, *, L, group):
    c_id = pl.program_id(1)

    @pl.when(c_id == 0)
    def _():
        s_ref[...] = s0_ref[...]

    ld_all = ld_ref[0]
    ti = lax.broadcasted_iota(jnp.int32, (L, L), 0)
    si = lax.broadcasted_iota(jnp.int32, (L, L), 1)
    cs_all = _dot_hi((si <= ti).astype(F32), ld_all)
    ec_all = jnp.exp(cs_all)
    enc_all = jnp.exp(-cs_all)
    ecm_all = jnp.exp(cs_all - ld_all)
    c_last_all = cs_all[L - 1:L, :]
    e_tail_all = jnp.exp(c_last_all - cs_all)
    g_last_all = jnp.exp(c_last_all)

    lane = lax.broadcasted_iota(jnp.int32, (L, 128), 1)
    head_a = lane < R_HEAD

    def stack(x):
        return jnp.concatenate([jnp.where(head_a, x, 0.0), jnp.where(head_a, 0.0, x)], axis=0)

    P2 = 2 * L
    ri = lax.broadcasted_iota(jnp.int32, (P2, P2), 0)
    ci = lax.broadcasted_iota(jnp.int32, (P2, P2), 1)
    strict = ri > ci
    incl = ri >= ci
    eye = jnp.where(ri == ci, 1.0, 0.0)
    ones_bd = j_ref[...]
    inv_n = 1.0 / R_HEAD

    cat0 = lambda a, b: jnp.concatenate([a, b], axis=0)
    cat1 = lambda a, b: jnp.concatenate([a, b], axis=1)

    for g0 in range(0, R_PAIRS, group):
        pairs = list(range(g0, g0 + group))
        sls = [slice(p * 128, (p + 1) * 128) for p in pairs]
        each = lambda f: [f(i) for i in range(group)]

        S = each(lambda i: s_ref[0, pairs[i]])
        Rs = each(lambda i: stack(r_ref[0, :, sls[i]] * ec_all[:, sls[i]]))
        Bs = each(lambda i: stack(kk_ref[0, :, sls[i]] * ecm_all[:, sls[i]]))
        Ks = each(lambda i: stack(k_ref[0, :, sls[i]] * enc_all[:, sls[i]]))
        As = each(lambda i: stack(-(b_ref[0, :, sls[i]] * enc_all[:, sls[i]])))
        Vs = each(lambda i: stack(v_ref[0, :, sls[i]]))
        Kt = each(lambda i: stack(k_ref[0, :, sls[i]] * e_tail_all[:, sls[i]]))
        At = each(lambda i: stack(-(b_ref[0, :, sls[i]] * e_tail_all[:, sls[i]])))

        if P2 % 128 == 0:
            sc = each(lambda i: _dot_nt(cat0(Bs[i], Rs[i]), cat0(As[i], Ks[i])))
            s_ba = each(lambda i: sc[i][:P2, :P2])
            s_bk = each(lambda i: sc[i][:P2, P2:])
            s_ra = each(lambda i: sc[i][P2:, :P2])
            s_rk = each(lambda i: sc[i][P2:, P2:])
        else:
            s_ba = each(lambda i: _dot_nt(Bs[i], As[i]))
            s_bk = each(lambda i: _dot_nt(Bs[i], Ks[i]))
            s_ra = each(lambda i: _dot_nt(Rs[i], As[i]))
            s_rk = each(lambda i: _dot_nt(Rs[i], Ks[i]))
        Nm = each(lambda i: jnp.where(strict, s_ba[i], 0.0))
        Mbk = each(lambda i: jnp.where(strict, s_bk[i], 0.0))
        Mra = each(lambda i: jnp.where(incl, s_ra[i], 0.0))
        Mrk = each(lambda i: jnp.where(incl, s_rk[i], 0.0))

        Tm = each(lambda i: eye + Nm[i])
        Pw = Nm
        span = 2
        while span < L:
            Pw = [_dot(x, x) for x in Pw]
            Tm = each(lambda i: Tm[i] + _dot(Tm[i], Pw[i]))
            span *= 2

        mv = each(lambda i: _dot(cat0(Mbk[i], Mrk[i]), Vs[i]))
        tb = each(lambda i: _dot(Tm[i], cat1(Bs[i], mv[i][:P2])))
        mu = each(lambda i: _dot(Mra[i], tb[i]))
        Ro = each(lambda i: Rs[i] + mu[i][:, :128])
        uo = each(lambda i: _dot_nt(cat0(tb[i][:, :128], Ro[i]), S[i]))
        U = each(lambda i: uo[i][:P2] + tb[i][:, 128:])
        O = each(lambda i: uo[i][P2:] + mv[i][P2:] + mu[i][:, 128:])
        for i, p in enumerate(pairs):
            s_ref[0, p] = S[i] * g_last_all[:, sls[i]] + _dot_tn(
                cat0(U[i], Vs[i]), cat0(At[i], Kt[i]))

        for i in range(group):
            sl = sls[i]
            out = O[i][:L] + O[i][L:]
            mean = _segsum(out, ones_bd) * inv_n
            d = out - mean
            var = _segsum(d * d, ones_bd) * inv_n
            y = d * lax.rsqrt(var + R_GN_EPS) * lng_ref[:, sl] + lnb_ref[:, sl]
            o_ref[0, :, sl] = ((y + bon_ref[0, :, sl]) * g_ref[0, :, sl]).astype(BF16)


def _rwkv_scan(seqs, g, bonus, ln_g, ln_b, ones_bd, s0_pairs, *, L):
    bsz, t, _ = seqs[0].shape
    seq_spec = pl.BlockSpec((1, L, BRANCH_W), lambda bi, c: (bi, c, 0))
    vec_spec = pl.BlockSpec((1, BRANCH_W), lambda bi, c: (0, 0))
    st_spec = pl.BlockSpec((1, R_PAIRS, 128, 128), lambda bi, c: (bi, 0, 0, 0))
    return pl.pallas_call(
        functools.partial(_rwkv_scan_kernel, L=L, group=R_PAIRS),
        out_shape=[jax.ShapeDtypeStruct((bsz, t, BRANCH_W), BF16),
                   jax.ShapeDtypeStruct((bsz, R_PAIRS, 128, 128), F32)],
        grid=(bsz, t // L),
        in_specs=[seq_spec] * 8 + [vec_spec, vec_spec,
                                   pl.BlockSpec((128, 128), lambda bi, c: (0, 0)), st_spec],
        out_specs=[seq_spec, st_spec],
        compiler_params=_cparams(2),
        name="rwkv_scan",
    )(*seqs, g, bonus, ln_g, ln_b, ones_bd, s0_pairs)


def _pairs_from_heads(s):
    bsz = s.shape[0]
    s = s.reshape(bsz, R_PAIRS, 2, R_HEAD, R_HEAD)
    z = jnp.zeros_like(s[:, :, 0])
    top = jnp.concatenate([s[:, :, 0], z], axis=-1)
    bot = jnp.concatenate([z, s[:, :, 1]], axis=-1)
    return jnp.concatenate([top, bot], axis=-2)


def _heads_from_pairs(sp):
    bsz = sp.shape[0]
    a = sp[:, :, :R_HEAD, :R_HEAD]
    b = sp[:, :, R_HEAD:, R_HEAD:]
    return jnp.stack([a, b], axis=2).reshape(bsz, R_HEADS, R_HEAD, R_HEAD)


def _rwkv_mixer(proj3, prev, s0, lw, ones_bd, *, L, tm):
    bsz, t, _ = proj3.shape
    rkv = proj3[:, :, :3 * BRANCH_W]
    small = proj3[:, :, C_SMALL:C_SMALL + SMALL_W]
    cur = jnp.concatenate([rkv, small], axis=-1)
    prev_row = jnp.concatenate(
        [prev, jnp.zeros((bsz, 3 * BRANCH_W + SMALL_W - R_COLS), F32)], axis=-1)[:, None, :]
    pprev = jnp.concatenate([prev_row, cur[:, :-1]], axis=1)
    new_shift = cur[:, -1, :R_COLS]

    m = bsz * t
    outs = _rwkv_prep(proj3.reshape(m, N_PACKED), pprev.reshape(m, -1), lw, ones_bd, tm=tm)
    outs = [o.reshape(bsz, t, BRANCH_W) for o in outs]
    tp = -(-t // L) * L
    if tp != t:
        outs = [jnp.pad(o, ((0, 0), (0, tp - t), (0, 0))) for o in outs]
    r, k2, v, kkn, b, ld, g, bonus = outs
    o_r, s_pairs = _rwkv_scan((r, k2, v, kkn, b, ld), g, bonus, lw["ln_g"], lw["ln_b"],
                              ones_bd, _pairs_from_heads(s0), L=L)
    return o_r[:, :t].reshape(m, BRANCH_W), _heads_from_pairs(s_pairs), new_shift


def _mlstm_kernel(q_ref, k_ref, v_ref, o_ref, sm_ref, cq_ref, ck_ref, wq_ref, wk_ref,
                  bq_ref, bk_ref, ib_ref, fb_ref, ng_ref, c0_ref, n0_ref, m0_ref,
                  out_ref, c_ref, n_ref, m_ref, qs_ref, ks_ref, *, L, t_valid):
    c_id = pl.program_id(1)
    W = BRANCH_W

    @pl.when(c_id == 0)
    def _():
        c_ref[...] = c0_ref[...]
        n_ref[...] = n0_ref[...]
        m_ref[...] = m0_ref[...]
        qs_ref[5:8, :] = cq_ref[0]
        ks_ref[5:8, :] = ck_ref[0]

    qs_ref[8:8 + L, :] = _rows(q_ref, L)
    ks_ref[8:8 + L, :] = _rows(k_ref, L)
    conv_q = bq_ref[...]
    conv_k = bk_ref[...]
    for j in range(CONV_W):
        conv_q = conv_q + qs_ref[5 + j:5 + j + L, :] * wq_ref[j:j + 1, :]
        conv_k = conv_k + ks_ref[5 + j:5 + j + L, :] * wk_ref[j:j + 1, :]
    if L >= CONV_W - 1:
        tail_q = qs_ref[5 + L:8 + L, :]
        tail_k = ks_ref[5 + L:8 + L, :]
        qs_ref[5:8, :] = tail_q
        ks_ref[5:8, :] = tail_k

    row = lax.broadcasted_iota(jnp.int32, (L, 1), 0)
    valid = (c_id * L + row) < t_valid
    q_all = jnp.where(valid, _silu(conv_q), 0.0)
    k_all = jnp.where(valid, _silu(conv_k) * (M_DK ** -0.5), 0.0)
    v_all = jnp.where(valid, _rows(v_ref, L), 0.0)
    gate_o = _sigmoid(_rows(o_ref, L))

    sm = _rows(sm_ref, L)
    lane = lax.broadcasted_iota(jnp.int32, (L, SMALL_W), 1)
    head_lane = lane < M_HEADS
    i_pre = jnp.where(head_lane, pltpu.roll(sm, SMALL_W - S_I, axis=1), 0.0)
    f_pre = jnp.where(head_lane, pltpu.roll(sm, SMALL_W - S_F, axis=1), 0.0)
    ig4 = jnp.where(valid & head_lane, i_pre + ib_ref[...], NEG)
    lf4 = jnp.where(valid & head_lane, _log_sigmoid(f_pre + fb_ref[...]), 0.0)

    ti = lax.broadcasted_iota(jnp.int32, (L, L), 0)
    si = lax.broadcasted_iota(jnp.int32, (L, L), 1)
    causal = si <= ti
    diag = ti == si
    ones_l = jnp.ones((L, L), F32)
    F4 = _dot_hi(causal.astype(F32), lf4)
    gmf4 = ig4 - F4

    for h in range(M_HEADS):
        sl = slice(h * M_DK, (h + 1) * M_DK)
        q, k, v = q_all[:, sl], k_all[:, sl], v_all[:, sl]
        pick = lane == h
        F = jnp.sum(jnp.where(pick, F4, 0.0), axis=-1, keepdims=True)
        ig = jnp.sum(jnp.where(pick, ig4, 0.0), axis=-1, keepdims=True)
        gmf = jnp.sum(jnp.where(pick, gmf4, 0.0), axis=-1, keepdims=True)
        g_row = _dot_hi(ones_l, jnp.where(diag, jnp.broadcast_to(gmf, (L, L)), 0.0))
        Dm = jnp.where(causal, F + g_row, NEG)

        C = c_ref[0, h]
        n = n_ref[0, h]
        m_prev = m_ref[0, h]
        inter = F + m_prev
        m_t = jnp.maximum(inter, jnp.max(Dm, axis=-1, keepdims=True))
        w_inter = jnp.exp(inter - m_t)
        Sm = _dot_nt(q, k) * jnp.exp(Dm - m_t)
        num = w_inter * _dot(q, C) + _dot(Sm, v)
        den = (w_inter * jnp.sum(q * n, axis=-1, keepdims=True)
               + jnp.sum(Sm, axis=-1, keepdims=True))
        hh = num / jnp.maximum(jnp.abs(den), jnp.exp(-m_t))

        FL = F[L - 1:L, :]
        g_s = FL - F + ig
        m_new = jnp.maximum(FL + m_prev, jnp.max(g_s, axis=0, keepdims=True))
        a_c = jnp.exp(FL + m_prev - m_new)
        kw = k * jnp.exp(g_s - m_new)
        c_ref[0, h] = a_c * C + _dot_tn(kw, v)
        n_ref[0, h] = a_c * n + jnp.sum(kw, axis=0, keepdims=True)
        m_ref[0, h] = m_new

        hn = hh * lax.rsqrt(jnp.mean(hh * hh, axis=-1, keepdims=True) + EPS) * ng_ref[:, sl]
        out_ref[0, :, sl] = (gate_o[:, sl] * hn).astype(BF16)


def _rows(ref, L):
    x = ref[0]
    if x.shape[0] == L:
        return x
    assert x.shape[0] == 1
    row = lax.broadcasted_iota(jnp.int32, (L, x.shape[1]), 0)
    return jnp.where(row == 0, x, 0.0)


def _mlstm_mixer(proj3, conv_buf, c0, n0, m0, lw, *, L):
    bsz, t, _ = proj3.shape
    tb = min(t, L)
    nc = -(-t // L)
    seq = lambda col: pl.BlockSpec((1, tb, BRANCH_W), lambda bi, c: (bi, c, col // BRANCH_W))
    st_c = pl.BlockSpec((1, M_HEADS, M_DK, M_DK), lambda bi, c: (bi, 0, 0, 0))
    st_n = pl.BlockSpec((1, M_HEADS, 1, M_DK), lambda bi, c: (bi, 0, 0, 0))
    st_m = pl.BlockSpec((1, M_HEADS, 1, 1), lambda bi, c: (bi, 0, 0, 0))
    conv = lambda blk: pl.BlockSpec((1, CONV_W - 1, BRANCH_W), lambda bi, c: (bi, 0, blk))
    cw = lambda blk: pl.BlockSpec((CONV_W, BRANCH_W), lambda bi, c: (0, blk))
    vec = lambda blk: pl.BlockSpec((1, BRANCH_W), lambda bi, c: (0, blk))
    hb = pl.BlockSpec((1, SMALL_W), lambda bi, c: (0, 0))
    pad_heads = lambda a: jnp.pad(a, ((0, 0), (0, SMALL_W - M_HEADS)))
    out, c_new, n_new, m_new = pl.pallas_call(
        functools.partial(_mlstm_kernel, L=L, t_valid=t),
        out_shape=[jax.ShapeDtypeStruct((bsz, nc * L, BRANCH_W), BF16),
                   jax.ShapeDtypeStruct((bsz, M_HEADS, M_DK, M_DK), F32),
                   jax.ShapeDtypeStruct((bsz, M_HEADS, 1, M_DK), F32),
                   jax.ShapeDtypeStruct((bsz, M_HEADS, 1, 1), F32)],
        grid=(bsz, nc),
        in_specs=[seq(C_MQK), seq(C_MQK + BRANCH_W), seq(C_MV), seq(C_MO),
                  pl.BlockSpec((1, tb, SMALL_W), lambda bi, c: (bi, c, C_SMALL // SMALL_W)),
                  conv(0), conv(1), cw(0), cw(1), vec(0), vec(1), hb, hb, vec(0),
                  st_c, st_n, st_m],
        out_specs=[pl.BlockSpec((1, L, BRANCH_W), lambda bi, c: (bi, c, 0)), st_c, st_n, st_m],
        scratch_shapes=[pltpu.VMEM((L + 8, BRANCH_W), F32), pltpu.VMEM((L + 8, BRANCH_W), F32)],
        compiler_params=_cparams(2),
        name="mlstm_scan",
    )(proj3, proj3, proj3, proj3, proj3, conv_buf, conv_buf,
      lw["conv_w"], lw["conv_w"], lw["conv_b"], lw["conv_b"],
      pad_heads(lw["i_b"]), pad_heads(lw["f_b"]),
      lw["m_norm_g"], c0, n0.reshape(bsz, M_HEADS, 1, M_DK), m0.reshape(bsz, M_HEADS, 1, 1))
    return out, c_new, n_new.reshape(bsz, M_HEADS, M_DK), m_new.reshape(bsz, M_HEADS)


def _gla_kernel(q_ref, k_ref, v_ref, og_ref, sm_ref, a2_ref, ab_ref, ng_ref, s0_ref,
                out_ref, s_ref, b_scr, q_scr, *, L, t_valid):
    c_id = pl.program_id(1)
    GW = G_HEADS * G_DK

    @pl.when(c_id == 0)
    def _():
        s_ref[...] = s0_ref[...]

    row = lax.broadcasted_iota(jnp.int32, (L, 1), 0)
    valid = (c_id * L + row) < t_valid
    q_all = jnp.where(valid, _rows(q_ref, L) * (G_DK ** -0.5), 0.0)
    k_all = jnp.where(valid, _rows(k_ref, L), 0.0)
    v_all = jnp.where(valid, _rows(v_ref, L), 0.0)
    gate_o = _silu(_rows(og_ref, L))
    lg = _log_sigmoid(_dot(_rows(sm_ref, L), a2_ref[...]) + ab_ref[...]) * (1.0 / G_GATE_NORM)
    lg = jnp.where(valid, lg, 0.0)

    ti = lax.broadcasted_iota(jnp.int32, (L, L), 0)
    si = lax.broadcasted_iota(jnp.int32, (L, L), 1)
    b_all = _dot_hi((si <= ti).astype(F32), lg)
    b_scr[...] = b_all
    q_scr[...] = q_all
    eb_all = jnp.exp(b_all)
    b_last_all = b_all[L - 1:L, :]
    g_last_all = jnp.exp(b_last_all)
    e_tail_all = jnp.exp(b_last_all - b_all)

    n_t = L if t_valid >= L else t_valid
    s_col = {rows: lax.broadcasted_iota(jnp.int32, (rows, 1), 0) for rows in range(8, L + 1, 8)}
    t_lane = {rows: lax.broadcasted_iota(jnp.int32, (rows, L), 1) for rows in range(8, L + 1, 8)}

    for h in range(G_HEADS):
        sl = slice(h * G_DK, (h + 1) * G_DK)
        sv = slice(h * G_DV, (h + 1) * G_DV)
        q, k, v, b = q_all[:, sl], k_all[:, sl], v_all[:, sv], b_all[:, sl]

        at = jnp.zeros((L, L), F32)
        for t in range(n_t):
            rows = 8 * (t // 8 + 1)
            bt = b_scr[t:t + 1, sl]
            qt = q_scr[t:t + 1, sl]
            e = jnp.exp(jnp.where(s_col[rows] <= t, bt - b[:rows], NEG))
            col = jnp.sum(qt * k[:rows] * e, axis=-1, keepdims=True)
            top = jnp.where(t_lane[rows] == t, col, at[:rows])
            at = top if rows == L else jnp.concatenate([top, at[rows:]], axis=0)

        St = s_ref[0, h]
        o = _dot_nt(q * eb_all[:, sl], St) + _dot_tn(at, v)
        s_ref[0, h] = St * g_last_all[:, sl] + _dot_tn(v, k * e_tail_all[:, sl])

        on = o * lax.rsqrt(jnp.mean(o * o, axis=-1, keepdims=True) + EPS) * ng_ref[:, sv]
        out_ref[0, :, sv] = (on * gate_o[:, sv]).astype(BF16)


def _gla_mixer(proj3, s0, lw, *, L):
    bsz, t, _ = proj3.shape
    tb = min(t, L)
    nc = -(-t // L)
    gw = G_HEADS * G_DK
    st = pl.BlockSpec((1, G_HEADS, G_DV, G_DK), lambda bi, c: (bi, 0, 0, 0))
    out, s_t = pl.pallas_call(
        functools.partial(_gla_kernel, L=L, t_valid=t),
        out_shape=[jax.ShapeDtypeStruct((bsz, nc * L, BRANCH_W), BF16),
                   jax.ShapeDtypeStruct((bsz, G_HEADS, G_DV, G_DK), F32)],
        grid=(bsz, nc),
        in_specs=[pl.BlockSpec((1, tb, gw), lambda bi, c: (bi, c, C_GQ // gw)),
                  pl.BlockSpec((1, tb, gw), lambda bi, c: (bi, c, C_GQ // gw + 1)),
                  pl.BlockSpec((1, tb, BRANCH_W), lambda bi, c: (bi, c, C_GV // BRANCH_W)),
                  pl.BlockSpec((1, tb, BRANCH_W), lambda bi, c: (bi, c, C_GOG // BRANCH_W)),
                  pl.BlockSpec((1, tb, SMALL_W), lambda bi, c: (bi, c, C_SMALL // SMALL_W)),
                  pl.BlockSpec((SMALL_W, gw), lambda bi, c: (0, 0)),
                  pl.BlockSpec((1, gw), lambda bi, c: (0, 0)),
                  pl.BlockSpec((1, BRANCH_W), lambda bi, c: (0, 0)),
                  st],
        out_specs=[pl.BlockSpec((1, L, BRANCH_W), lambda bi, c: (bi, c, 0)), st],
        scratch_shapes=[pltpu.VMEM((L, gw), F32), pltpu.VMEM((L, gw), F32)],
        compiler_params=_cparams(2),
        name="gla_scan",
    )(proj3, proj3, proj3, proj3, proj3, lw["g_a2p"], lw["g_a_b"], lw["g_norm_g"],
      jnp.swapaxes(s0, -1, -2))
    return out, jnp.swapaxes(s_t, -1, -2)


def _pack_w_in(w):
    r0, m0, g0, t0 = 0, R_COLS, R_COLS + 4104, R_COLS + 4104 + 3088
    pieces = [
        w[:, r0:r0 + 3072],
        w[:, m0:m0 + 2048], w[:, m0 + 2048:m0 + 3072], w[:, m0 + 3080:m0 + 4104],
        w[:, g0:g0 + 512], w[:, g0 + 512:g0 + 1024], w[:, g0 + 1024:g0 + 2048],
        w[:, g0 + 2064:g0 + 3088],
        w[:, t0:t0 + 3 * D_MODEL],
        w[:, r0 + 3072:r0 + 3264], w[:, m0 + 3072:m0 + 3080], w[:, g0 + 2048:g0 + 2064],
    ]
    used = sum(p.shape[1] for p in pieces)
    pieces.append(jnp.zeros((w.shape[0], N_PACKED - used), w.dtype))
    return jnp.concatenate(pieces, axis=1)


def _rows_padded(w, row0, total):
    return jnp.pad(w, ((row0, total - row0 - w.shape[0]), (0, 0)))


def _layer_weights(l, P):
    mu = P["rwkv_mu"][l]
    mu_p = jnp.concatenate([mu, jnp.zeros((3 * BRANCH_W + SMALL_W - R_COLS,), F32)]).reshape(1, -1)
    row = lambda a: a.reshape(1, -1)
    return {
        "norm1_g": P["norm1_g"][l], "w_in": _pack_w_in(P["w_in"][l]), "gate_b": P["gate_b"][l],
        "mu_p": mu_p, "w0": row(P["rwkv_w0"][l]), "a0": row(P["rwkv_a0"][l]),
        "k_k": row(P["rwkv_k_k"][l]), "k_a": row(P["rwkv_k_a"][l]), "r_k": row(P["rwkv_r_k"][l]),
        "w2p": _rows_padded(P["rwkv_w2"][l], 0, SMALL_W),
        "a2p": _rows_padded(P["rwkv_a2"][l], R_LORA, SMALL_W),
        "g2p": _rows_padded(P["rwkv_g2"][l], 2 * R_LORA, SMALL_W),
        "ln_g": row(P["rwkv_ln_g"][l]), "ln_b": row(P["rwkv_ln_b"][l]),
        "conv_w": P["mlstm_conv_w"][l], "conv_b": row(P["mlstm_conv_b"][l]),
        "i_b": row(P["mlstm_i_b"][l]), "f_b": row(P["mlstm_f_b"][l]),
        "m_norm_g": row(P["mlstm_norm_g"][l]),
        "g_a2p": _rows_padded(P["gla_a2"][l], S_GXA, SMALL_W), "g_a_b": row(P["gla_a_b"][l]),
        "g_norm_g": row(P["gla_norm_g"][l]),
        "w_branch": P["w_branch"][l], "w_out": P["w_out"][l], "norm2_g": P["norm2_g"][l],
        "w_gu": P["ffn_w_gu"][l], "w_down": P["ffn_w_down"][l],
    }


def _layer(x2, bsz, t, states, lw, ones_bd, cfg):
    rw_prev, rw_s, m_conv, m_c, m_n, m_m, g_s = states
    m = bsz * t
    L, tm = cfg["L"], cfg["tm"]
    proj = _rms_matmul(x2, lw["norm1_g"], lw["w_in"], tm=tm, tn=512)
    proj3 = proj.reshape(bsz, t, N_PACKED)

    o_r, rw_s_new, rw_prev_new = _rwkv_mixer(proj3, rw_prev, rw_s, lw, ones_bd, L=L,
                                             tm=cfg["tm_prep"])

    o_m, m_c_new, m_n_new, m_m_new = _mlstm_mixer(proj3, m_conv, m_c, m_n, m_m, lw, L=L)
    qk_tail = proj3[:, -min(t, CONV_W - 1):, C_MQK:C_MQK + 2 * BRANCH_W]
    m_conv_new = jnp.concatenate([m_conv, qk_tail], axis=1)[:, -(CONV_W - 1):]
    o_g, g_s_new = _gla_mixer(proj3, g_s, lw, L=L)
    o_m = o_m[:, :t].reshape(m, BRANCH_W)
    o_g = o_g[:, :t].reshape(m, BRANCH_W)

    merged = _merge(o_r, o_m, o_g, lw["w_branch"], proj, lw["gate_b"], tm=cfg["tm_merge"], tn=512)
    x2 = _matmul_residual(merged, lw["w_out"], x2, tm=tm, tn=512, tk=D_MODEL)
    hidden = _rms_swiglu(x2, lw["norm2_g"], lw["w_gu"], tm=tm, tn=512)
    x2 = _matmul_residual(hidden, lw["w_down"], x2, tm=tm, tn=512, tk=1408)
    return x2, (rw_prev_new, rw_s_new, m_conv_new, m_c_new, m_n_new, m_m_new, g_s_new)


def _trunk(x, states, layer_ws, final_g, ones_bd, cfg):
    bsz, t, d = x.shape
    x2 = x.reshape(bsz * t, d)
    per_layer = []
    for l in range(DEPTH):
        x2, new = _layer(x2, bsz, t, [s[l] for s in states], layer_ws[l], ones_bd, cfg)
        per_layer.append(new)
    new_states = [jnp.stack([st[i] for st in per_layer], axis=0) for i in range(len(states))]
    y = _rmsnorm(x2, final_g, tm=cfg["tm_norm"]).reshape(bsz, t, d)
    return y, new_states


PROMPT_CFG = dict(L=64, tm=1024, tm_prep=256, tm_merge=512, tm_norm=512)
SAMPLE_CFG = dict(L=16,tm=128, tm_prep=128, tm_merge=128, tm_norm=128)


def _zero_states(bsz):
    return (jnp.zeros((DEPTH, bsz, R_COLS), F32),
            jnp.zeros((DEPTH, bsz, R_HEADS, R_HEAD, R_HEAD), F32),
            jnp.zeros((DEPTH, bsz, CONV_W - 1, 2 * BRANCH_W), F32),
            jnp.zeros((DEPTH, bsz, M_HEADS, M_DK, M_DK), F32),
            jnp.zeros((DEPTH, bsz, M_HEADS, M_DK), F32),
            jnp.zeros((DEPTH, bsz, M_HEADS), F32),
            jnp.zeros((DEPTH, bsz, G_HEADS, G_DK, G_DV), F32))


def kernel(x_prompt, x_sample, state_rwkv_shift, state_rwkv_wkv, state_mlstm_conv, state_mlstm_C, state_mlstm_n, state_mlstm_m, state_gla_S, norm1_g, w_in, gate_b, rwkv_mu, rwkv_w0, rwkv_w2, rwkv_a0, rwkv_a2, rwkv_g2, rwkv_k_k, rwkv_k_a, rwkv_r_k, rwkv_ln_g, rwkv_ln_b, mlstm_conv_w, mlstm_conv_b, mlstm_i_b, mlstm_f_b, mlstm_norm_g, gla_a2, gla_a_b, gla_norm_g, w_branch, w_out, norm2_g, ffn_w_gu, ffn_w_down, final_norm_g):
    P = dict(norm1_g=norm1_g, w_in=w_in, gate_b=gate_b, rwkv_mu=rwkv_mu, rwkv_w0=rwkv_w0,
             rwkv_w2=rwkv_w2, rwkv_a0=rwkv_a0, rwkv_a2=rwkv_a2, rwkv_g2=rwkv_g2,
             rwkv_k_k=rwkv_k_k, rwkv_k_a=rwkv_k_a, rwkv_r_k=rwkv_r_k, rwkv_ln_g=rwkv_ln_g,
             rwkv_ln_b=rwkv_ln_b, mlstm_conv_w=mlstm_conv_w, mlstm_conv_b=mlstm_conv_b,
             mlstm_i_b=mlstm_i_b, mlstm_f_b=mlstm_f_b, mlstm_norm_g=mlstm_norm_g,
             gla_a2=gla_a2, gla_a_b=gla_a_b, gla_norm_g=gla_norm_g, w_branch=w_branch,
             w_out=w_out, norm2_g=norm2_g, ffn_w_gu=ffn_w_gu, ffn_w_down=ffn_w_down)
    layer_ws = [_layer_weights(l, P) for l in range(DEPTH)]
    head_of_lane = jnp.arange(128) // R_HEAD
    ones_bd = (head_of_lane[:, None] == head_of_lane[None, :]).astype(BF16)

    y_p, p_states = _trunk(x_prompt, _zero_states(x_prompt.shape[0]), layer_ws, final_norm_g,
                           ones_bd, PROMPT_CFG)
    s_states = (state_rwkv_shift, state_rwkv_wkv, state_mlstm_conv, state_mlstm_C,
                state_mlstm_n, state_mlstm_m, state_gla_S)
    y_s, s_states = _trunk(x_sample, s_states, layer_ws, final_norm_g, ones_bd, SAMPLE_CFG)
    return (y_p, y_s, *p_states, *s_states)
```

```python
import functools

import jax
import jax.numpy as jnp
from jax import lax
from jax.experimental import pallas as pl
from jax.experimental.pallas import tpu as pltpu

F32 = jnp.float32
BF16 = jnp.bfloat16

D_MODEL = 2048
DEPTH = 2
BRANCH_W = 1024
R_HEADS, R_HEAD = 16, 64
R_PAIRS = R_HEADS // 2
R_LORA = 64
R_COLS = 3 * BRANCH_W + 3 * R_LORA
R_GN_EPS = 64e-5
M_HEADS, M_DK = 4, 256
CONV_W = 4
G_HEADS, G_DK, G_DV = 4, 128, 256
G_LR = 16
G_GATE_NORM = 16.0
D_FF = 5632
EPS = 1e-6
NEG = -1e30

C_RWKV = 0
C_MQK = 3072
C_MV = 5120
C_MO = 6144
C_GQ = 7168
C_GV = 8192
C_GOG = 9216
C_GATE = 10240
C_SMALL = 16384
SMALL_W = 256
S_I, S_F, S_GXA = 192, 196, 200
N_PACKED = 16896

VMEM_LIMIT = 56 * 1024 * 1024


def _cparams(n_axes):
    return pltpu.CompilerParams(dimension_semantics=("arbitrary",) * n_axes,
                                vmem_limit_bytes=VMEM_LIMIT)


def _dot(a, b):
    return jnp.dot(a.astype(BF16), b.astype(BF16), preferred_element_type=F32)


def _dot_nt(a, b):
    return lax.dot_general(a.astype(BF16), b.astype(BF16), (((1,), (1,)), ((), ())),
                           preferred_element_type=F32)


def _dot_tn(a, b):
    return lax.dot_general(a.astype(BF16), b.astype(BF16), (((0,), (0,)), ((), ())),
                           preferred_element_type=F32)


def _dot_hi(a, b):
    return jnp.dot(a, b, precision=lax.Precision.HIGHEST, preferred_element_type=F32)


def _segsum(y, ones_blockdiag):
    hi = y.astype(BF16)
    lo = (y - hi.astype(F32)).astype(BF16)
    return (jnp.dot(hi, ones_blockdiag, preferred_element_type=F32)
            + jnp.dot(lo, ones_blockdiag, preferred_element_type=F32))


def _sigmoid(x):
    return 1.0 / (1.0 + jnp.exp(-x))


def _silu(x):
    return x * _sigmoid(x)


def _log_sigmoid(x):
    return -_softplus(-x)


def _softplus(x):
    return jnp.maximum(x, 0.0) + jnp.log(1.0 + jnp.exp(-jnp.abs(x)))


def _rms_mm_kernel(x_ref, g_ref, w_ref, o_ref, h_scr):
    @pl.when(pl.program_id(1) == 0)
    def _():
        x = x_ref[...]
        ms = jnp.mean(x * x, axis=-1, keepdims=True)
        h_scr[...] = (x * lax.rsqrt(ms + EPS) * g_ref[...]).astype(BF16)

    o_ref[...] = jnp.dot(h_scr[...], w_ref[0].astype(BF16), preferred_element_type=F32)


def _rms_matmul(x, g, w, l, *, tm, tn):
    m, k = x.shape
    n = w.shape[2]
    return pl.pallas_call(
        _rms_mm_kernel,
        out_shape=jax.ShapeDtypeStruct((m, n), F32),
        grid=(m // tm, n // tn),
        in_specs=[pl.BlockSpec((tm, k), lambda i, j: (i, 0)),
                  pl.BlockSpec((1, k), lambda i, j: (0, 0)),
                  pl.BlockSpec((1, k, tn), lambda i, j: (l, 0, j))],
        out_specs=pl.BlockSpec((tm, tn), lambda i, j: (i, j)),
        scratch_shapes=[pltpu.VMEM((tm, k), BF16)],
        compiler_params=_cparams(2),
        name="rms_in_proj",
    )(x, g.reshape(1, k), w)


def _rms_swiglu_kernel(x_ref, g_ref, wg_ref, wu_ref, o_ref, h_scr):
    @pl.when(pl.program_id(1) == 0)
    def _():
        x = x_ref[...]
        ms = jnp.mean(x * x, axis=-1, keepdims=True)
        h_scr[...] = (x * lax.rsqrt(ms + EPS) * g_ref[...]).astype(BF16)

    h = h_scr[...]
    gg = jnp.dot(h, wg_ref[0].astype(BF16), preferred_element_type=F32)
    uu = jnp.dot(h, wu_ref[0].astype(BF16), preferred_element_type=F32)
    o_ref[...] = (_silu(gg) * uu).astype(BF16)


def _rms_swiglu(x, g, w_gu, l, *, tm, tn):
    m, k = x.shape
    nj = D_FF // tn
    return pl.pallas_call(
        _rms_swiglu_kernel,
        out_shape=jax.ShapeDtypeStruct((m, D_FF), BF16),
        grid=(m // tm, nj),
        in_specs=[pl.BlockSpec((tm, k), lambda i, j: (i, 0)),
                  pl.BlockSpec((1, k), lambda i, j: (0, 0)),
                  pl.BlockSpec((1, k, tn), lambda i, j: (l, 0, j)),
                  pl.BlockSpec((1, k, tn), lambda i, j: (l, 0, j + nj))],
        out_specs=pl.BlockSpec((tm, tn), lambda i, j: (i, j)),
        scratch_shapes=[pltpu.VMEM((tm, k), BF16)],
        compiler_params=_cparams(2),
        name="rms_ffn_swiglu",
    )(x, g.reshape(1, k), w_gu, w_gu)


def _mm_res_kernel(a_ref, w_ref, res_ref, o_ref, acc_ref, *, nk):
    kk = pl.program_id(2)

    @pl.when(kk == 0)
    def _():
        acc_ref[...] = jnp.zeros_like(acc_ref)

    acc_ref[...] += jnp.dot(a_ref[...], w_ref[0].astype(BF16), preferred_element_type=F32)

    @pl.when(kk == nk - 1)
    def _():
        o_ref[...] = acc_ref[...] + res_ref[...]


def _matmul_residual(a, w, l, res, *, tm, tn, tk):
    m, k = a.shape
    n = w.shape[2]
    nk = k // tk
    return pl.pallas_call(
        functools.partial(_mm_res_kernel, nk=nk),
        out_shape=jax.ShapeDtypeStruct((m, n), F32),
        grid=(m // tm, n // tn, nk),
        in_specs=[pl.BlockSpec((tm, tk), lambda i, j, kk: (i, kk)),
                  pl.BlockSpec((1, tk, tn), lambda i, j, kk: (l, kk, j)),
                  pl.BlockSpec((tm, tn), lambda i, j, kk: (i, j))],
        out_specs=pl.BlockSpec((tm, tn), lambda i, j, kk: (i, j)),
        scratch_shapes=[pltpu.VMEM((tm, tn), F32)],
        compiler_params=_cparams(3),
        name="matmul_residual",
    )(a, w, res)


def _merge_kernel(or_ref, om_ref, og_ref, wr_ref, wm_ref, wg_ref,
                  gr_ref, gm_ref, gg_ref, br_ref, bm_ref, bg_ref, o_ref):
    acc = _sigmoid(gr_ref[...] + br_ref[0]) * jnp.dot(
        or_ref[...], wr_ref[0, 0].astype(BF16), preferred_element_type=F32)
    acc += _sigmoid(gm_ref[...] + bm_ref[0]) * jnp.dot(
        om_ref[...], wm_ref[0, 0].astype(BF16), preferred_element_type=F32)
    acc += _sigmoid(gg_ref[...] + bg_ref[0]) * jnp.dot(
        og_ref[...], wg_ref[0, 0].astype(BF16), preferred_element_type=F32)
    o_ref[...] = acc.astype(BF16)


def _merge(o_r, o_m, o_g, w_branch, l, proj, gate_b, *, tm, tn):
    m = o_r.shape[0]
    gate_blk = C_GATE // tn
    per = D_MODEL // tn
    o_spec = pl.BlockSpec((tm, BRANCH_W), lambda i, j: (i, 0))

    def w_spec(b):
        return pl.BlockSpec((1, 1, BRANCH_W, tn), lambda i, j: (l, b, 0, j))

    def g_spec(b):
        return pl.BlockSpec((tm, tn), lambda i, j: (i, gate_blk + b * per + j))

    def b_spec(b):
        return pl.BlockSpec((1, 1, tn), lambda i, j: (b, 0, j))

    gate_b = gate_b.reshape(3, 1, D_MODEL)

    return pl.pallas_call(
        _merge_kernel,
        out_shape=jax.ShapeDtypeStruct((m, D_MODEL), BF16),
        grid=(m // tm, per),
        in_specs=[o_spec, o_spec, o_spec, w_spec(0), w_spec(1), w_spec(2),
                  g_spec(0), g_spec(1), g_spec(2), b_spec(0), b_spec(1), b_spec(2)],
        out_specs=pl.BlockSpec((tm, tn), lambda i, j: (i, j)),
        compiler_params=_cparams(2),
        name="gated_merge",
    )(o_r, o_m, o_g, w_branch, w_branch, w_branch, proj, proj, proj, gate_b, gate_b, gate_b)


def _rmsnorm_kernel(x_ref, g_ref, o_ref):
    x = x_ref[...]
    ms = jnp.mean(x * x, axis=-1, keepdims=True)
    o_ref[...] = x * lax.rsqrt(ms + EPS) * g_ref[...]


def _rmsnorm(x, g, *, tm):
    m, k = x.shape
    return pl.pallas_call(
        _rmsnorm_kernel,
        out_shape=jax.ShapeDtypeStruct((m, k), F32),
        grid=(m // tm,),
        in_specs=[pl.BlockSpec((tm, k), lambda i: (i, 0)),
                  pl.BlockSpec((1, k), lambda i: (0, 0))],
        out_specs=pl.BlockSpec((tm, k), lambda i: (i, 0)),
        compiler_params=_cparams(1),
        name="final_rmsnorm",
    )(x, g.reshape(1, k))


def _rwkv_prep_kernel(pr_ref, pk_ref, pv_ref, ps_ref, qr_ref, qk_ref, qv_ref, qs_ref,
                      mur_ref, muk_ref, muv_ref, mus_ref,
                      w0_ref, a0_ref, kk_ref, ka_ref, rk_ref,
                      w2_ref, a2_ref, g2_ref, j_ref,
                      r_out, k_out, v_out, kkn_out, b_out, ld_out, g_out, bon_out,
                      *scratch, tm, explicit_prev):
    def shift(p_ref, q_ref, mu_ref, scr):
        p = p_ref[0]
        if explicit_prev:
            prev = q_ref[0]
        else:
            @pl.when(pl.program_id(1) == 0)
            def _():
                scr[7:8, :] = q_ref[0]

            scr[8:8 + tm, :] = p
            prev = scr[7:7 + tm, :]
            scr[7:8, :] = p[tm - 1:tm, :]
        return p + (prev - p) * mu_ref[...]

    scr = scratch if scratch else (None,) * 4
    xr = shift(pr_ref, qr_ref, mur_ref, scr[0])
    xk = shift(pk_ref, qk_ref, muk_ref, scr[1])
    xv = shift(pv_ref, qv_ref, muv_ref, scr[2])
    xs = shift(ps_ref, qs_ref, mus_ref, scr[3])

    w = -_softplus(-(w0_ref[...] + _dot(jnp.tanh(xs), w2_ref[...]))) - 0.5
    ld_out[0] = -jnp.exp(w)
    a = _sigmoid(a0_ref[...] + _dot(xs, a2_ref[...]))
    g_out[0] = _dot(_sigmoid(xs), g2_ref[...])

    ones_bd = j_ref[...]
    kkr = xk * kk_ref[...]
    k2 = xk * (1.0 + (a - 1.0) * ka_ref[...])
    rkk = xr * k2 * rk_ref[...]
    for p in range(R_PAIRS):
        sl = slice(p * 128, (p + 1) * 128)
        kb = kkr[:, sl]
        nrm = jnp.sqrt(_segsum(kb * kb, ones_bd))
        kn = kb / jnp.maximum(nrm, 1e-12)
        kkn_out[0, :, sl] = kn
        b_out[0, :, sl] = kn * a[:, sl]
        bon_out[0, :, sl] = _segsum(rkk[:, sl], ones_bd) * xv[:, sl]
    r_out[0] = xr
    k_out[0] = k2
    v_out[0] = xv


def _rwkv_prep(proj3, prev3, lw, ones_bd, *, tm):
    bsz, t, _ = proj3.shape
    explicit_prev = prev3.shape[1] == t
    tq = tm if explicit_prev else 1
    qmap = (lambda blk: (lambda bi, i: (bi, i, blk))) if explicit_prev else (
        lambda blk: (lambda bi, i: (bi, 0, blk)))
    small_blk = 3 * BRANCH_W // SMALL_W
    big = lambda blk: pl.BlockSpec((1, tm, BRANCH_W), lambda bi, i: (bi, i, blk))
    vec = lambda blk: pl.BlockSpec((1, BRANCH_W), lambda bi, i: (0, blk))
    full = lambda shape: pl.BlockSpec(shape, lambda bi, i: (0, 0))
    out = jax.ShapeDtypeStruct((bsz, t, BRANCH_W), F32)
    scratch = [] if explicit_prev else (
        [pltpu.VMEM((tm + 8, BRANCH_W), F32)] * 3 + [pltpu.VMEM((tm + 8, SMALL_W), F32)])
    return pl.pallas_call(
        functools.partial(_rwkv_prep_kernel, tm=tm, explicit_prev=explicit_prev),
        out_shape=[out] * 8,
        grid=(bsz, t // tm),
        in_specs=[big(0), big(1), big(2),
                  pl.BlockSpec((1, tm, SMALL_W), lambda bi, i: (bi, i, C_SMALL // SMALL_W)),
                  pl.BlockSpec((1, tq, BRANCH_W), qmap(0)),
                  pl.BlockSpec((1, tq, BRANCH_W), qmap(1)),
                  pl.BlockSpec((1, tq, BRANCH_W), qmap(2)),
                  pl.BlockSpec((1, tq, SMALL_W), qmap(small_blk)),
                  vec(0), vec(1), vec(2),
                  pl.BlockSpec((1, SMALL_W), lambda bi, i: (0, small_blk)),
                  vec(0), vec(0), vec(0), vec(0), vec(0),
                  full((SMALL_W, BRANCH_W)), full((SMALL_W, BRANCH_W)), full((SMALL_W, BRANCH_W)),
                  full((128, 128))],
        out_specs=[pl.BlockSpec((1, tm, BRANCH_W), lambda bi, i: (bi, i, 0))] * 8,
        scratch_shapes=scratch,
        compiler_params=_cparams(2),
        name="rwkv_prep",
    )(proj3, proj3, proj3, proj3, prev3, prev3, prev3, prev3,
      lw["mu_p"], lw["mu_p"], lw["mu_p"], lw["mu_p"],
      lw["w0"], lw["a0"], lw["k_k"], lw["k_a"], lw["r_k"],
      lw["w2p"], lw["a2p"], lw["g2p"], ones_bd)


def _rwkv_scan_kernel(r_ref, k_ref, v_ref, kk_ref, b_ref, ld_ref, g_ref, bon_ref,
                      lng_ref, lnb_ref, j_ref, s0_ref, o_ref, st_ref, s_ref, *, L, group):
    c_id = pl.program_id(1)

    @pl.when(c_id == 0)
    def _():
        z = jnp.zeros((R_HEAD, R_HEAD), F32)
        for p in range(R_PAIRS):
            top = jnp.concatenate([s0_ref[0, 0, 2 * p], z], axis=1)
            bot = jnp.concatenate([z, s0_ref[0, 0, 2 * p + 1]], axis=1)
            s_ref[p] = jnp.concatenate([top, bot], axis=0)

    ld_all = ld_ref[0]
    ti = lax.broadcasted_iota(jnp.int32, (L, L), 0)
    si = lax.broadcasted_iota(jnp.int32, (L, L), 1)
    cs_all = _dot_hi((si <= ti).astype(F32), ld_all)
    ec_all = jnp.exp(cs_all)
    enc_all = jnp.exp(-cs_all)
    ecm_all = jnp.exp(cs_all - ld_all)
    c_last_all = cs_all[L - 1:L, :]
    e_tail_all = jnp.exp(c_last_all - cs_all)
    g_last_all = jnp.exp(c_last_all)

    lane = lax.broadcasted_iota(jnp.int32, (L, 128), 1)
    head_a = lane < R_HEAD

    def stack(x):
        return jnp.concatenate([jnp.where(head_a, x, 0.0), jnp.where(head_a, 0.0, x)], axis=0)

    P2 = 2 * L
    ri = lax.broadcasted_iota(jnp.int32, (P2, P2), 0)
    ci = lax.broadcasted_iota(jnp.int32, (P2, P2), 1)
    strict = ri > ci
    incl = ri >= ci
    eye = jnp.where(ri == ci, 1.0, 0.0)
    ones_bd = j_ref[...]
    inv_n = 1.0 / R_HEAD

    cat0 = lambda a, b: jnp.concatenate([a, b], axis=0)
    cat1 = lambda a, b: jnp.concatenate([a, b], axis=1)

    for g0 in range(0, R_PAIRS, group):
        pairs = list(range(g0, g0 + group))
        sls = [slice(p * 128, (p + 1) * 128) for p in pairs]
        each = lambda f: [f(i) for i in range(group)]

        S = each(lambda i: s_ref[pairs[i]])
        Rs = each(lambda i: stack(r_ref[0, :, sls[i]] * ec_all[:, sls[i]]))
        Bs = each(lambda i: stack(kk_ref[0, :, sls[i]] * ecm_all[:, sls[i]]))
        Ks = each(lambda i: stack(k_ref[0, :, sls[i]] * enc_all[:, sls[i]]))
        As = each(lambda i: stack(-(b_ref[0, :, sls[i]] * enc_all[:, sls[i]])))
        Vs = each(lambda i: stack(v_ref[0, :, sls[i]]))
        Kt = each(lambda i: stack(k_ref[0, :, sls[i]] * e_tail_all[:, sls[i]]))
        At = each(lambda i: stack(-(b_ref[0, :, sls[i]] * e_tail_all[:, sls[i]])))

        if P2 % 128 == 0:
            sc = each(lambda i: _dot_nt(cat0(Bs[i], Rs[i]), cat0(As[i], Ks[i])))
            s_ba = each(lambda i: sc[i][:P2, :P2])
            s_bk = each(lambda i: sc[i][:P2, P2:])
            s_ra = each(lambda i: sc[i][P2:, :P2])
            s_rk = each(lambda i: sc[i][P2:, P2:])
        else:
            s_ba = each(lambda i: _dot_nt(Bs[i], As[i]))
            s_bk = each(lambda i: _dot_nt(Bs[i], Ks[i]))
            s_ra = each(lambda i: _dot_nt(Rs[i], As[i]))
            s_rk = each(lambda i: _dot_nt(Rs[i], Ks[i]))
        Nm = each(lambda i: jnp.where(strict, s_ba[i], 0.0))
        Mbk = each(lambda i: jnp.where(strict, s_bk[i], 0.0))
        Mra = each(lambda i: jnp.where(incl, s_ra[i], 0.0))
        Mrk = each(lambda i: jnp.where(incl, s_rk[i], 0.0))

        Tm = each(lambda i: eye + Nm[i])
        Pw = Nm
        span = 2
        while span < L:
            Pw = [_dot(x, x) for x in Pw]
            Tm = each(lambda i: Tm[i] + _dot(Tm[i], Pw[i]))
            span *= 2

        mv = each(lambda i: _dot(cat0(Mbk[i], Mrk[i]), Vs[i]))
        tb = each(lambda i: _dot(Tm[i], cat1(Bs[i], mv[i][:P2])))
        mu = each(lambda i: _dot(Mra[i], tb[i]))
        Ro = each(lambda i: Rs[i] + mu[i][:, :128])
        uo = each(lambda i: _dot_nt(cat0(tb[i][:, :128], Ro[i]), S[i]))
        U = each(lambda i: uo[i][:P2] + tb[i][:, 128:])
        O = each(lambda i: uo[i][P2:] + mv[i][P2:] + mu[i][:, 128:])
        for i, p in enumerate(pairs):
            s_ref[p] = S[i] * g_last_all[:, sls[i]] + _dot_tn(
                cat0(U[i], Vs[i]), cat0(At[i], Kt[i]))

        for i in range(group):
            sl = sls[i]
            out = O[i][:L] + O[i][L:]
            mean = _segsum(out, ones_bd) * inv_n
            d = out - mean
            var = _segsum(d * d, ones_bd) * inv_n
            y = d * lax.rsqrt(var + R_GN_EPS) * lng_ref[:, sl] + lnb_ref[:, sl]
            o_ref[0, :, sl] = ((y + bon_ref[0, :, sl]) * g_ref[0, :, sl]).astype(BF16)

    @pl.when(c_id == pl.num_programs(1) - 1)
    def _():
        for p in range(R_PAIRS):
            sp = s_ref[p]
            st_ref[0, 2 * p] = sp[:R_HEAD, :R_HEAD]
            st_ref[0, 2 * p + 1] = sp[R_HEAD:, R_HEAD:]


def _rwkv_scan(seqs, g, bonus, ln_g, ln_b, ones_bd, s0, l, *, L):
    bsz, t, _ = seqs[0].shape
    seq_spec = pl.BlockSpec((1, L, BRANCH_W), lambda bi, c: (bi, c, 0))
    vec_spec = pl.BlockSpec((1, BRANCH_W), lambda bi, c: (0, 0))
    return pl.pallas_call(
        functools.partial(_rwkv_scan_kernel, L=L, group=R_PAIRS),
        out_shape=[jax.ShapeDtypeStruct((bsz, t, BRANCH_W), BF16),
                   jax.ShapeDtypeStruct((bsz, R_HEADS, R_HEAD, R_HEAD), F32)],
        grid=(bsz, t // L),
        in_specs=[seq_spec] * 8 + [
            vec_spec, vec_spec, pl.BlockSpec((128, 128), lambda bi, c: (0, 0)),
            pl.BlockSpec((1, 1, R_HEADS, R_HEAD, R_HEAD), lambda bi, c: (l, bi, 0, 0, 0))],
        out_specs=[seq_spec,
                   pl.BlockSpec((1, R_HEADS, R_HEAD, R_HEAD), lambda bi, c: (bi, 0, 0, 0))],
        scratch_shapes=[pltpu.VMEM((R_PAIRS, 128, 128), F32)],
        compiler_params=_cparams(2),
        name="rwkv_scan",
    )(*seqs, g, bonus, ln_g, ln_b, ones_bd, s0)


def _rwkv_mixer(proj3, prev, s0, l, lw, ones_bd, *, L, tm):
    bsz, t, _ = proj3.shape
    last = proj3[:, -1]
    new_shift = jnp.concatenate(
        [last[:, :3 * BRANCH_W], last[:, C_SMALL:C_SMALL + R_COLS - 3 * BRANCH_W]], axis=-1)
    prev3 = jnp.pad(prev, ((0, 0), (0, 3 * BRANCH_W + SMALL_W - R_COLS)))[:, None, :]
    if t == 1:
        outs = _rwkv_prep(proj3.reshape(1, bsz, N_PACKED), prev3.reshape(1, bsz, -1), lw,
                          ones_bd, tm=tm)
        outs = [o.reshape(bsz, 1, BRANCH_W) for o in outs]
    else:
        outs = _rwkv_prep(proj3, prev3, lw, ones_bd, tm=tm)
    tp = -(-t // L) * L
    if tp != t:
        outs = [jnp.pad(o, ((0, 0), (0, tp - t), (0, 0))) for o in outs]
    r, k2, v, kkn, b, ld, g, bonus = outs
    o_r, s_new = _rwkv_scan((r, k2, v, kkn, b, ld), g, bonus, lw["ln_g"], lw["ln_b"],
                            ones_bd, s0, l, L=L)
    return o_r[:, :t].reshape(bsz * t, BRANCH_W), s_new, new_shift


def _mlstm_kernel(q_ref, k_ref, v_ref, o_ref, sm_ref, cq_ref, ck_ref, wq_ref, wk_ref,
                  bq_ref, bk_ref, ib_ref, fb_ref, ng_ref, c0_ref, n0_ref, m0_ref,
                  out_ref, c_ref, n_ref, m_ref, qs_ref, ks_ref, *, L, t_valid):
    c_id = pl.program_id(1)
    W = BRANCH_W

    @pl.when(c_id == 0)
    def _():
        c_ref[...] = c0_ref[0]
        n_ref[...] = n0_ref[0]
        m_ref[...] = m0_ref[0]
        qs_ref[5:8, :] = cq_ref[0, 0]
        ks_ref[5:8, :] = ck_ref[0, 0]

    qs_ref[8:8 + L, :] = _rows(q_ref, L)
    ks_ref[8:8 + L, :] = _rows(k_ref, L)
    conv_q = bq_ref[...]
    conv_k = bk_ref[...]
    for j in range(CONV_W):
        conv_q = conv_q + qs_ref[5 + j:5 + j + L, :] * wq_ref[j:j + 1, :]
        conv_k = conv_k + ks_ref[5 + j:5 + j + L, :] * wk_ref[j:j + 1, :]
    if L >= CONV_W - 1:
        tail_q = qs_ref[5 + L:8 + L, :]
        tail_k = ks_ref[5 + L:8 + L, :]
        qs_ref[5:8, :] = tail_q
        ks_ref[5:8, :] = tail_k

    row = lax.broadcasted_iota(jnp.int32, (L, 1), 0)
    valid = (c_id * L + row) < t_valid
    q_all = jnp.where(valid, _silu(conv_q), 0.0)
    k_all = jnp.where(valid, _silu(conv_k) * (M_DK ** -0.5), 0.0)
    v_all = jnp.where(valid, _rows(v_ref, L), 0.0)
    gate_o = _sigmoid(_rows(o_ref, L))

    sm = _rows(sm_ref, L)
    lane = lax.broadcasted_iota(jnp.int32, (L, SMALL_W), 1)
    head_lane = lane < M_HEADS
    i_pre = jnp.where(head_lane, pltpu.roll(sm, SMALL_W - S_I, axis=1), 0.0)
    f_pre = jnp.where(head_lane, pltpu.roll(sm, SMALL_W - S_F, axis=1), 0.0)
    ig4 = jnp.where(valid & head_lane, i_pre + ib_ref[...], NEG)
    lf4 = jnp.where(valid & head_lane, _log_sigmoid(f_pre + fb_ref[...]), 0.0)

    ti = lax.broadcasted_iota(jnp.int32, (L, L), 0)
    si = lax.broadcasted_iota(jnp.int32, (L, L), 1)
    causal = si <= ti
    diag = ti == si
    ones_l = jnp.ones((L, L), F32)
    F4 = _dot_hi(causal.astype(F32), lf4)
    gmf4 = ig4 - F4

    for h in range(M_HEADS):
        sl = slice(h * M_DK, (h + 1) * M_DK)
        q, k, v = q_all[:, sl], k_all[:, sl], v_all[:, sl]
        pick = lane == h
        F = jnp.sum(jnp.where(pick, F4, 0.0), axis=-1, keepdims=True)
        ig = jnp.sum(jnp.where(pick, ig4, 0.0), axis=-1, keepdims=True)
        gmf = jnp.sum(jnp.where(pick, gmf4, 0.0), axis=-1, keepdims=True)
        g_row = _dot_hi(ones_l, jnp.where(diag, jnp.broadcast_to(gmf, (L, L)), 0.0))
        Dm = jnp.where(causal, F + g_row, NEG)

        C = c_ref[0, h]
        n = n_ref[0, h]
        m_prev = m_ref[0, h]
        inter = F + m_prev
        m_t = jnp.maximum(inter, jnp.max(Dm, axis=-1, keepdims=True))
        w_inter = jnp.exp(inter - m_t)
        Sm = _dot_nt(q, k) * jnp.exp(Dm - m_t)
        num = w_inter * _dot(q, C) + _dot(Sm, v)
        den = (w_inter * jnp.sum(q * n, axis=-1, keepdims=True)
               + jnp.sum(Sm, axis=-1, keepdims=True))
        hh = num / jnp.maximum(jnp.abs(den), jnp.exp(-m_t))

        FL = F[L - 1:L, :]
        g_s = FL - F + ig
        m_new = jnp.maximum(FL + m_prev, jnp.max(g_s, axis=0, keepdims=True))
        a_c = jnp.exp(FL + m_prev - m_new)
        kw = k * jnp.exp(g_s - m_new)
        c_ref[0, h] = a_c * C + _dot_tn(kw, v)
        n_ref[0, h] = a_c * n + jnp.sum(kw, axis=0, keepdims=True)
        m_ref[0, h] = m_new

        hn = hh * lax.rsqrt(jnp.mean(hh * hh, axis=-1, keepdims=True) + EPS) * ng_ref[:, sl]
        out_ref[0, :, sl] = (gate_o[:, sl] * hn).astype(BF16)


def _rows(ref, L):
    x = ref[0]
    if x.shape[0] == L:
        return x
    assert x.shape[0] == 1
    row = lax.broadcasted_iota(jnp.int32, (L, x.shape[1]), 0)
    return jnp.where(row == 0, x, 0.0)


def _mlstm_mixer(proj3, conv_buf, c0, n0, m0, l, lw, *, L):
    bsz, t, _ = proj3.shape
    tb = min(t, L)
    nc = -(-t // L)
    seq = lambda col: pl.BlockSpec((1, tb, BRANCH_W), lambda bi, c: (bi, c, col // BRANCH_W))
    st_c = pl.BlockSpec((1, M_HEADS, M_DK, M_DK), lambda bi, c: (bi, 0, 0, 0))
    st_n = pl.BlockSpec((1, M_HEADS, 1, M_DK), lambda bi, c: (bi, 0, 0, 0))
    st_m = pl.BlockSpec((1, M_HEADS, 1, 1), lambda bi, c: (bi, 0, 0, 0))
    in_c = pl.BlockSpec((1, 1, M_HEADS, M_DK, M_DK), lambda bi, c: (l, bi, 0, 0, 0))
    in_n = pl.BlockSpec((1, 1, M_HEADS, 1, M_DK), lambda bi, c: (l, bi, 0, 0, 0))
    in_m = pl.BlockSpec((1, 1, M_HEADS, 1, 1), lambda bi, c: (l, bi, 0, 0, 0))
    conv = lambda blk: pl.BlockSpec((1, 1, CONV_W - 1, BRANCH_W), lambda bi, c: (l, bi, 0, blk))
    cw = lambda blk: pl.BlockSpec((CONV_W, BRANCH_W), lambda bi, c: (0, blk))
    vec = lambda blk: pl.BlockSpec((1, BRANCH_W), lambda bi, c: (0, blk))
    hb = pl.BlockSpec((1, SMALL_W), lambda bi, c: (0, 0))
    pad_heads = lambda a: jnp.pad(a, ((0, 0), (0, SMALL_W - M_HEADS)))
    out, c_new, n_new, m_new = pl.pallas_call(
        functools.partial(_mlstm_kernel, L=L, t_valid=t),
        out_shape=[jax.ShapeDtypeStruct((bsz, nc * L, BRANCH_W), BF16),
                   jax.ShapeDtypeStruct((bsz, M_HEADS, M_DK, M_DK), F32),
                   jax.ShapeDtypeStruct((bsz, M_HEADS, 1, M_DK), F32),
                   jax.ShapeDtypeStruct((bsz, M_HEADS, 1, 1), F32)],
        grid=(bsz, nc),
        in_specs=[seq(C_MQK), seq(C_MQK + BRANCH_W), seq(C_MV), seq(C_MO),
                  pl.BlockSpec((1, tb, SMALL_W), lambda bi, c: (bi, c, C_SMALL // SMALL_W)),
                  conv(0), conv(1), cw(0), cw(1), vec(0), vec(1), hb, hb, vec(0),
                  in_c, in_n, in_m],
        out_specs=[pl.BlockSpec((1, L, BRANCH_W), lambda bi, c: (bi, c, 0)), st_c, st_n, st_m],
        scratch_shapes=[pltpu.VMEM((L + 8, BRANCH_W), F32), pltpu.VMEM((L + 8, BRANCH_W), F32)],
        compiler_params=_cparams(2),
        name="mlstm_scan",
    )(proj3, proj3, proj3, proj3, proj3, conv_buf, conv_buf,
      lw["conv_w"], lw["conv_w"], lw["conv_b"], lw["conv_b"],
      pad_heads(lw["i_b"]), pad_heads(lw["f_b"]),
      lw["m_norm_g"], c0, n0.reshape(DEPTH, bsz, M_HEADS, 1, M_DK),
      m0.reshape(DEPTH, bsz, M_HEADS, 1, 1))
    return out, c_new, n_new.reshape(bsz, M_HEADS, M_DK), m_new.reshape(bsz, M_HEADS)


def _gla_kernel(q_ref, k_ref, v_ref, og_ref, sm_ref, a2_ref, ab_ref, ng_ref, s0_ref,
                out_ref, s_ref, b_scr, q_scr, *, L, t_valid):
    c_id = pl.program_id(1)

    @pl.when(c_id == 0)
    def _():
        s_ref[...] = s0_ref[0]

    row = lax.broadcasted_iota(jnp.int32, (L, 1), 0)
    valid = (c_id * L + row) < t_valid
    q_all = jnp.where(valid, _rows(q_ref, L) * (G_DK ** -0.5), 0.0)
    k_all = jnp.where(valid, _rows(k_ref, L), 0.0)
    v_all = jnp.where(valid, _rows(v_ref, L), 0.0)
    gate_o = _silu(_rows(og_ref, L))
    lg = _log_sigmoid(_dot(_rows(sm_ref, L), a2_ref[...]) + ab_ref[...]) * (1.0 / G_GATE_NORM)
    lg = jnp.where(valid, lg, 0.0)

    ti = lax.broadcasted_iota(jnp.int32, (L, L), 0)
    si = lax.broadcasted_iota(jnp.int32, (L, L), 1)
    b_all = _dot_hi((si <= ti).astype(F32), lg)
    b_scr[...] = b_all
    q_scr[...] = q_all
    eb_all = jnp.exp(b_all)
    b_last_all = b_all[L - 1:L, :]
    e_tail_all = jnp.exp(b_last_all - b_all)
    ones_lv = jnp.ones((L, G_DV), F32)

    n_t = L if t_valid >= L else t_valid
    s_col = {rows: lax.broadcasted_iota(jnp.int32, (rows, 1), 0) for rows in range(8, L + 1, 8)}
    t_lane = {rows: lax.broadcasted_iota(jnp.int32, (rows, L), 1) for rows in range(8, L + 1, 8)}

    for h in range(G_HEADS):
        sl = slice(h * G_DK, (h + 1) * G_DK)
        sv = slice(h * G_DV, (h + 1) * G_DV)
        q, k, v, b = q_all[:, sl], k_all[:, sl], v_all[:, sv], b_all[:, sl]

        at = jnp.zeros((L, L), F32)
        for t in range(n_t):
            rows = 8 * (t // 8 + 1)
            bt = b_scr[t:t + 1, sl]
            qt = q_scr[t:t + 1, sl]
            e = jnp.exp(jnp.where(s_col[rows] <= t, bt - b[:rows], NEG))
            col = jnp.sum(qt * k[:rows] * e, axis=-1, keepdims=True)
            top = jnp.where(t_lane[rows] == t, col, at[:rows])
            at = top if rows == L else jnp.concatenate([top, at[rows:]], axis=0)

        S = s_ref[0, h]
        o = _dot(q * eb_all[:, sl], S) + _dot_tn(at, v)
        decay = jnp.exp(lax.dot_general(lg[:, sl], ones_lv, (((0,), (0,)), ((), ())),
                                        precision=lax.Precision.HIGHEST,
                                        preferred_element_type=F32))
        s_ref[0, h] = S * decay + _dot_tn(k * e_tail_all[:, sl], v)

        on = o * lax.rsqrt(jnp.mean(o * o, axis=-1, keepdims=True) + EPS) * ng_ref[:, sv]
        out_ref[0, :, sv] = (on * gate_o[:, sv]).astype(BF16)


def _gla_mixer(proj3, s0, l, lw, *, L):
    bsz, t, _ = proj3.shape
    tb = min(t, L)
    nc = -(-t // L)
    gw = G_HEADS * G_DK
    st = pl.BlockSpec((1, G_HEADS, G_DK, G_DV), lambda bi, c: (bi, 0, 0, 0))
    st_in = pl.BlockSpec((1, 1, G_HEADS, G_DK, G_DV), lambda bi, c: (l, bi, 0, 0, 0))
    return pl.pallas_call(
        functools.partial(_gla_kernel, L=L, t_valid=t),
        out_shape=[jax.ShapeDtypeStruct((bsz, nc * L, BRANCH_W), BF16),
                   jax.ShapeDtypeStruct((bsz, G_HEADS, G_DK, G_DV), F32)],
        grid=(bsz, nc),
        in_specs=[pl.BlockSpec((1, tb, gw), lambda bi, c: (bi, c, C_GQ // gw)),
                  pl.BlockSpec((1, tb, gw), lambda bi, c: (bi, c, C_GQ // gw + 1)),
                  pl.BlockSpec((1, tb, BRANCH_W), lambda bi, c: (bi, c, C_GV // BRANCH_W)),
                  pl.BlockSpec((1, tb, BRANCH_W), lambda bi, c: (bi, c, C_GOG // BRANCH_W)),
                  pl.BlockSpec((1, tb, SMALL_W), lambda bi, c: (bi, c, C_SMALL // SMALL_W)),
                  pl.BlockSpec((SMALL_W, gw), lambda bi, c: (0, 0)),
                  pl.BlockSpec((1, gw), lambda bi, c: (0, 0)),
                  pl.BlockSpec((1, BRANCH_W), lambda bi, c: (0, 0)),
                  st_in],
        out_specs=[pl.BlockSpec((1, L, BRANCH_W), lambda bi, c: (bi, c, 0)), st],
        scratch_shapes=[pltpu.VMEM((L, gw), F32), pltpu.VMEM((L, gw), F32)],
        compiler_params=_cparams(2),
        name="gla_scan",
    )(proj3, proj3, proj3, proj3, proj3, lw["g_a2p"], lw["g_a_b"], lw["g_norm_g"], s0)


def _pack_w_in(w):
    r0, m0, g0, t0 = 0, R_COLS, R_COLS + 4104, R_COLS + 4104 + 3088
    pieces = [
        w[..., r0:r0 + 3072],
        w[..., m0:m0 + 2048], w[..., m0 + 2048:m0 + 3072], w[..., m0 + 3080:m0 + 4104],
        w[..., g0:g0 + 512], w[..., g0 + 512:g0 + 1024], w[..., g0 + 1024:g0 + 2048],
        w[..., g0 + 2064:g0 + 3088],
        w[..., t0:t0 + 3 * D_MODEL],
        w[..., r0 + 3072:r0 + 3264], w[..., m0 + 3072:m0 + 3080], w[..., g0 + 2048:g0 + 2064],
    ]
    used = sum(p.shape[-1] for p in pieces)
    pieces.append(jnp.zeros(w.shape[:-1] + (N_PACKED - used,), w.dtype))
    return jnp.concatenate(pieces, axis=-1)


def _rows_padded(w, row0, total):
    return jnp.pad(w, ((row0, total - row0 - w.shape[0]), (0, 0)))


def _layer_weights(l, P):
    mu = P["rwkv_mu"][l]
    mu_p = jnp.concatenate([mu, jnp.zeros((3 * BRANCH_W + SMALL_W - R_COLS,), F32)]).reshape(1, -1)
    row = lambda a: a.reshape(1, -1)
    return {
        "norm1_g": P["norm1_g"][l], "gate_b": P["gate_b"][l],
        "mu_p": mu_p, "w0": row(P["rwkv_w0"][l]), "a0": row(P["rwkv_a0"][l]),
        "k_k": row(P["rwkv_k_k"][l]), "k_a": row(P["rwkv_k_a"][l]), "r_k": row(P["rwkv_r_k"][l]),
        "w2p": _rows_padded(P["rwkv_w2"][l], 0, SMALL_W),
        "a2p": _rows_padded(P["rwkv_a2"][l], R_LORA, SMALL_W),
        "g2p": _rows_padded(P["rwkv_g2"][l], 2 * R_LORA, SMALL_W),
        "ln_g": row(P["rwkv_ln_g"][l]), "ln_b": row(P["rwkv_ln_b"][l]),
        "conv_w": P["mlstm_conv_w"][l], "conv_b": row(P["mlstm_conv_b"][l]),
        "i_b": row(P["mlstm_i_b"][l]), "f_b": row(P["mlstm_f_b"][l]),
        "m_norm_g": row(P["mlstm_norm_g"][l]),
        "g_a2p": _rows_padded(P["gla_a2"][l], S_GXA, SMALL_W), "g_a_b": row(P["gla_a_b"][l]),
        "g_norm_g": row(P["gla_norm_g"][l]),
        "norm2_g": P["norm2_g"][l],
    }


def _layer(x2, bsz, t, states, l, lw, big, ones_bd, cfg):
    rw_prev, rw_s, m_conv, m_c, m_n, m_m, g_s = states
    m = bsz * t
    L, tm = cfg["L"], cfg["tm"]
    proj = _rms_matmul(x2, lw["norm1_g"], big["w_in"], l, tm=tm, tn=512)
    proj3 = proj.reshape(bsz, t, N_PACKED)

    o_r, rw_s_new, rw_prev_new = _rwkv_mixer(proj3, rw_prev[l], rw_s, l, lw, ones_bd, L=L,
                                             tm=cfg["tm_prep"])

    o_m, m_c_new, m_n_new, m_m_new = _mlstm_mixer(proj3, m_conv, m_c, m_n, m_m, l, lw, L=L)
    qk_tail = proj3[:, -min(t, CONV_W - 1):, C_MQK:C_MQK + 2 * BRANCH_W]
    m_conv_new = jnp.concatenate([m_conv[l], qk_tail], axis=1)[:, -(CONV_W - 1):]
    o_g, g_s_new = _gla_mixer(proj3, g_s, l, lw, L=L)
    o_m = o_m[:, :t].reshape(m, BRANCH_W)
    o_g = o_g[:, :t].reshape(m, BRANCH_W)

    merged = _merge(o_r, o_m, o_g, big["w_branch"], l, proj, lw["gate_b"],
                    tm=cfg["tm_merge"], tn=512)
    x2 = _matmul_residual(merged, big["w_out"], l, x2, tm=tm, tn=512, tk=D_MODEL)
    hidden = _rms_swiglu(x2, lw["norm2_g"], big["w_gu"], l, tm=tm, tn=512)
    x2 = _matmul_residual(hidden, big["w_down"], l, x2, tm=tm, tn=512, tk=1408)
    return x2, (rw_prev_new, rw_s_new, m_conv_new, m_c_new, m_n_new, m_m_new, g_s_new)


def _trunk(x, states, layer_ws, big, final_g, ones_bd, cfg):
    bsz, t, d = x.shape
    x2 = x.reshape(bsz * t, d)
    per_layer = []
    for l in range(DEPTH):
        x2, new = _layer(x2, bsz, t, states, l, layer_ws[l], big, ones_bd, cfg)
        per_layer.append(new)
    new_states = [jnp.stack([st[i] for st in per_layer], axis=0) for i in range(len(states))]
    y = _rmsnorm(x2, final_g, tm=cfg["tm_norm"]).reshape(bsz, t, d)
    return y, new_states


PROMPT_CFG = dict(L=64, tm=1024, tm_prep=256, tm_merge=512, tm_norm=512)
SAMPLE_CFG = dict(L=16,tm=128, tm_prep=128, tm_merge=128, tm_norm=128)


def _zero_states(bsz):
    return (jnp.zeros((DEPTH, bsz, R_COLS), F32),
            jnp.zeros((DEPTH, bsz, R_HEADS, R_HEAD, R_HEAD), F32),
            jnp.zeros((DEPTH, bsz, CONV_W - 1, 2 * BRANCH_W), F32),
            jnp.zeros((DEPTH, bsz, M_HEADS, M_DK, M_DK), F32),
            jnp.zeros((DEPTH, bsz, M_HEADS, M_DK), F32),
            jnp.zeros((DEPTH, bsz, M_HEADS), F32),
            jnp.zeros((DEPTH, bsz, G_HEADS, G_DK, G_DV), F32))


def kernel(x_prompt, x_sample, state_rwkv_shift, state_rwkv_wkv, state_mlstm_conv, state_mlstm_C, state_mlstm_n, state_mlstm_m, state_gla_S, norm1_g, w_in, gate_b, rwkv_mu, rwkv_w0, rwkv_w2, rwkv_a0, rwkv_a2, rwkv_g2, rwkv_k_k, rwkv_k_a, rwkv_r_k, rwkv_ln_g, rwkv_ln_b, mlstm_conv_w, mlstm_conv_b, mlstm_i_b, mlstm_f_b, mlstm_norm_g, gla_a2, gla_a_b, gla_norm_g, w_branch, w_out, norm2_g, ffn_w_gu, ffn_w_down, final_norm_g):
    P = dict(norm1_g=norm1_g, w_in=w_in, gate_b=gate_b, rwkv_mu=rwkv_mu, rwkv_w0=rwkv_w0,
             rwkv_w2=rwkv_w2, rwkv_a0=rwkv_a0, rwkv_a2=rwkv_a2, rwkv_g2=rwkv_g2,
             rwkv_k_k=rwkv_k_k, rwkv_k_a=rwkv_k_a, rwkv_r_k=rwkv_r_k, rwkv_ln_g=rwkv_ln_g,
             rwkv_ln_b=rwkv_ln_b, mlstm_conv_w=mlstm_conv_w, mlstm_conv_b=mlstm_conv_b,
             mlstm_i_b=mlstm_i_b, mlstm_f_b=mlstm_f_b, mlstm_norm_g=mlstm_norm_g,
             gla_a2=gla_a2, gla_a_b=gla_a_b, gla_norm_g=gla_norm_g, w_branch=w_branch,
             w_out=w_out, norm2_g=norm2_g, ffn_w_gu=ffn_w_gu, ffn_w_down=ffn_w_down)
    layer_ws = [_layer_weights(l, P) for l in range(DEPTH)]
    big = dict(w_in=_pack_w_in(w_in), w_branch=w_branch, w_out=w_out, w_gu=ffn_w_gu,
               w_down=ffn_w_down)
    head_of_lane = jnp.arange(128) // R_HEAD
    ones_bd = (head_of_lane[:, None] == head_of_lane[None, :]).astype(BF16)

    y_p, p_states = _trunk(x_prompt, _zero_states(x_prompt.shape[0]), layer_ws, big,
                           final_norm_g, ones_bd, PROMPT_CFG)
    s_states = (state_rwkv_shift, state_rwkv_wkv, state_mlstm_conv, state_mlstm_C,
                state_mlstm_n, state_mlstm_m, state_gla_S)
    y_s, s_states = _trunk(x_sample, s_states, layer_ws, big, final_norm_g, ones_bd, SAMPLE_CFG)
    return (y_p, y_s, *p_states, *s_states)
```

```python
import functools

import jax
import jax.numpy as jnp
from jax import lax
from jax.experimental import pallas as pl
from jax.experimental.pallas import tpu as pltpu

F32 = jnp.float32
BF16 = jnp.bfloat16

D_MODEL = 2048
DEPTH = 2
BRANCH_W = 1024
R_HEADS, R_HEAD = 16, 64
R_PAIRS = R_HEADS // 2
R_LORA = 64
R_COLS = 3 * BRANCH_W + 3 * R_LORA
R_GN_EPS = 64e-5
M_HEADS, M_DK = 4, 256
CONV_W = 4
G_HEADS, G_DK, G_DV = 4, 128, 256
G_LR = 16
G_GATE_NORM = 16.0
D_FF = 5632
EPS = 1e-6
NEG = -1e30

C_RWKV = 0
C_MQK = 3072
C_MV = 5120
C_MO = 6144
C_GQ = 7168
C_GV = 8192
C_GOG = 9216
C_GATE = 10240
C_SMALL = 16384
SMALL_W = 256
S_I, S_F, S_GXA = 192, 196, 200
N_PACKED = 16896

VMEM_LIMIT = 56 * 1024 * 1024


def _cparams(n_axes):
    return pltpu.CompilerParams(dimension_semantics=("arbitrary",) * n_axes,
                                vmem_limit_bytes=VMEM_LIMIT)


def _dot(a, b):
    return jnp.dot(a.astype(BF16), b.astype(BF16), preferred_element_type=F32)


def _dot_nt(a, b):
    return lax.dot_general(a.astype(BF16), b.astype(BF16), (((1,), (1,)), ((), ())),
                           preferred_element_type=F32)


def _dot_tn(a, b):
    return lax.dot_general(a.astype(BF16), b.astype(BF16), (((0,), (0,)), ((), ())),
                           preferred_element_type=F32)


def _dot_hi(a, b):
    return jnp.dot(a, b, precision=lax.Precision.HIGHEST, preferred_element_type=F32)


def _segsum(y, ones_blockdiag):
    hi = y.astype(BF16)
    lo = (y - hi.astype(F32)).astype(BF16)
    return (jnp.dot(hi, ones_blockdiag, preferred_element_type=F32)
            + jnp.dot(lo, ones_blockdiag, preferred_element_type=F32))


def _sigmoid(x):
    return 1.0 / (1.0 + jnp.exp(-x))


def _silu(x):
    return x * _sigmoid(x)


def _log_sigmoid(x):
    return -_softplus(-x)


def _softplus(x):
    return jnp.maximum(x, 0.0) + jnp.log(1.0 + jnp.exp(-jnp.abs(x)))


def _rms_mm_kernel(x_ref, g_ref, w_ref, o_ref, osm_ref, h_scr, *, n_main):
    @pl.when(pl.program_id(1) == 0)
    def _():
        x = x_ref[...]
        ms = jnp.mean(x * x, axis=-1, keepdims=True)
        h_scr[...] = (x * lax.rsqrt(ms + EPS) * g_ref[...]).astype(BF16)

    acc = jnp.dot(h_scr[...], w_ref[0].astype(BF16), preferred_element_type=F32)
    j = pl.program_id(1)

    @pl.when(j < n_main)
    def _():
        o_ref[...] = acc.astype(o_ref.dtype)

    @pl.when(j >= n_main)
    def _():
        osm_ref[...] = acc


def _rms_matmul(x, g, w, l, *, tm, tn, main_dtype):
    m, k = x.shape
    n = w.shape[2]
    n_main = C_SMALL // tn
    assert n == C_SMALL + tn
    return pl.pallas_call(
        functools.partial(_rms_mm_kernel, n_main=n_main),
        out_shape=[jax.ShapeDtypeStruct((m, C_SMALL), main_dtype),
                   jax.ShapeDtypeStruct((m, tn), F32)],
        grid=(m // tm, n // tn),
        in_specs=[pl.BlockSpec((tm, k), lambda i, j: (i, 0)),
                  pl.BlockSpec((1, k), lambda i, j: (0, 0)),
                  pl.BlockSpec((1, k, tn), lambda i, j: (l, 0, j))],
        out_specs=[pl.BlockSpec((tm, tn), lambda i, j: (i, jnp.minimum(j, n_main - 1))),
                   pl.BlockSpec((tm, tn), lambda i, j: (i, 0))],
        scratch_shapes=[pltpu.VMEM((tm, k), BF16)],
        compiler_params=_cparams(2),
        name="rms_in_proj",
    )(x, g.reshape(1, k), w)


def _rms_swiglu_kernel(x_ref, g_ref, wg_ref, wu_ref, o_ref, h_scr):
    @pl.when(pl.program_id(1) == 0)
    def _():
        x = x_ref[...]
        ms = jnp.mean(x * x, axis=-1, keepdims=True)
        h_scr[...] = (x * lax.rsqrt(ms + EPS) * g_ref[...]).astype(BF16)

    h = h_scr[...]
    gg = jnp.dot(h, wg_ref[0].astype(BF16), preferred_element_type=F32)
    uu = jnp.dot(h, wu_ref[0].astype(BF16), preferred_element_type=F32)
    o_ref[...] = (_silu(gg) * uu).astype(BF16)


def _rms_swiglu(x, g, w_gu, l, *, tm, tn):
    m, k = x.shape
    nj = D_FF // tn
    return pl.pallas_call(
        _rms_swiglu_kernel,
        out_shape=jax.ShapeDtypeStruct((m, D_FF), BF16),
        grid=(m // tm, nj),
        in_specs=[pl.BlockSpec((tm, k), lambda i, j: (i, 0)),
                  pl.BlockSpec((1, k), lambda i, j: (0, 0)),
                  pl.BlockSpec((1, k, tn), lambda i, j: (l, 0, j)),
                  pl.BlockSpec((1, k, tn), lambda i, j: (l, 0, j + nj))],
        out_specs=pl.BlockSpec((tm, tn), lambda i, j: (i, j)),
        scratch_shapes=[pltpu.VMEM((tm, k), BF16)],
        compiler_params=_cparams(2),
        name="rms_ffn_swiglu",
    )(x, g.reshape(1, k), w_gu, w_gu)


def _mm_res_kernel(a_ref, w_ref, res_ref, o_ref, acc_ref, *, nk):
    kk = pl.program_id(2)

    @pl.when(kk == 0)
    def _():
        acc_ref[...] = jnp.zeros_like(acc_ref)

    acc_ref[...] += jnp.dot(a_ref[...], w_ref[0].astype(BF16), preferred_element_type=F32)

    @pl.when(kk == nk - 1)
    def _():
        o_ref[...] = acc_ref[...] + res_ref[...]


def _matmul_residual(a, w, l, res, *, tm, tn, tk):
    m, k = a.shape
    n = w.shape[2]
    nk = k // tk
    return pl.pallas_call(
        functools.partial(_mm_res_kernel, nk=nk),
        out_shape=jax.ShapeDtypeStruct((m, n), F32),
        grid=(m // tm, n // tn, nk),
        in_specs=[pl.BlockSpec((tm, tk), lambda i, j, kk: (i, kk)),
                  pl.BlockSpec((1, tk, tn), lambda i, j, kk: (l, kk, j)),
                  pl.BlockSpec((tm, tn), lambda i, j, kk: (i, j))],
        out_specs=pl.BlockSpec((tm, tn), lambda i, j, kk: (i, j)),
        scratch_shapes=[pltpu.VMEM((tm, tn), F32)],
        compiler_params=_cparams(3),
        name="matmul_residual",
    )(a, w, res)


def _merge_kernel(or_ref, om_ref, og_ref, wr_ref, wm_ref, wg_ref,
                  gr_ref, gm_ref, gg_ref, br_ref, bm_ref, bg_ref, o_ref):
    acc = _sigmoid(gr_ref[...].astype(F32) + br_ref[0]) * jnp.dot(
        or_ref[...], wr_ref[0, 0].astype(BF16), preferred_element_type=F32)
    acc += _sigmoid(gm_ref[...].astype(F32) + bm_ref[0]) * jnp.dot(
        om_ref[...], wm_ref[0, 0].astype(BF16), preferred_element_type=F32)
    acc += _sigmoid(gg_ref[...].astype(F32) + bg_ref[0]) * jnp.dot(
        og_ref[...], wg_ref[0, 0].astype(BF16), preferred_element_type=F32)
    o_ref[...] = acc.astype(BF16)


def _merge(o_r, o_m, o_g, w_branch, l, proj, gate_b, *, tm, tn):
    m = o_r.shape[0]
    gate_blk = C_GATE // tn
    per = D_MODEL // tn
    o_spec = pl.BlockSpec((tm, BRANCH_W), lambda i, j: (i, 0))

    def w_spec(b):
        return pl.BlockSpec((1, 1, BRANCH_W, tn), lambda i, j: (l, b, 0, j))

    def g_spec(b):
        return pl.BlockSpec((tm, tn), lambda i, j: (i, gate_blk + b * per + j))

    def b_spec(b):
        return pl.BlockSpec((1, 1, tn), lambda i, j: (b, 0, j))

    gate_b = gate_b.reshape(3, 1, D_MODEL)

    return pl.pallas_call(
        _merge_kernel,
        out_shape=jax.ShapeDtypeStruct((m, D_MODEL), BF16),
        grid=(m // tm, per),
        in_specs=[o_spec, o_spec, o_spec, w_spec(0), w_spec(1), w_spec(2),
                  g_spec(0), g_spec(1), g_spec(2), b_spec(0), b_spec(1), b_spec(2)],
        out_specs=pl.BlockSpec((tm, tn), lambda i, j: (i, j)),
        compiler_params=_cparams(2),
        name="gated_merge",
    )(o_r, o_m, o_g, w_branch, w_branch, w_branch, proj, proj, proj, gate_b, gate_b, gate_b)


def _rmsnorm_kernel(x_ref, g_ref, o_ref):
    x = x_ref[...]
    ms = jnp.mean(x * x, axis=-1, keepdims=True)
    o_ref[...] = x * lax.rsqrt(ms + EPS) * g_ref[...]


def _rmsnorm(x, g, *, tm):
    m, k = x.shape
    return pl.pallas_call(
        _rmsnorm_kernel,
        out_shape=jax.ShapeDtypeStruct((m, k), F32),
        grid=(m // tm,),
        in_specs=[pl.BlockSpec((tm, k), lambda i: (i, 0)),
                  pl.BlockSpec((1, k), lambda i: (0, 0))],
        out_specs=pl.BlockSpec((tm, k), lambda i: (i, 0)),
        compiler_params=_cparams(1),
        name="final_rmsnorm",
    )(x, g.reshape(1, k))


def _rwkv_prep_kernel(pr_ref, pk_ref, pv_ref, ps_ref, qr_ref, qk_ref, qv_ref, qs_ref,
                      mur_ref, muk_ref, muv_ref, mus_ref,
                      w0_ref, a0_ref, kk_ref, ka_ref, rk_ref,
                      w2_ref, a2_ref, g2_ref, j_ref,
                      r_out, k_out, v_out, kkn_out, b_out, ld_out, g_out, bon_out,
                      *scratch, tm, explicit_prev):
    def shift(p_ref, q_ref, mu_ref, scr):
        p = p_ref[0].astype(F32)
        if explicit_prev:
            prev = q_ref[0]
        else:
            @pl.when(pl.program_id(1) == 0)
            def _():
                scr[7:8, :] = q_ref[0]

            scr[8:8 + tm, :] = p
            prev = scr[7:7 + tm, :]
            scr[7:8, :] = p[tm - 1:tm, :]
        return p + (prev - p) * mu_ref[...]

    scr = scratch if scratch else (None,) * 4
    xr = shift(pr_ref, qr_ref, mur_ref, scr[0])
    xk = shift(pk_ref, qk_ref, muk_ref, scr[1])
    xv = shift(pv_ref, qv_ref, muv_ref, scr[2])
    xs = shift(ps_ref, qs_ref, mus_ref, scr[3])

    w = -_softplus(-(w0_ref[...] + _dot(jnp.tanh(xs), w2_ref[...]))) - 0.5
    ld_out[0] = -jnp.exp(w)
    a = _sigmoid(a0_ref[...] + _dot(xs, a2_ref[...]))
    g_out[0] = _dot(_sigmoid(xs), g2_ref[...])

    ones_bd = j_ref[...]
    kkr = xk * kk_ref[...]
    k2 = xk * (1.0 + (a - 1.0) * ka_ref[...])
    rkk = xr * k2 * rk_ref[...]
    for p in range(R_PAIRS):
        sl = slice(p * 128, (p + 1) * 128)
        kb = kkr[:, sl]
        nrm = jnp.sqrt(_segsum(kb * kb, ones_bd))
        kn = kb / jnp.maximum(nrm, 1e-12)
        kkn_out[0, :, sl] = kn
        b_out[0, :, sl] = kn * a[:, sl]
        bon_out[0, :, sl] = _segsum(rkk[:, sl], ones_bd) * xv[:, sl]
    r_out[0] = xr
    k_out[0] = k2
    v_out[0] = xv


def _rwkv_prep(proj3, small3, prev3, lw, ones_bd, *, tm):
    bsz, t, _ = proj3.shape
    explicit_prev = prev3.shape[1] == t
    tq = tm if explicit_prev else 1
    qmap = (lambda blk: (lambda bi, i: (bi, i, blk))) if explicit_prev else (
        lambda blk: (lambda bi, i: (bi, 0, blk)))
    small_blk = 3 * BRANCH_W // SMALL_W
    big = lambda blk: pl.BlockSpec((1, tm, BRANCH_W), lambda bi, i: (bi, i, blk))
    vec = lambda blk: pl.BlockSpec((1, BRANCH_W), lambda bi, i: (0, blk))
    full = lambda shape: pl.BlockSpec(shape, lambda bi, i: (0, 0))
    out = jax.ShapeDtypeStruct((bsz, t, BRANCH_W), F32)
    scratch = [] if explicit_prev else (
        [pltpu.VMEM((tm + 8, BRANCH_W), F32)] * 3 + [pltpu.VMEM((tm + 8, SMALL_W), F32)])
    return pl.pallas_call(
        functools.partial(_rwkv_prep_kernel, tm=tm, explicit_prev=explicit_prev),
        out_shape=[out] * 8,
        grid=(bsz, t // tm),
        in_specs=[big(0), big(1), big(2),
                  pl.BlockSpec((1, tm, SMALL_W), lambda bi, i: (bi, i, 0)),
                  pl.BlockSpec((1, tq, BRANCH_W), qmap(0)),
                  pl.BlockSpec((1, tq, BRANCH_W), qmap(1)),
                  pl.BlockSpec((1, tq, BRANCH_W), qmap(2)),
                  pl.BlockSpec((1, tq, SMALL_W), qmap(small_blk)),
                  vec(0), vec(1), vec(2),
                  pl.BlockSpec((1, SMALL_W), lambda bi, i: (0, small_blk)),
                  vec(0), vec(0), vec(0), vec(0), vec(0),
                  full((SMALL_W, BRANCH_W)), full((SMALL_W, BRANCH_W)), full((SMALL_W, BRANCH_W)),
                  full((128, 128))],
        out_specs=[pl.BlockSpec((1, tm, BRANCH_W), lambda bi, i: (bi, i, 0))] * 8,
        scratch_shapes=scratch,
        compiler_params=_cparams(2),
        name="rwkv_prep",
    )(proj3, proj3, proj3, small3, prev3, prev3, prev3, prev3,
      lw["mu_p"], lw["mu_p"], lw["mu_p"], lw["mu_p"],
      lw["w0"], lw["a0"], lw["k_k"], lw["k_a"], lw["r_k"],
      lw["w2p"], lw["a2p"], lw["g2p"], ones_bd)


def _rwkv_scan_kernel(r_ref, k_ref, v_ref, kk_ref, b_ref, ld_ref, g_ref, bon_ref,
                      lng_ref, lnb_ref, j_ref, s0_ref, *rest, L, group, n_earlier):
    earlier = rest[:n_earlier]
    o_ref, st_ref, s_ref = rest[n_earlier:]
    c_id = pl.program_id(1)

    @pl.when(c_id == 0)
    def _():
        z = jnp.zeros((R_HEAD, R_HEAD), F32)
        for p in range(R_PAIRS):
            top = jnp.concatenate([s0_ref[0, 0, 2 * p], z], axis=1)
            bot = jnp.concatenate([z, s0_ref[0, 0, 2 * p + 1]], axis=1)
            s_ref[p] = jnp.concatenate([top, bot], axis=0)

    ld_all = ld_ref[0]
    ti = lax.broadcasted_iota(jnp.int32, (L, L), 0)
    si = lax.broadcasted_iota(jnp.int32, (L, L), 1)
    cs_all = _dot_hi((si <= ti).astype(F32), ld_all)
    ec_all = jnp.exp(cs_all)
    enc_all = jnp.exp(-cs_all)
    ecm_all = jnp.exp(cs_all - ld_all)
    c_last_all = cs_all[L - 1:L, :]
    e_tail_all = jnp.exp(c_last_all - cs_all)
    g_last_all = jnp.exp(c_last_all)

    lane = lax.broadcasted_iota(jnp.int32, (L, 128), 1)
    head_a = lane < R_HEAD

    def stack(x):
        return jnp.concatenate([jnp.where(head_a, x, 0.0), jnp.where(head_a, 0.0, x)], axis=0)

    P2 = 2 * L
    ri = lax.broadcasted_iota(jnp.int32, (P2, P2), 0)
    ci = lax.broadcasted_iota(jnp.int32, (P2, P2), 1)
    strict = ri > ci
    incl = ri >= ci
    eye = jnp.where(ri == ci, 1.0, 0.0)
    ones_bd = j_ref[...]
    inv_n = 1.0 / R_HEAD

    cat0 = lambda a, b: jnp.concatenate([a, b], axis=0)
    cat1 = lambda a, b: jnp.concatenate([a, b], axis=1)

    for g0 in range(0, R_PAIRS, group):
        pairs = list(range(g0, g0 + group))
        sls = [slice(p * 128, (p + 1) * 128) for p in pairs]
        each = lambda f: [f(i) for i in range(group)]

        S = each(lambda i: s_ref[pairs[i]])
        Rs = each(lambda i: stack(r_ref[0, :, sls[i]] * ec_all[:, sls[i]]))
        Bs = each(lambda i: stack(kk_ref[0, :, sls[i]] * ecm_all[:, sls[i]]))
        Ks = each(lambda i: stack(k_ref[0, :, sls[i]] * enc_all[:, sls[i]]))
        As = each(lambda i: stack(-(b_ref[0, :, sls[i]] * enc_all[:, sls[i]])))
        Vs = each(lambda i: stack(v_ref[0, :, sls[i]]))
        Kt = each(lambda i: stack(k_ref[0, :, sls[i]] * e_tail_all[:, sls[i]]))
        At = each(lambda i: stack(-(b_ref[0, :, sls[i]] * e_tail_all[:, sls[i]])))

        if P2 % 128 == 0:
            sc = each(lambda i: _dot_nt(cat0(Bs[i], Rs[i]), cat0(As[i], Ks[i])))
            s_ba = each(lambda i: sc[i][:P2, :P2])
            s_bk = each(lambda i: sc[i][:P2, P2:])
            s_ra = each(lambda i: sc[i][P2:, :P2])
            s_rk = each(lambda i: sc[i][P2:, P2:])
        else:
            s_ba = each(lambda i: _dot_nt(Bs[i], As[i]))
            s_bk = each(lambda i: _dot_nt(Bs[i], Ks[i]))
            s_ra = each(lambda i: _dot_nt(Rs[i], As[i]))
            s_rk = each(lambda i: _dot_nt(Rs[i], Ks[i]))
        Nm = each(lambda i: jnp.where(strict, s_ba[i], 0.0))
        Mbk = each(lambda i: jnp.where(strict, s_bk[i], 0.0))
        Mra = each(lambda i: jnp.where(incl, s_ra[i], 0.0))
        Mrk = each(lambda i: jnp.where(incl, s_rk[i], 0.0))

        Tm = each(lambda i: eye + Nm[i])
        Pw = Nm
        span = 2
        while span < L:
            Pw = [_dot(x, x) for x in Pw]
            Tm = each(lambda i: Tm[i] + _dot(Tm[i], Pw[i]))
            span *= 2

        mv = each(lambda i: _dot(cat0(Mbk[i], Mrk[i]), Vs[i]))
        tb = each(lambda i: _dot(Tm[i], cat1(Bs[i], mv[i][:P2])))
        mu = each(lambda i: _dot(Mra[i], tb[i]))
        Ro = each(lambda i: Rs[i] + mu[i][:, :128])
        uo = each(lambda i: _dot_nt(cat0(tb[i][:, :128], Ro[i]), S[i]))
        U = each(lambda i: uo[i][:P2] + tb[i][:, 128:])
        O = each(lambda i: uo[i][P2:] + mv[i][P2:] + mu[i][:, 128:])
        for i, p in enumerate(pairs):
            s_ref[p] = S[i] * g_last_all[:, sls[i]] + _dot_tn(
                cat0(U[i], Vs[i]), cat0(At[i], Kt[i]))

        for i in range(group):
            sl = sls[i]
            out = O[i][:L] + O[i][L:]
            mean = _segsum(out, ones_bd) * inv_n
            d = out - mean
            var = _segsum(d * d, ones_bd) * inv_n
            y = d * lax.rsqrt(var + R_GN_EPS) * lng_ref[:, sl] + lnb_ref[:, sl]
            o_ref[0, :, sl] = ((y + bon_ref[0, :, sl]) * g_ref[0, :, sl]).astype(BF16)

    @pl.when(c_id == pl.num_programs(1) - 1)
    def _():
        for p in range(R_PAIRS):
            sp = s_ref[p]
            st_ref[n_earlier, 0, 2 * p] = sp[:R_HEAD, :R_HEAD]
            st_ref[n_earlier, 0, 2 * p + 1] = sp[R_HEAD:, R_HEAD:]
        for i, e_ref in enumerate(earlier):
            st_ref[i] = e_ref[0]


def _rwkv_scan(seqs, g, bonus, ln_g, ln_b, ones_bd, s0, l, earlier_s, *, L):
    bsz, t, _ = seqs[0].shape
    nl = len(earlier_s) + 1
    seq_spec = pl.BlockSpec((1, L, BRANCH_W), lambda bi, c: (bi, c, 0))
    vec_spec = pl.BlockSpec((1, BRANCH_W), lambda bi, c: (0, 0))
    st_ea = pl.BlockSpec((1, 1, R_HEADS, R_HEAD, R_HEAD), lambda bi, c: (0, bi, 0, 0, 0))
    return pl.pallas_call(
        functools.partial(_rwkv_scan_kernel, L=L, group=R_PAIRS, n_earlier=nl - 1),
        out_shape=[jax.ShapeDtypeStruct((bsz, t, BRANCH_W), BF16),
                   jax.ShapeDtypeStruct((nl, bsz, R_HEADS, R_HEAD, R_HEAD), F32)],
        grid=(bsz, t // L),
        in_specs=[seq_spec] * 8 + [
            vec_spec, vec_spec, pl.BlockSpec((128, 128), lambda bi, c: (0, 0)),
            pl.BlockSpec((1, 1, R_HEADS, R_HEAD, R_HEAD), lambda bi, c: (l, bi, 0, 0, 0))]
        + [st_ea] * (nl - 1),
        out_specs=[seq_spec,
                   pl.BlockSpec((nl, 1, R_HEADS, R_HEAD, R_HEAD), lambda bi, c: (0, bi, 0, 0, 0))],
        scratch_shapes=[pltpu.VMEM((R_PAIRS, 128, 128), F32)],
        compiler_params=_cparams(2),
        name="rwkv_scan",
    )(*seqs, g, bonus, ln_g, ln_b, ones_bd, s0, *earlier_s)


def _rwkv_mixer(proj3, small3, prev, s0, l, lw, ones_bd, earlier_s, *, L, tm):
    bsz, t, _ = proj3.shape
    new_shift = jnp.concatenate(
        [proj3[:, -1, :3 * BRANCH_W].astype(F32), small3[:, -1, :R_COLS - 3 * BRANCH_W]], axis=-1)
    prev3 = jnp.pad(prev, ((0, 0), (0, 3 * BRANCH_W + SMALL_W - R_COLS)))[:, None, :]
    if t == 1:
        outs = _rwkv_prep(proj3.reshape(1, bsz, -1), small3.reshape(1, bsz, -1),
                          prev3.reshape(1, bsz, -1), lw, ones_bd, tm=tm)
        outs = [o.reshape(bsz, 1, BRANCH_W) for o in outs]
    else:
        outs = _rwkv_prep(proj3, small3, prev3, lw, ones_bd, tm=tm)
    tp = -(-t // L) * L
    if tp != t:
        outs = [jnp.pad(o, ((0, 0), (0, tp - t), (0, 0))) for o in outs]
    r, k2, v, kkn, b, ld, g, bonus = outs
    o_r, s_new = _rwkv_scan((r, k2, v, kkn, b, ld), g, bonus, lw["ln_g"], lw["ln_b"],
                            ones_bd, s0, l, earlier_s, L=L)
    return o_r[:, :t].reshape(bsz * t, BRANCH_W), s_new, new_shift


def _mlstm_kernel(q_ref, k_ref, v_ref, o_ref, sm_ref, cq_ref, ck_ref, wq_ref, wk_ref,
                  bq_ref, bk_ref, ib_ref, fb_ref, ng_ref, c0_ref, n0_ref, m0_ref,
                  *rest, L, t_valid, n_earlier):
    earlier = rest[:n_earlier]
    out_ref, c_ref, n_ref, m_ref, qs_ref, ks_ref = rest[n_earlier:]
    last = n_earlier
    c_id = pl.program_id(1)

    @pl.when(c_id == 0)
    def _():
        for i, e_ref in enumerate(earlier):
            c_ref[i] = e_ref[0]
        c_ref[last] = c0_ref[0]
        n_ref[...] = n0_ref[0]
        m_ref[...] = m0_ref[0]
        qs_ref[5:8, :] = cq_ref[0, 0]
        ks_ref[5:8, :] = ck_ref[0, 0]

    qs_ref[8:8 + L, :] = _rows(q_ref, L)
    ks_ref[8:8 + L, :] = _rows(k_ref, L)
    conv_q = bq_ref[...]
    conv_k = bk_ref[...]
    for j in range(CONV_W):
        conv_q = conv_q + qs_ref[5 + j:5 + j + L, :] * wq_ref[j:j + 1, :]
        conv_k = conv_k + ks_ref[5 + j:5 + j + L, :] * wk_ref[j:j + 1, :]
    if L >= CONV_W - 1:
        tail_q = qs_ref[5 + L:8 + L, :]
        tail_k = ks_ref[5 + L:8 + L, :]
        qs_ref[5:8, :] = tail_q
        ks_ref[5:8, :] = tail_k

    row = lax.broadcasted_iota(jnp.int32, (L, 1), 0)
    valid = (c_id * L + row) < t_valid
    q_all = jnp.where(valid, _silu(conv_q), 0.0)
    k_all = jnp.where(valid, _silu(conv_k) * (M_DK ** -0.5), 0.0)
    v_all = jnp.where(valid, _rows(v_ref, L), 0.0)
    gate_o = _sigmoid(_rows(o_ref, L))

    sm = _rows(sm_ref, L)
    lane = lax.broadcasted_iota(jnp.int32, (L, SMALL_W), 1)
    head_lane = lane < M_HEADS
    i_pre = jnp.where(head_lane, pltpu.roll(sm, SMALL_W - S_I, axis=1), 0.0)
    f_pre = jnp.where(head_lane, pltpu.roll(sm, SMALL_W - S_F, axis=1), 0.0)
    ig4 = jnp.where(valid & head_lane, i_pre + ib_ref[...], NEG)
    lf4 = jnp.where(valid & head_lane, _log_sigmoid(f_pre + fb_ref[...]), 0.0)

    ti = lax.broadcasted_iota(jnp.int32, (L, L), 0)
    si = lax.broadcasted_iota(jnp.int32, (L, L), 1)
    causal = si <= ti
    diag = ti == si
    ones_l = jnp.ones((L, L), F32)
    F4 = _dot_hi(causal.astype(F32), lf4)
    gmf4 = ig4 - F4

    heads = range(M_HEADS)
    each = lambda f: [f(h) for h in heads]
    sls = [slice(h * M_DK, (h + 1) * M_DK) for h in heads]
    rowsum = lambda x: jnp.sum(x, axis=-1, keepdims=True)
    q = each(lambda h: q_all[:, sls[h]])
    k = each(lambda h: k_all[:, sls[h]])
    v = each(lambda h: v_all[:, sls[h]])
    F = each(lambda h: rowsum(jnp.where(lane == h, F4, 0.0)))
    ig = each(lambda h: rowsum(jnp.where(lane == h, ig4, 0.0)))
    gmf = each(lambda h: rowsum(jnp.where(lane == h, gmf4, 0.0)))
    g_row = each(lambda h: _dot_hi(ones_l, jnp.where(diag, jnp.broadcast_to(gmf[h], (L, L)), 0.0)))
    Dm = each(lambda h: jnp.where(causal, F[h] + g_row[h], NEG))

    C = each(lambda h: c_ref[last, 0, h])
    n = each(lambda h: n_ref[0, h])
    m_prev = each(lambda h: m_ref[0, h])
    inter = each(lambda h: F[h] + m_prev[h])
    m_t = each(lambda h: jnp.maximum(inter[h], jnp.max(Dm[h], axis=-1, keepdims=True)))
    w_inter = each(lambda h: jnp.exp(inter[h] - m_t[h]))
    Sm = each(lambda h: _dot_nt(q[h], k[h]) * jnp.exp(Dm[h] - m_t[h]))
    num = each(lambda h: w_inter[h] * _dot(q[h], C[h]) + _dot(Sm[h], v[h]))
    den = each(lambda h: w_inter[h] * rowsum(q[h] * n[h]) + rowsum(Sm[h]))
    hh = each(lambda h: num[h] / jnp.maximum(jnp.abs(den[h]), jnp.exp(-m_t[h])))

    FL = each(lambda h: F[h][L - 1:L, :])
    g_s = each(lambda h: FL[h] - F[h] + ig[h])
    m_new = each(lambda h: jnp.maximum(FL[h] + m_prev[h], jnp.max(g_s[h], axis=0, keepdims=True)))
    a_c = each(lambda h: jnp.exp(FL[h] + m_prev[h] - m_new[h]))
    kw = each(lambda h: k[h] * jnp.exp(g_s[h] - m_new[h]))
    for h in heads:
        c_ref[last, 0, h] = a_c[h] * C[h] + _dot_tn(kw[h], v[h])
        n_ref[0, h] = a_c[h] * n[h] + jnp.sum(kw[h], axis=0, keepdims=True)
        m_ref[0, h] = m_new[h]
        hn = hh[h] * lax.rsqrt(jnp.mean(hh[h] * hh[h], axis=-1, keepdims=True) + EPS)
        out_ref[0, :, sls[h]] = (gate_o[:, sls[h]] * hn * ng_ref[:, sls[h]]).astype(BF16)


def _rows(ref, L):
    x = ref[0].astype(F32)
    if x.shape[0] == L:
        return x
    assert x.shape[0] == 1
    row = lax.broadcasted_iota(jnp.int32, (L, x.shape[1]), 0)
    return jnp.where(row == 0, x, 0.0)


def _mlstm_mixer(proj3, small3, conv_buf, c0, n0, m0, l, lw, earlier_c, *, L):
    bsz, t, _ = proj3.shape
    tb = min(t, L)
    nc = -(-t // L)
    nl = len(earlier_c) + 1
    seq = lambda col: pl.BlockSpec((1, tb, BRANCH_W), lambda bi, c: (bi, c, col // BRANCH_W))
    st_c = pl.BlockSpec((nl, 1, M_HEADS, M_DK, M_DK), lambda bi, c: (0, bi, 0, 0, 0))
    ea_c = pl.BlockSpec((1, 1, M_HEADS, M_DK, M_DK), lambda bi, c: (0, bi, 0, 0, 0))
    st_n = pl.BlockSpec((1, M_HEADS, 1, M_DK), lambda bi, c: (bi, 0, 0, 0))
    st_m = pl.BlockSpec((1, M_HEADS, 1, 1), lambda bi, c: (bi, 0, 0, 0))
    in_c = pl.BlockSpec((1, 1, M_HEADS, M_DK, M_DK), lambda bi, c: (l, bi, 0, 0, 0))
    in_n = pl.BlockSpec((1, 1, M_HEADS, 1, M_DK), lambda bi, c: (l, bi, 0, 0, 0))
    in_m = pl.BlockSpec((1, 1, M_HEADS, 1, 1), lambda bi, c: (l, bi, 0, 0, 0))
    conv = lambda blk: pl.BlockSpec((1, 1, CONV_W - 1, BRANCH_W), lambda bi, c: (l, bi, 0, blk))
    cw = lambda blk: pl.BlockSpec((CONV_W, BRANCH_W), lambda bi, c: (0, blk))
    vec = lambda blk: pl.BlockSpec((1, BRANCH_W), lambda bi, c: (0, blk))
    hb = pl.BlockSpec((1, SMALL_W), lambda bi, c: (0, 0))
    pad_heads = lambda a: jnp.pad(a, ((0, 0), (0, SMALL_W - M_HEADS)))
    out, c_new, n_new, m_new = pl.pallas_call(
        functools.partial(_mlstm_kernel, L=L, t_valid=t, n_earlier=nl - 1),
        out_shape=[jax.ShapeDtypeStruct((bsz, nc * L, BRANCH_W), BF16),
                   jax.ShapeDtypeStruct((nl, bsz, M_HEADS, M_DK, M_DK), F32),
                   jax.ShapeDtypeStruct((bsz, M_HEADS, 1, M_DK), F32),
                   jax.ShapeDtypeStruct((bsz, M_HEADS, 1, 1), F32)],
        grid=(bsz, nc),
        in_specs=[seq(C_MQK), seq(C_MQK + BRANCH_W), seq(C_MV), seq(C_MO),
                  pl.BlockSpec((1, tb, SMALL_W), lambda bi, c: (bi, c, 0)),
                  conv(0), conv(1), cw(0), cw(1), vec(0), vec(1), hb, hb, vec(0),
                  in_c, in_n, in_m] + [ea_c] * (nl - 1),
        out_specs=[pl.BlockSpec((1, L, BRANCH_W), lambda bi, c: (bi, c, 0)), st_c, st_n, st_m],
        scratch_shapes=[pltpu.VMEM((L + 8, BRANCH_W), F32), pltpu.VMEM((L + 8, BRANCH_W), F32)],
        compiler_params=_cparams(2),
        name="mlstm_scan",
    )(proj3, proj3, proj3, proj3, small3, conv_buf, conv_buf,
      lw["conv_w"], lw["conv_w"], lw["conv_b"], lw["conv_b"],
      pad_heads(lw["i_b"]), pad_heads(lw["f_b"]),
      lw["m_norm_g"], c0, n0.reshape(DEPTH, bsz, M_HEADS, 1, M_DK),
      m0.reshape(DEPTH, bsz, M_HEADS, 1, 1), *earlier_c)
    return out, c_new, n_new.reshape(bsz, M_HEADS, M_DK), m_new.reshape(bsz, M_HEADS)


def _gla_kernel(q_ref, k_ref, v_ref, og_ref, sm_ref, a2_ref, ab_ref, ng_ref, s0_ref,
                *rest, L, t_valid, n_earlier):
    earlier = rest[:n_earlier]
    out_ref, s_ref, b_scr, q_scr = rest[n_earlier:]
    last = n_earlier
    c_id = pl.program_id(1)

    @pl.when(c_id == 0)
    def _():
        for i, e_ref in enumerate(earlier):
            s_ref[i] = e_ref[0]
        s_ref[last] = s0_ref[0]

    row = lax.broadcasted_iota(jnp.int32, (L, 1), 0)
    valid = (c_id * L + row) < t_valid
    q_all = jnp.where(valid, _rows(q_ref, L) * (G_DK ** -0.5), 0.0)
    k_all = jnp.where(valid, _rows(k_ref, L), 0.0)
    v_all = jnp.where(valid, _rows(v_ref, L), 0.0)
    gate_o = _silu(_rows(og_ref, L))
    lg = _log_sigmoid(_dot(_rows(sm_ref, L), a2_ref[...]) + ab_ref[...]) * (1.0 / G_GATE_NORM)
    lg = jnp.where(valid, lg, 0.0)

    ti = lax.broadcasted_iota(jnp.int32, (L, L), 0)
    si = lax.broadcasted_iota(jnp.int32, (L, L), 1)
    b_all = _dot_hi((si <= ti).astype(F32), lg)
    b_scr[...] = b_all
    q_scr[...] = q_all
    eb_all = jnp.exp(b_all)
    b_last_all = b_all[L - 1:L, :]
    e_tail_all = jnp.exp(b_last_all - b_all)
    ones_lv = jnp.ones((L, G_DV), F32)

    n_t = L if t_valid >= L else t_valid
    s_col = {rows: lax.broadcasted_iota(jnp.int32, (rows, 1), 0) for rows in range(8, L + 1, 8)}
    t_lane = {rows: lax.broadcasted_iota(jnp.int32, (rows, L), 1) for rows in range(8, L + 1, 8)}

    for h in range(G_HEADS):
        sl = slice(h * G_DK, (h + 1) * G_DK)
        sv = slice(h * G_DV, (h + 1) * G_DV)
        q, k, v, b = q_all[:, sl], k_all[:, sl], v_all[:, sv], b_all[:, sl]

        at = jnp.zeros((L, L), F32)
        for t in range(n_t):
            rows = 8 * (t // 8 + 1)
            bt = b_scr[t:t + 1, sl]
            qt = q_scr[t:t + 1, sl]
            e = jnp.exp(jnp.where(s_col[rows] <= t, bt - b[:rows], NEG))
            col = jnp.sum(qt * k[:rows] * e, axis=-1, keepdims=True)
            top = jnp.where(t_lane[rows] == t, col, at[:rows])
            at = top if rows == L else jnp.concatenate([top, at[rows:]], axis=0)

        S = s_ref[last, 0, h]
        o = _dot(q * eb_all[:, sl], S) + _dot_tn(at, v)
        decay = jnp.exp(lax.dot_general(lg[:, sl], ones_lv, (((0,), (0,)), ((), ())),
                                        precision=lax.Precision.HIGHEST,
                                        preferred_element_type=F32))
        s_ref[last, 0, h] = S * decay + _dot_tn(k * e_tail_all[:, sl], v)

        on = o * lax.rsqrt(jnp.mean(o * o, axis=-1, keepdims=True) + EPS) * ng_ref[:, sv]
        out_ref[0, :, sv] = (on * gate_o[:, sv]).astype(BF16)


def _gla_mixer(proj3, small3, s0, l, lw, earlier_s, *, L):
    bsz, t, _ = proj3.shape
    tb = min(t, L)
    nc = -(-t // L)
    nl = len(earlier_s) + 1
    gw = G_HEADS * G_DK
    st = pl.BlockSpec((nl, 1, G_HEADS, G_DK, G_DV), lambda bi, c: (0, bi, 0, 0, 0))
    st_ea = pl.BlockSpec((1, 1, G_HEADS, G_DK, G_DV), lambda bi, c: (0, bi, 0, 0, 0))
    st_in = pl.BlockSpec((1, 1, G_HEADS, G_DK, G_DV), lambda bi, c: (l, bi, 0, 0, 0))
    return pl.pallas_call(
        functools.partial(_gla_kernel, L=L, t_valid=t, n_earlier=nl - 1),
        out_shape=[jax.ShapeDtypeStruct((bsz, nc * L, BRANCH_W), BF16),
                   jax.ShapeDtypeStruct((nl, bsz, G_HEADS, G_DK, G_DV), F32)],
        grid=(bsz, nc),
        in_specs=[pl.BlockSpec((1, tb, gw), lambda bi, c: (bi, c, C_GQ // gw)),
                  pl.BlockSpec((1, tb, gw), lambda bi, c: (bi, c, C_GQ // gw + 1)),
                  pl.BlockSpec((1, tb, BRANCH_W), lambda bi, c: (bi, c, C_GV // BRANCH_W)),
                  pl.BlockSpec((1, tb, BRANCH_W), lambda bi, c: (bi, c, C_GOG // BRANCH_W)),
                  pl.BlockSpec((1, tb, SMALL_W), lambda bi, c: (bi, c, 0)),
                  pl.BlockSpec((SMALL_W, gw), lambda bi, c: (0, 0)),
                  pl.BlockSpec((1, gw), lambda bi, c: (0, 0)),
                  pl.BlockSpec((1, BRANCH_W), lambda bi, c: (0, 0)),
                  st_in] + [st_ea] * (nl - 1),
        out_specs=[pl.BlockSpec((1, L, BRANCH_W), lambda bi, c: (bi, c, 0)), st],
        scratch_shapes=[pltpu.VMEM((L, gw), F32), pltpu.VMEM((L, gw), F32)],
        compiler_params=_cparams(2),
        name="gla_scan",
    )(proj3, proj3, proj3, proj3, small3, lw["g_a2p"], lw["g_a_b"], lw["g_norm_g"], s0,
      *earlier_s)


def _pack_w_in(w):
    r0, m0, g0, t0 = 0, R_COLS, R_COLS + 4104, R_COLS + 4104 + 3088
    pieces = [
        w[..., r0:r0 + 3072],
        w[..., m0:m0 + 2048], w[..., m0 + 2048:m0 + 3072], w[..., m0 + 3080:m0 + 4104],
        w[..., g0:g0 + 512], w[..., g0 + 512:g0 + 1024], w[..., g0 + 1024:g0 + 2048],
        w[..., g0 + 2064:g0 + 3088],
        w[..., t0:t0 + 3 * D_MODEL],
        w[..., r0 + 3072:r0 + 3264], w[..., m0 + 3072:m0 + 3080], w[..., g0 + 2048:g0 + 2064],
    ]
    used = sum(p.shape[-1] for p in pieces)
    pieces.append(jnp.zeros(w.shape[:-1] + (N_PACKED - used,), w.dtype))
    return jnp.concatenate(pieces, axis=-1)


def _rows_padded(w, row0, total):
    return jnp.pad(w, ((row0, total - row0 - w.shape[0]), (0, 0)))


def _layer_weights(l, P):
    mu = P["rwkv_mu"][l]
    mu_p = jnp.concatenate([mu, jnp.zeros((3 * BRANCH_W + SMALL_W - R_COLS,), F32)]).reshape(1, -1)
    row = lambda a: a.reshape(1, -1)
    return {
        "norm1_g": P["norm1_g"][l], "gate_b": P["gate_b"][l],
        "mu_p": mu_p, "w0": row(P["rwkv_w0"][l]), "a0": row(P["rwkv_a0"][l]),
        "k_k": row(P["rwkv_k_k"][l]), "k_a": row(P["rwkv_k_a"][l]), "r_k": row(P["rwkv_r_k"][l]),
        "w2p": _rows_padded(P["rwkv_w2"][l], 0, SMALL_W),
        "a2p": _rows_padded(P["rwkv_a2"][l], R_LORA, SMALL_W),
        "g2p": _rows_padded(P["rwkv_g2"][l], 2 * R_LORA, SMALL_W),
        "ln_g": row(P["rwkv_ln_g"][l]), "ln_b": row(P["rwkv_ln_b"][l]),
        "conv_w": P["mlstm_conv_w"][l], "conv_b": row(P["mlstm_conv_b"][l]),
        "i_b": row(P["mlstm_i_b"][l]), "f_b": row(P["mlstm_f_b"][l]),
        "m_norm_g": row(P["mlstm_norm_g"][l]),
        "g_a2p": _rows_padded(P["gla_a2"][l], S_GXA, SMALL_W), "g_a_b": row(P["gla_a_b"][l]),
        "g_norm_g": row(P["gla_norm_g"][l]),
        "norm2_g": P["norm2_g"][l],
    }


def _layer(x2, bsz, t, states, l, lw, big, ones_bd, cfg, earlier):
    ea_wkv, ea_c, ea_s = earlier
    rw_prev, rw_s, m_conv, m_c, m_n, m_m, g_s = states
    m = bsz * t
    L, tm = cfg["L"], cfg["tm"]
    proj, small = _rms_matmul(x2, lw["norm1_g"], big["w_in"], l, tm=tm, tn=512,
                              main_dtype=cfg["proj_dtype"])
    proj3 = proj.reshape(bsz, t, C_SMALL)
    small3 = small.reshape(bsz, t, -1)

    o_r, rw_s_new, rw_prev_new = _rwkv_mixer(proj3, small3, rw_prev[l], rw_s, l, lw, ones_bd,
                                             ea_wkv, L=L, tm=cfg["tm_prep"])

    o_m, m_c_new, m_n_new, m_m_new = _mlstm_mixer(proj3, small3, m_conv, m_c, m_n, m_m, l, lw,
                                                  ea_c, L=L)
    qk_tail = proj3[:, -min(t, CONV_W - 1):, C_MQK:C_MQK + 2 * BRANCH_W].astype(F32)
    m_conv_new = jnp.concatenate([m_conv[l], qk_tail], axis=1)[:, -(CONV_W - 1):]
    o_g, g_s_new = _gla_mixer(proj3, small3, g_s, l, lw, ea_s, L=L)
    o_m = o_m[:, :t].reshape(m, BRANCH_W)
    o_g = o_g[:, :t].reshape(m, BRANCH_W)

    merged = _merge(o_r, o_m, o_g, big["w_branch"], l, proj, lw["gate_b"],
                    tm=cfg["tm_merge"], tn=512)
    x2 = _matmul_residual(merged, big["w_out"], l, x2, tm=tm, tn=512, tk=D_MODEL)
    hidden = _rms_swiglu(x2, lw["norm2_g"], big["w_gu"], l, tm=tm, tn=512)
    x2 = _matmul_residual(hidden, big["w_down"], l, x2, tm=tm, tn=512, tk=1408)
    return x2, (rw_prev_new, rw_s_new, m_conv_new, m_c_new, m_n_new, m_m_new, g_s_new)


def _trunk(x, states, layer_ws, big, final_g, ones_bd, cfg):
    bsz, t, d = x.shape
    x2 = x.reshape(bsz * t, d)
    per_layer = []
    for l in range(DEPTH):
        is_last = l == DEPTH - 1
        earlier = tuple([st[i] for st in per_layer] if is_last else [] for i in BIG_STATES)
        x2, new = _layer(x2, bsz, t, states, l, layer_ws[l], big, ones_bd, cfg, earlier)
        per_layer.append(new)
    new_states = [per_layer[-1][i] if i in BIG_STATES
                  else jnp.stack([st[i] for st in per_layer], axis=0) for i in range(len(states))]
    y = _rmsnorm(x2, final_g, tm=cfg["tm_norm"]).reshape(bsz, t, d)
    return y, new_states


BIG_STATES = (1, 3, 6)

PROMPT_CFG = dict(L=64, tm=1024, tm_prep=256, tm_merge=512, tm_norm=512, proj_dtype=BF16)
SAMPLE_CFG = dict(L=16, tm=128, tm_prep=128, tm_merge=128, tm_norm=128, proj_dtype=F32)


def _zero_states(bsz):
    return (jnp.zeros((DEPTH, bsz, R_COLS), F32),
            jnp.zeros((DEPTH, bsz, R_HEADS, R_HEAD, R_HEAD), F32),
            jnp.zeros((DEPTH, bsz, CONV_W - 1, 2 * BRANCH_W), F32),
            jnp.zeros((DEPTH, bsz, M_HEADS, M_DK, M_DK), F32),
            jnp.zeros((DEPTH, bsz, M_HEADS, M_DK), F32),
            jnp.zeros((DEPTH, bsz, M_HEADS), F32),
            jnp.zeros((DEPTH, bsz, G_HEADS, G_DK, G_DV), F32))


def kernel(x_prompt, x_sample, state_rwkv_shift, state_rwkv_wkv, state_mlstm_conv, state_mlstm_C, state_mlstm_n, state_mlstm_m, state_gla_S, norm1_g, w_in, gate_b, rwkv_mu, rwkv_w0, rwkv_w2, rwkv_a0, rwkv_a2, rwkv_g2, rwkv_k_k, rwkv_k_a, rwkv_r_k, rwkv_ln_g, rwkv_ln_b, mlstm_conv_w, mlstm_conv_b, mlstm_i_b, mlstm_f_b, mlstm_norm_g, gla_a2, gla_a_b, gla_norm_g, w_branch, w_out, norm2_g, ffn_w_gu, ffn_w_down, final_norm_g):
    P = dict(norm1_g=norm1_g, w_in=w_in, gate_b=gate_b, rwkv_mu=rwkv_mu, rwkv_w0=rwkv_w0,
             rwkv_w2=rwkv_w2, rwkv_a0=rwkv_a0, rwkv_a2=rwkv_a2, rwkv_g2=rwkv_g2,
             rwkv_k_k=rwkv_k_k, rwkv_k_a=rwkv_k_a, rwkv_r_k=rwkv_r_k, rwkv_ln_g=rwkv_ln_g,
             rwkv_ln_b=rwkv_ln_b, mlstm_conv_w=mlstm_conv_w, mlstm_conv_b=mlstm_conv_b,
             mlstm_i_b=mlstm_i_b, mlstm_f_b=mlstm_f_b, mlstm_norm_g=mlstm_norm_g,
             gla_a2=gla_a2, gla_a_b=gla_a_b, gla_norm_g=gla_norm_g, w_branch=w_branch,
             w_out=w_out, norm2_g=norm2_g, ffn_w_gu=ffn_w_gu, ffn_w_down=ffn_w_down)
    layer_ws = [_layer_weights(l, P) for l in range(DEPTH)]
    big = dict(w_in=_pack_w_in(w_in), w_branch=w_branch, w_out=w_out, w_gu=ffn_w_gu,
               w_down=ffn_w_down)
    head_of_lane = jnp.arange(128) // R_HEAD
    ones_bd = (head_of_lane[:, None] == head_of_lane[None, :]).astype(BF16)

    y_p, p_states = _trunk(x_prompt, _zero_states(x_prompt.shape[0]), layer_ws, big,
                           final_norm_g, ones_bd, PROMPT_CFG)
    s_states = (state_rwkv_shift, state_rwkv_wkv, state_mlstm_conv, state_mlstm_C,
                state_mlstm_n, state_mlstm_m, state_gla_S)
    y_s, s_states = _trunk(x_sample, s_states, layer_ws, big, final_norm_g, ones_bd, SAMPLE_CFG)
    return (y_p, y_s, *p_states, *s_states)
```

```python
import functools

import jax
import jax.numpy as jnp
from jax import lax
from jax.experimental import pallas as pl
from jax.experimental.pallas import tpu as pltpu

F32 = jnp.float32
BF16 = jnp.bfloat16

D_MODEL = 2048
DEPTH = 2
BRANCH_W = 1024
R_HEADS, R_HEAD = 16, 64
R_PAIRS = R_HEADS // 2
R_LORA = 64
R_COLS = 3 * BRANCH_W + 3 * R_LORA
R_GN_EPS = 64e-5
M_HEADS, M_DK = 4, 256
CONV_W = 4
G_HEADS, G_DK, G_DV = 4, 128, 256
G_LR = 16
G_GATE_NORM = 16.0
D_FF = 5632
EPS = 1e-6
NEG = -1e30

C_RWKV = 0
C_MQK = 3072
C_MV = 5120
C_MO = 6144
C_GQ = 7168
C_GV = 8192
C_GOG = 9216
C_GATE = 10240
C_SMALL = 16384
SMALL_W = 256
S_I, S_F, S_GXA = 192, 196, 200
N_PACKED = 16896

VMEM_LIMIT = 56 * 1024 * 1024


def _cparams(n_axes):
    return pltpu.CompilerParams(dimension_semantics=("arbitrary",) * n_axes,
                                vmem_limit_bytes=VMEM_LIMIT)


def _dot(a, b):
    return jnp.dot(a.astype(BF16), b.astype(BF16), preferred_element_type=F32)


def _dot_nt(a, b):
    return lax.dot_general(a.astype(BF16), b.astype(BF16), (((1,), (1,)), ((), ())),
                           preferred_element_type=F32)


def _dot_tn(a, b):
    return lax.dot_general(a.astype(BF16), b.astype(BF16), (((0,), (0,)), ((), ())),
                           preferred_element_type=F32)


def _dot_hi(a, b):
    return jnp.dot(a, b, precision=lax.Precision.HIGHEST, preferred_element_type=F32)


def _cumsum_rows(x, single_step):
    n = x.shape[0]
    if single_step:
        row = lax.broadcasted_iota(jnp.int32, x.shape, 0)
        return jnp.where(row == 0, x, x[0:1, :])
    ti = lax.broadcasted_iota(jnp.int32, (n, n), 0)
    si = lax.broadcasted_iota(jnp.int32, (n, n), 1)
    return _dot_hi((si <= ti).astype(F32), x)


def _segsum(y, ones_blockdiag):
    hi = y.astype(BF16)
    lo = (y - hi.astype(F32)).astype(BF16)
    return (jnp.dot(hi, ones_blockdiag, preferred_element_type=F32)
            + jnp.dot(lo, ones_blockdiag, preferred_element_type=F32))


def _sigmoid(x):
    return 1.0 / (1.0 + jnp.exp(-x))


def _silu(x):
    return x * _sigmoid(x)


def _log_sigmoid(x):
    return -_softplus(-x)


def _softplus(x):
    return jnp.maximum(x, 0.0) + jnp.log(1.0 + jnp.exp(-jnp.abs(x)))


def _rms_mm_kernel(x_ref, g_ref, w_ref, o_ref, osm_ref, h_scr, *, n_main):
    @pl.when(pl.program_id(1) == 0)
    def _():
        x = x_ref[...]
        ms = jnp.mean(x * x, axis=-1, keepdims=True)
        h_scr[...] = (x * lax.rsqrt(ms + EPS) * g_ref[...]).astype(BF16)

    acc = jnp.dot(h_scr[...], w_ref[0].astype(BF16), preferred_element_type=F32)
    j = pl.program_id(1)

    @pl.when(j < n_main)
    def _():
        o_ref[...] = acc.astype(o_ref.dtype)

    @pl.when(j >= n_main)
    def _():
        osm_ref[...] = acc


def _rms_matmul(x, g, w, l, *, tm, tn, main_dtype):
    m, k = x.shape
    n = w.shape[2]
    n_main = C_SMALL // tn
    assert n == C_SMALL + tn
    return pl.pallas_call(
        functools.partial(_rms_mm_kernel, n_main=n_main),
        out_shape=[jax.ShapeDtypeStruct((m, C_SMALL), main_dtype),
                   jax.ShapeDtypeStruct((m, tn), F32)],
        grid=(m // tm, n // tn),
        in_specs=[pl.BlockSpec((tm, k), lambda i, j: (i, 0)),
                  pl.BlockSpec((1, k), lambda i, j: (0, 0)),
                  pl.BlockSpec((1, k, tn), lambda i, j: (l, 0, j))],
        out_specs=[pl.BlockSpec((tm, tn), lambda i, j: (i, jnp.minimum(j, n_main - 1))),
                   pl.BlockSpec((tm, tn), lambda i, j: (i, 0))],
        scratch_shapes=[pltpu.VMEM((tm, k), BF16)],
        compiler_params=_cparams(2),
        name="rms_in_proj",
    )(x, g.reshape(1, k), w)


def _rms_swiglu_kernel(x_ref, g_ref, wg_ref, wu_ref, o_ref, h_scr):
    @pl.when(pl.program_id(1) == 0)
    def _():
        x = x_ref[...]
        ms = jnp.mean(x * x, axis=-1, keepdims=True)
        h_scr[...] = (x * lax.rsqrt(ms + EPS) * g_ref[...]).astype(BF16)

    h = h_scr[...]
    gg = jnp.dot(h, wg_ref[0].astype(BF16), preferred_element_type=F32)
    uu = jnp.dot(h, wu_ref[0].astype(BF16), preferred_element_type=F32)
    o_ref[...] = (_silu(gg) * uu).astype(BF16)


def _rms_swiglu(x, g, w_gu, l, *, tm, tn):
    m, k = x.shape
    nj = D_FF // tn
    return pl.pallas_call(
        _rms_swiglu_kernel,
        out_shape=jax.ShapeDtypeStruct((m, D_FF), BF16),
        grid=(m // tm, nj),
        in_specs=[pl.BlockSpec((tm, k), lambda i, j: (i, 0)),
                  pl.BlockSpec((1, k), lambda i, j: (0, 0)),
                  pl.BlockSpec((1, k, tn), lambda i, j: (l, 0, j)),
                  pl.BlockSpec((1, k, tn), lambda i, j: (l, 0, j + nj))],
        out_specs=pl.BlockSpec((tm, tn), lambda i, j: (i, j)),
        scratch_shapes=[pltpu.VMEM((tm, k), BF16)],
        compiler_params=_cparams(2),
        name="rms_ffn_swiglu",
    )(x, g.reshape(1, k), w_gu, w_gu)


def _mm_res_kernel(a_ref, w_ref, res_ref, o_ref, acc_ref, *, nk):
    kk = pl.program_id(2)

    @pl.when(kk == 0)
    def _():
        acc_ref[...] = jnp.zeros_like(acc_ref)

    acc_ref[...] += jnp.dot(a_ref[...], w_ref[0].astype(BF16), preferred_element_type=F32)

    @pl.when(kk == nk - 1)
    def _():
        o_ref[...] = acc_ref[...] + res_ref[...]


def _matmul_residual(a, w, l, res, *, tm, tn, tk):
    m, k = a.shape
    n = w.shape[2]
    nk = k // tk
    return pl.pallas_call(
        functools.partial(_mm_res_kernel, nk=nk),
        out_shape=jax.ShapeDtypeStruct((m, n), F32),
        grid=(m // tm, n // tn, nk),
        in_specs=[pl.BlockSpec((tm, tk), lambda i, j, kk: (i, kk)),
                  pl.BlockSpec((1, tk, tn), lambda i, j, kk: (l, kk, j)),
                  pl.BlockSpec((tm, tn), lambda i, j, kk: (i, j))],
        out_specs=pl.BlockSpec((tm, tn), lambda i, j, kk: (i, j)),
        scratch_shapes=[pltpu.VMEM((tm, tn), F32)],
        compiler_params=_cparams(3),
        name="matmul_residual",
    )(a, w, res)


def _merge_kernel(or_ref, om_ref, og_ref, wr_ref, wm_ref, wg_ref,
                  gr_ref, gm_ref, gg_ref, br_ref, bm_ref, bg_ref, o_ref):
    acc = _sigmoid(gr_ref[...].astype(F32) + br_ref[0]) * jnp.dot(
        or_ref[...], wr_ref[0, 0].astype(BF16), preferred_element_type=F32)
    acc += _sigmoid(gm_ref[...].astype(F32) + bm_ref[0]) * jnp.dot(
        om_ref[...], wm_ref[0, 0].astype(BF16), preferred_element_type=F32)
    acc += _sigmoid(gg_ref[...].astype(F32) + bg_ref[0]) * jnp.dot(
        og_ref[...], wg_ref[0, 0].astype(BF16), preferred_element_type=F32)
    o_ref[...] = acc.astype(BF16)


def _merge(o_r, o_m, o_g, w_branch, l, proj, gate_b, *, tm, tn):
    m = o_r.shape[0]
    gate_blk = C_GATE // tn
    per = D_MODEL // tn
    o_spec = pl.BlockSpec((tm, BRANCH_W), lambda i, j: (i, 0))

    def w_spec(b):
        return pl.BlockSpec((1, 1, BRANCH_W, tn), lambda i, j: (l, b, 0, j))

    def g_spec(b):
        return pl.BlockSpec((tm, tn), lambda i, j: (i, gate_blk + b * per + j))

    def b_spec(b):
        return pl.BlockSpec((1, 1, tn), lambda i, j: (b, 0, j))

    gate_b = gate_b.reshape(3, 1, D_MODEL)

    return pl.pallas_call(
        _merge_kernel,
        out_shape=jax.ShapeDtypeStruct((m, D_MODEL), BF16),
        grid=(m // tm, per),
        in_specs=[o_spec, o_spec, o_spec, w_spec(0), w_spec(1), w_spec(2),
                  g_spec(0), g_spec(1), g_spec(2), b_spec(0), b_spec(1), b_spec(2)],
        out_specs=pl.BlockSpec((tm, tn), lambda i, j: (i, j)),
        compiler_params=_cparams(2),
        name="gated_merge",
    )(o_r, o_m, o_g, w_branch, w_branch, w_branch, proj, proj, proj, gate_b, gate_b, gate_b)


def _rmsnorm_kernel(x_ref, g_ref, o_ref):
    x = x_ref[...]
    ms = jnp.mean(x * x, axis=-1, keepdims=True)
    o_ref[...] = x * lax.rsqrt(ms + EPS) * g_ref[...]


def _rmsnorm(x, g, *, tm):
    m, k = x.shape
    return pl.pallas_call(
        _rmsnorm_kernel,
        out_shape=jax.ShapeDtypeStruct((m, k), F32),
        grid=(m // tm,),
        in_specs=[pl.BlockSpec((tm, k), lambda i: (i, 0)),
                  pl.BlockSpec((1, k), lambda i: (0, 0))],
        out_specs=pl.BlockSpec((tm, k), lambda i: (i, 0)),
        compiler_params=_cparams(1),
        name="final_rmsnorm",
    )(x, g.reshape(1, k))


def _rwkv_prep_kernel(pr_ref, pk_ref, pv_ref, ps_ref, qr_ref, qk_ref, qv_ref, qs_ref,
                      mur_ref, muk_ref, muv_ref, mus_ref,
                      w0_ref, a0_ref, kk_ref, ka_ref, rk_ref,
                      w2_ref, a2_ref, g2_ref, j_ref,
                      r_out, k_out, v_out, kkn_out, b_out, ld_out, g_out, bon_out,
                      *scratch, tm, explicit_prev):
    def shift(p_ref, q_ref, mu_ref, scr):
        p = p_ref[0].astype(F32)
        if explicit_prev:
            prev = q_ref[0]
        else:
            @pl.when(pl.program_id(1) == 0)
            def _():
                scr[7:8, :] = q_ref[0]

            scr[8:8 + tm, :] = p
            prev = scr[7:7 + tm, :]
            scr[7:8, :] = p[tm - 1:tm, :]
        return p + (prev - p) * mu_ref[...]

    scr = scratch if scratch else (None,) * 4
    xr = shift(pr_ref, qr_ref, mur_ref, scr[0])
    xk = shift(pk_ref, qk_ref, muk_ref, scr[1])
    xv = shift(pv_ref, qv_ref, muv_ref, scr[2])
    xs = shift(ps_ref, qs_ref, mus_ref, scr[3])

    w = -_softplus(-(w0_ref[...] + _dot(jnp.tanh(xs), w2_ref[...]))) - 0.5
    ld_out[0] = -jnp.exp(w)
    a = _sigmoid(a0_ref[...] + _dot(xs, a2_ref[...]))
    g_out[0] = _dot(_sigmoid(xs), g2_ref[...])

    ones_bd = j_ref[...]
    kkr = xk * kk_ref[...]
    k2 = xk * (1.0 + (a - 1.0) * ka_ref[...])
    rkk = xr * k2 * rk_ref[...]
    for p in range(R_PAIRS):
        sl = slice(p * 128, (p + 1) * 128)
        kb = kkr[:, sl]
        nrm = jnp.sqrt(_segsum(kb * kb, ones_bd))
        kn = kb / jnp.maximum(nrm, 1e-12)
        kkn_out[0, :, sl] = kn
        b_out[0, :, sl] = kn * a[:, sl]
        bon_out[0, :, sl] = _segsum(rkk[:, sl], ones_bd) * xv[:, sl]
    r_out[0] = xr
    k_out[0] = k2
    v_out[0] = xv


def _rwkv_prep(proj3, small3, prev3, lw, ones_bd, *, tm):
    bsz, t, _ = proj3.shape
    explicit_prev = prev3.shape[1] == t
    tq = tm if explicit_prev else 1
    qmap = (lambda blk: (lambda bi, i: (bi, i, blk))) if explicit_prev else (
        lambda blk: (lambda bi, i: (bi, 0, blk)))
    small_blk = 3 * BRANCH_W // SMALL_W
    big = lambda blk: pl.BlockSpec((1, tm, BRANCH_W), lambda bi, i: (bi, i, blk))
    vec = lambda blk: pl.BlockSpec((1, BRANCH_W), lambda bi, i: (0, blk))
    full = lambda shape: pl.BlockSpec(shape, lambda bi, i: (0, 0))
    out = jax.ShapeDtypeStruct((bsz, t, BRANCH_W), F32)
    scratch = [] if explicit_prev else (
        [pltpu.VMEM((tm + 8, BRANCH_W), F32)] * 3 + [pltpu.VMEM((tm + 8, SMALL_W), F32)])
    return pl.pallas_call(
        functools.partial(_rwkv_prep_kernel, tm=tm, explicit_prev=explicit_prev),
        out_shape=[out] * 8,
        grid=(bsz, t // tm),
        in_specs=[big(0), big(1), big(2),
                  pl.BlockSpec((1, tm, SMALL_W), lambda bi, i: (bi, i, 0)),
                  pl.BlockSpec((1, tq, BRANCH_W), qmap(0)),
                  pl.BlockSpec((1, tq, BRANCH_W), qmap(1)),
                  pl.BlockSpec((1, tq, BRANCH_W), qmap(2)),
                  pl.BlockSpec((1, tq, SMALL_W), qmap(small_blk)),
                  vec(0), vec(1), vec(2),
                  pl.BlockSpec((1, SMALL_W), lambda bi, i: (0, small_blk)),
                  vec(0), vec(0), vec(0), vec(0), vec(0),
                  full((SMALL_W, BRANCH_W)), full((SMALL_W, BRANCH_W)), full((SMALL_W, BRANCH_W)),
                  full((128, 128))],
        out_specs=[pl.BlockSpec((1, tm, BRANCH_W), lambda bi, i: (bi, i, 0))] * 8,
        scratch_shapes=scratch,
        compiler_params=_cparams(2),
        name="rwkv_prep",
    )(proj3, proj3, proj3, small3, prev3, prev3, prev3, prev3,
      lw["mu_p"], lw["mu_p"], lw["mu_p"], lw["mu_p"],
      lw["w0"], lw["a0"], lw["k_k"], lw["k_a"], lw["r_k"],
      lw["w2p"], lw["a2p"], lw["g2p"], ones_bd)


def _rwkv_scan_kernel(r_ref, k_ref, v_ref, kk_ref, b_ref, ld_ref, g_ref, bon_ref,
                      lng_ref, lnb_ref, j_ref, s0_ref, *rest, L, group, n_earlier, single_step):
    earlier = rest[:n_earlier]
    o_ref, st_ref, s_ref = rest[n_earlier:]
    c_id = pl.program_id(1)

    @pl.when(c_id == 0)
    def _():
        z = jnp.zeros((R_HEAD, R_HEAD), F32)
        for p in range(R_PAIRS):
            top = jnp.concatenate([s0_ref[0, 0, 2 * p], z], axis=1)
            bot = jnp.concatenate([z, s0_ref[0, 0, 2 * p + 1]], axis=1)
            s_ref[p] = jnp.concatenate([top, bot], axis=0)

    ld_all = ld_ref[0]
    ti = lax.broadcasted_iota(jnp.int32, (L, L), 0)
    si = lax.broadcasted_iota(jnp.int32, (L, L), 1)
    cs_all = _cumsum_rows(ld_all, single_step)
    ec_all = jnp.exp(cs_all)
    enc_all = jnp.exp(-cs_all)
    ecm_all = jnp.exp(cs_all - ld_all)
    c_last_all = cs_all[L - 1:L, :]
    e_tail_all = jnp.exp(c_last_all - cs_all)
    g_last_all = jnp.exp(c_last_all)

    lane = lax.broadcasted_iota(jnp.int32, (L, 128), 1)
    head_a = lane < R_HEAD

    def stack(x):
        return jnp.concatenate([jnp.where(head_a, x, 0.0), jnp.where(head_a, 0.0, x)], axis=0)

    P2 = 2 * L
    ri = lax.broadcasted_iota(jnp.int32, (P2, P2), 0)
    ci = lax.broadcasted_iota(jnp.int32, (P2, P2), 1)
    strict = ri > ci
    incl = ri >= ci
    eye = jnp.where(ri == ci, 1.0, 0.0)
    ones_bd = j_ref[...]
    inv_n = 1.0 / R_HEAD

    cat0 = lambda a, b: jnp.concatenate([a, b], axis=0)
    cat1 = lambda a, b: jnp.concatenate([a, b], axis=1)

    for g0 in range(0, R_PAIRS, group):
        pairs = list(range(g0, g0 + group))
        sls = [slice(p * 128, (p + 1) * 128) for p in pairs]
        each = lambda f: [f(i) for i in range(group)]

        S = each(lambda i: s_ref[pairs[i]])
        Rs = each(lambda i: stack(r_ref[0, :, sls[i]] * ec_all[:, sls[i]]))
        Bs = each(lambda i: stack(kk_ref[0, :, sls[i]] * ecm_all[:, sls[i]]))
        Ks = each(lambda i: stack(k_ref[0, :, sls[i]] * enc_all[:, sls[i]]))
        As = each(lambda i: stack(-(b_ref[0, :, sls[i]] * enc_all[:, sls[i]])))
        Vs = each(lambda i: stack(v_ref[0, :, sls[i]]))
        Kt = each(lambda i: stack(k_ref[0, :, sls[i]] * e_tail_all[:, sls[i]]))
        At = each(lambda i: stack(-(b_ref[0, :, sls[i]] * e_tail_all[:, sls[i]])))

        if P2 % 128 == 0:
            sc = each(lambda i: _dot_nt(cat0(Bs[i], Rs[i]), cat0(As[i], Ks[i])))
            s_ba = each(lambda i: sc[i][:P2, :P2])
            s_bk = each(lambda i: sc[i][:P2, P2:])
            s_ra = each(lambda i: sc[i][P2:, :P2])
            s_rk = each(lambda i: sc[i][P2:, P2:])
        else:
            s_ba = each(lambda i: _dot_nt(Bs[i], As[i]))
            s_bk = each(lambda i: _dot_nt(Bs[i], Ks[i]))
            s_ra = each(lambda i: _dot_nt(Rs[i], As[i]))
            s_rk = each(lambda i: _dot_nt(Rs[i], Ks[i]))
        Nm = each(lambda i: jnp.where(strict, s_ba[i], 0.0))
        Mbk = each(lambda i: jnp.where(strict, s_bk[i], 0.0))
        Mra = each(lambda i: jnp.where(incl, s_ra[i], 0.0))
        Mrk = each(lambda i: jnp.where(incl, s_rk[i], 0.0))

        Tm = each(lambda i: eye + Nm[i])
        Pw = Nm
        span = 2
        while span < L and not single_step:
            Pw = [_dot(x, x) for x in Pw]
            Tm = each(lambda i: Tm[i] + _dot(Tm[i], Pw[i]))
            span *= 2

        mv = each(lambda i: _dot(cat0(Mbk[i], Mrk[i]), Vs[i]))
        tb = each(lambda i: _dot(Tm[i], cat1(Bs[i], mv[i][:P2])))
        mu = each(lambda i: _dot(Mra[i], tb[i]))
        Ro = each(lambda i: Rs[i] + mu[i][:, :128])
        uo = each(lambda i: _dot_nt(cat0(tb[i][:, :128], Ro[i]), S[i]))
        U = each(lambda i: uo[i][:P2] + tb[i][:, 128:])
        O = each(lambda i: uo[i][P2:] + mv[i][P2:] + mu[i][:, 128:])
        for i, p in enumerate(pairs):
            s_ref[p] = S[i] * g_last_all[:, sls[i]] + _dot_tn(
                cat0(U[i], Vs[i]), cat0(At[i], Kt[i]))

        for i in range(group):
            sl = sls[i]
            out = O[i][:L] + O[i][L:]
            mean = _segsum(out, ones_bd) * inv_n
            d = out - mean
            var = _segsum(d * d, ones_bd) * inv_n
            y = d * lax.rsqrt(var + R_GN_EPS) * lng_ref[:, sl] + lnb_ref[:, sl]
            o_ref[0, :, sl] = ((y + bon_ref[0, :, sl]) * g_ref[0, :, sl]).astype(BF16)

    @pl.when(c_id == pl.num_programs(1) - 1)
    def _():
        for p in range(R_PAIRS):
            sp = s_ref[p]
            st_ref[n_earlier, 0, 2 * p] = sp[:R_HEAD, :R_HEAD]
            st_ref[n_earlier, 0, 2 * p + 1] = sp[R_HEAD:, R_HEAD:]
        for i, e_ref in enumerate(earlier):
            st_ref[i] = e_ref[0]


def _rwkv_scan(seqs, g, bonus, ln_g, ln_b, ones_bd, s0, l, earlier_s, *, L, t_valid):
    bsz, t, _ = seqs[0].shape
    nl = len(earlier_s) + 1
    seq_spec = pl.BlockSpec((1, L, BRANCH_W), lambda bi, c: (bi, c, 0))
    vec_spec = pl.BlockSpec((1, BRANCH_W), lambda bi, c: (0, 0))
    st_ea = pl.BlockSpec((1, 1, R_HEADS, R_HEAD, R_HEAD), lambda bi, c: (0, bi, 0, 0, 0))
    return pl.pallas_call(
        functools.partial(_rwkv_scan_kernel, L=L, group=R_PAIRS, n_earlier=nl - 1,
                          single_step=t_valid == 1),
        out_shape=[jax.ShapeDtypeStruct((bsz, t, BRANCH_W), BF16),
                   jax.ShapeDtypeStruct((nl, bsz, R_HEADS, R_HEAD, R_HEAD), F32)],
        grid=(bsz, t // L),
        in_specs=[seq_spec] * 8 + [
            vec_spec, vec_spec, pl.BlockSpec((128, 128), lambda bi, c: (0, 0)),
            pl.BlockSpec((1, 1, R_HEADS, R_HEAD, R_HEAD), lambda bi, c: (l, bi, 0, 0, 0))]
        + [st_ea] * (nl - 1),
        out_specs=[seq_spec,
                   pl.BlockSpec((nl, 1, R_HEADS, R_HEAD, R_HEAD), lambda bi, c: (0, bi, 0, 0, 0))],
        scratch_shapes=[pltpu.VMEM((R_PAIRS, 128, 128), F32)],
        compiler_params=_cparams(2),
        name="rwkv_scan",
    )(*seqs, g, bonus, ln_g, ln_b, ones_bd, s0, *earlier_s)


def _rwkv_mixer(proj3, small3, prev, s0, l, lw, ones_bd, earlier_s, *, L, tm):
    bsz, t, _ = proj3.shape
    new_shift = jnp.concatenate(
        [proj3[:, -1, :3 * BRANCH_W].astype(F32), small3[:, -1, :R_COLS - 3 * BRANCH_W]], axis=-1)
    prev3 = jnp.pad(prev, ((0, 0), (0, 3 * BRANCH_W + SMALL_W - R_COLS)))[:, None, :]
    if t == 1:
        outs = _rwkv_prep(proj3.reshape(1, bsz, -1), small3.reshape(1, bsz, -1),
                          prev3.reshape(1, bsz, -1), lw, ones_bd, tm=tm)
        outs = [o.reshape(bsz, 1, BRANCH_W) for o in outs]
    else:
        outs = _rwkv_prep(proj3, small3, prev3, lw, ones_bd, tm=tm)
    tp = -(-t // L) * L
    if tp != t:
        outs = [jnp.pad(o, ((0, 0), (0, tp - t), (0, 0))) for o in outs]
    r, k2, v, kkn, b, ld, g, bonus = outs
    o_r, s_new = _rwkv_scan((r, k2, v, kkn, b, ld), g, bonus, lw["ln_g"], lw["ln_b"],
                            ones_bd, s0, l, earlier_s, L=L, t_valid=t)
    return o_r[:, :t].reshape(bsz * t, BRANCH_W), s_new, new_shift


def _mlstm_kernel(q_ref, k_ref, v_ref, o_ref, sm_ref, cq_ref, ck_ref, wq_ref, wk_ref,
                  bq_ref, bk_ref, ib_ref, fb_ref, ng_ref, c0_ref, n0_ref, m0_ref,
                  *rest, L, t_valid, n_earlier):
    earlier = rest[:n_earlier]
    out_ref, c_ref, n_ref, m_ref, qs_ref, ks_ref = rest[n_earlier:]
    last = n_earlier
    c_id = pl.program_id(1)

    @pl.when(c_id == 0)
    def _():
        for i, e_ref in enumerate(earlier):
            c_ref[i] = e_ref[0]
        c_ref[last] = c0_ref[0]
        n_ref[...] = n0_ref[0]
        m_ref[...] = m0_ref[0]
        qs_ref[5:8, :] = cq_ref[0, 0]
        ks_ref[5:8, :] = ck_ref[0, 0]

    qs_ref[8:8 + L, :] = _rows(q_ref, L)
    ks_ref[8:8 + L, :] = _rows(k_ref, L)
    conv_q = bq_ref[...]
    conv_k = bk_ref[...]
    for j in range(CONV_W):
        conv_q = conv_q + qs_ref[5 + j:5 + j + L, :] * wq_ref[j:j + 1, :]
        conv_k = conv_k + ks_ref[5 + j:5 + j + L, :] * wk_ref[j:j + 1, :]
    if L >= CONV_W - 1:
        tail_q = qs_ref[5 + L:8 + L, :]
        tail_k = ks_ref[5 + L:8 + L, :]
        qs_ref[5:8, :] = tail_q
        ks_ref[5:8, :] = tail_k

    row = lax.broadcasted_iota(jnp.int32, (L, 1), 0)
    valid = (c_id * L + row) < t_valid
    q_all = jnp.where(valid, _silu(conv_q), 0.0)
    k_all = jnp.where(valid, _silu(conv_k) * (M_DK ** -0.5), 0.0)
    v_all = jnp.where(valid, _rows(v_ref, L), 0.0)
    gate_o = _sigmoid(_rows(o_ref, L))

    sm = _rows(sm_ref, L)
    lane = lax.broadcasted_iota(jnp.int32, (L, SMALL_W), 1)
    head_lane = lane < M_HEADS
    i_pre = jnp.where(head_lane, pltpu.roll(sm, SMALL_W - S_I, axis=1), 0.0)
    f_pre = jnp.where(head_lane, pltpu.roll(sm, SMALL_W - S_F, axis=1), 0.0)
    ig4 = jnp.where(valid & head_lane, i_pre + ib_ref[...], NEG)
    lf4 = jnp.where(valid & head_lane, _log_sigmoid(f_pre + fb_ref[...]), 0.0)

    ti = lax.broadcasted_iota(jnp.int32, (L, L), 0)
    si = lax.broadcasted_iota(jnp.int32, (L, L), 1)
    causal = si <= ti
    diag = ti == si
    ones_l = jnp.ones((L, L), F32)
    F4 = _cumsum_rows(lf4, t_valid == 1)
    gmf4 = ig4 - F4

    heads = range(M_HEADS)
    each = lambda f: [f(h) for h in heads]
    sls = [slice(h * M_DK, (h + 1) * M_DK) for h in heads]
    rowsum = lambda x: jnp.sum(x, axis=-1, keepdims=True)
    q = each(lambda h: q_all[:, sls[h]])
    k = each(lambda h: k_all[:, sls[h]])
    v = each(lambda h: v_all[:, sls[h]])
    F = each(lambda h: rowsum(jnp.where(lane == h, F4, 0.0)))
    ig = each(lambda h: rowsum(jnp.where(lane == h, ig4, 0.0)))
    gmf = each(lambda h: rowsum(jnp.where(lane == h, gmf4, 0.0)))
    if t_valid == 1:
        g_row = each(lambda h: jnp.where(si == 0, gmf[h][0:1, :], NEG))
    else:
        g_row = each(lambda h: _dot_hi(
            ones_l, jnp.where(diag, jnp.broadcast_to(gmf[h], (L, L)), 0.0)))
    Dm = each(lambda h: jnp.where(causal, F[h] + g_row[h], NEG))

    C = each(lambda h: c_ref[last, 0, h])
    n = each(lambda h: n_ref[0, h])
    m_prev = each(lambda h: m_ref[0, h])
    inter = each(lambda h: F[h] + m_prev[h])
    m_t = each(lambda h: jnp.maximum(inter[h], jnp.max(Dm[h], axis=-1, keepdims=True)))
    w_inter = each(lambda h: jnp.exp(inter[h] - m_t[h]))
    Sm = each(lambda h: _dot_nt(q[h], k[h]) * jnp.exp(Dm[h] - m_t[h]))
    num = each(lambda h: w_inter[h] * _dot(q[h], C[h]) + _dot(Sm[h], v[h]))
    den = each(lambda h: w_inter[h] * rowsum(q[h] * n[h]) + rowsum(Sm[h]))
    hh = each(lambda h: num[h] / jnp.maximum(jnp.abs(den[h]), jnp.exp(-m_t[h])))

    FL = each(lambda h: F[h][L - 1:L, :])
    g_s = each(lambda h: FL[h] - F[h] + ig[h])
    m_new = each(lambda h: jnp.maximum(FL[h] + m_prev[h], jnp.max(g_s[h], axis=0, keepdims=True)))
    a_c = each(lambda h: jnp.exp(FL[h] + m_prev[h] - m_new[h]))
    kw = each(lambda h: k[h] * jnp.exp(g_s[h] - m_new[h]))
    for h in heads:
        c_ref[last, 0, h] = a_c[h] * C[h] + _dot_tn(kw[h], v[h])
        n_ref[0, h] = a_c[h] * n[h] + jnp.sum(kw[h], axis=0, keepdims=True)
        m_ref[0, h] = m_new[h]
        hn = hh[h] * lax.rsqrt(jnp.mean(hh[h] * hh[h], axis=-1, keepdims=True) + EPS)
        out_ref[0, :, sls[h]] = (gate_o[:, sls[h]] * hn * ng_ref[:, sls[h]]).astype(BF16)


def _rows(ref, L):
    x = ref[0].astype(F32)
    if x.shape[0] == L:
        return x
    assert x.shape[0] == 1
    row = lax.broadcasted_iota(jnp.int32, (L, x.shape[1]), 0)
    return jnp.where(row == 0, x, 0.0)


def _mlstm_mixer(proj3, small3, conv_buf, c0, n0, m0, l, lw, earlier_c, *, L):
    bsz, t, _ = proj3.shape
    tb = min(t, L)
    nc = -(-t // L)
    nl = len(earlier_c) + 1
    seq = lambda col: pl.BlockSpec((1, tb, BRANCH_W), lambda bi, c: (bi, c, col // BRANCH_W))
    st_c = pl.BlockSpec((nl, 1, M_HEADS, M_DK, M_DK), lambda bi, c: (0, bi, 0, 0, 0))
    ea_c = pl.BlockSpec((1, 1, M_HEADS, M_DK, M_DK), lambda bi, c: (0, bi, 0, 0, 0))
    st_n = pl.BlockSpec((1, M_HEADS, 1, M_DK), lambda bi, c: (bi, 0, 0, 0))
    st_m = pl.BlockSpec((1, M_HEADS, 1, 1), lambda bi, c: (bi, 0, 0, 0))
    in_c = pl.BlockSpec((1, 1, M_HEADS, M_DK, M_DK), lambda bi, c: (l, bi, 0, 0, 0))
    in_n = pl.BlockSpec((1, 1, M_HEADS, 1, M_DK), lambda bi, c: (l, bi, 0, 0, 0))
    in_m = pl.BlockSpec((1, 1, M_HEADS, 1, 1), lambda bi, c: (l, bi, 0, 0, 0))
    conv = lambda blk: pl.BlockSpec((1, 1, CONV_W - 1, BRANCH_W), lambda bi, c: (l, bi, 0, blk))
    cw = lambda blk: pl.BlockSpec((CONV_W, BRANCH_W), lambda bi, c: (0, blk))
    vec = lambda blk: pl.BlockSpec((1, BRANCH_W), lambda bi, c: (0, blk))
    hb = pl.BlockSpec((1, SMALL_W), lambda bi, c: (0, 0))
    pad_heads = lambda a: jnp.pad(a, ((0, 0), (0, SMALL_W - M_HEADS)))
    out, c_new, n_new, m_new = pl.pallas_call(
        functools.partial(_mlstm_kernel, L=L, t_valid=t, n_earlier=nl - 1),
        out_shape=[jax.ShapeDtypeStruct((bsz, nc * L, BRANCH_W), BF16),
                   jax.ShapeDtypeStruct((nl, bsz, M_HEADS, M_DK, M_DK), F32),
                   jax.ShapeDtypeStruct((bsz, M_HEADS, 1, M_DK), F32),
                   jax.ShapeDtypeStruct((bsz, M_HEADS, 1, 1), F32)],
        grid=(bsz, nc),
        in_specs=[seq(C_MQK), seq(C_MQK + BRANCH_W), seq(C_MV), seq(C_MO),
                  pl.BlockSpec((1, tb, SMALL_W), lambda bi, c: (bi, c, 0)),
                  conv(0), conv(1), cw(0), cw(1), vec(0), vec(1), hb, hb, vec(0),
                  in_c, in_n, in_m] + [ea_c] * (nl - 1),
        out_specs=[pl.BlockSpec((1, L, BRANCH_W), lambda bi, c: (bi, c, 0)), st_c, st_n, st_m],
        scratch_shapes=[pltpu.VMEM((L + 8, BRANCH_W), F32), pltpu.VMEM((L + 8, BRANCH_W), F32)],
        compiler_params=_cparams(2),
        name="mlstm_scan",
    )(proj3, proj3, proj3, proj3, small3, conv_buf, conv_buf,
      lw["conv_w"], lw["conv_w"], lw["conv_b"], lw["conv_b"],
      pad_heads(lw["i_b"]), pad_heads(lw["f_b"]),
      lw["m_norm_g"], c0, n0.reshape(DEPTH, bsz, M_HEADS, 1, M_DK),
      m0.reshape(DEPTH, bsz, M_HEADS, 1, 1), *earlier_c)
    return out, c_new, n_new.reshape(bsz, M_HEADS, M_DK), m_new.reshape(bsz, M_HEADS)


def _gla_kernel(q_ref, k_ref, v_ref, og_ref, sm_ref, a2_ref, ab_ref, ng_ref, s0_ref,
                *rest, L, t_valid, n_earlier):
    earlier = rest[:n_earlier]
    out_ref, s_ref, b_scr, q_scr = rest[n_earlier:]
    last = n_earlier
    c_id = pl.program_id(1)

    @pl.when(c_id == 0)
    def _():
        for i, e_ref in enumerate(earlier):
            s_ref[i] = e_ref[0]
        s_ref[last] = s0_ref[0]

    row = lax.broadcasted_iota(jnp.int32, (L, 1), 0)
    valid = (c_id * L + row) < t_valid
    q_all = jnp.where(valid, _rows(q_ref, L) * (G_DK ** -0.5), 0.0)
    k_all = jnp.where(valid, _rows(k_ref, L), 0.0)
    v_all = jnp.where(valid, _rows(v_ref, L), 0.0)
    gate_o = _silu(_rows(og_ref, L))
    lg = _log_sigmoid(_dot(_rows(sm_ref, L), a2_ref[...]) + ab_ref[...]) * (1.0 / G_GATE_NORM)
    lg = jnp.where(valid, lg, 0.0)

    ti = lax.broadcasted_iota(jnp.int32, (L, L), 0)
    si = lax.broadcasted_iota(jnp.int32, (L, L), 1)
    b_all = _cumsum_rows(lg, t_valid == 1)
    b_scr[...] = b_all
    q_scr[...] = q_all
    eb_all = jnp.exp(b_all)
    b_last_all = b_all[L - 1:L, :]
    e_tail_all = jnp.exp(b_last_all - b_all)
    ones_lv = jnp.ones((L, G_DV), F32)

    n_t = L if t_valid >= L else t_valid
    s_col = {rows: lax.broadcasted_iota(jnp.int32, (rows, 1), 0) for rows in range(8, L + 1, 8)}
    t_lane = {rows: lax.broadcasted_iota(jnp.int32, (rows, L), 1) for rows in range(8, L + 1, 8)}

    for h in range(G_HEADS):
        sl = slice(h * G_DK, (h + 1) * G_DK)
        sv = slice(h * G_DV, (h + 1) * G_DV)
        q, k, v, b = q_all[:, sl], k_all[:, sl], v_all[:, sv], b_all[:, sl]

        at = jnp.zeros((L, L), F32)
        for t in range(n_t):
            rows = 8 * (t // 8 + 1)
            bt = b_scr[t:t + 1, sl]
            qt = q_scr[t:t + 1, sl]
            e = jnp.exp(jnp.where(s_col[rows] <= t, bt - b[:rows], NEG))
            col = jnp.sum(qt * k[:rows] * e, axis=-1, keepdims=True)
            top = jnp.where(t_lane[rows] == t, col, at[:rows])
            at = top if rows == L else jnp.concatenate([top, at[rows:]], axis=0)

        S = s_ref[last, 0, h]
        o = _dot(q * eb_all[:, sl], S) + _dot_tn(at, v)
        decay = jnp.exp(lax.dot_general(lg[:, sl], ones_lv, (((0,), (0,)), ((), ())),
                                        precision=lax.Precision.HIGHEST,
                                        preferred_element_type=F32))
        s_ref[last, 0, h] = S * decay + _dot_tn(k * e_tail_all[:, sl], v)

        on = o * lax.rsqrt(jnp.mean(o * o, axis=-1, keepdims=True) + EPS) * ng_ref[:, sv]
        out_ref[0, :, sv] = (on * gate_o[:, sv]).astype(BF16)


def _gla_mixer(proj3, small3, s0, l, lw, earlier_s, *, L):
    bsz, t, _ = proj3.shape
    tb = min(t, L)
    nc = -(-t // L)
    nl = len(earlier_s) + 1
    gw = G_HEADS * G_DK
    st = pl.BlockSpec((nl, 1, G_HEADS, G_DK, G_DV), lambda bi, c: (0, bi, 0, 0, 0))
    st_ea = pl.BlockSpec((1, 1, G_HEADS, G_DK, G_DV), lambda bi, c: (0, bi, 0, 0, 0))
    st_in = pl.BlockSpec((1, 1, G_HEADS, G_DK, G_DV), lambda bi, c: (l, bi, 0, 0, 0))
    return pl.pallas_call(
        functools.partial(_gla_kernel, L=L, t_valid=t, n_earlier=nl - 1),
        out_shape=[jax.ShapeDtypeStruct((bsz, nc * L, BRANCH_W), BF16),
                   jax.ShapeDtypeStruct((nl, bsz, G_HEADS, G_DK, G_DV), F32)],
        grid=(bsz, nc),
        in_specs=[pl.BlockSpec((1, tb, gw), lambda bi, c: (bi, c, C_GQ // gw)),
                  pl.BlockSpec((1, tb, gw), lambda bi, c: (bi, c, C_GQ // gw + 1)),
                  pl.BlockSpec((1, tb, BRANCH_W), lambda bi, c: (bi, c, C_GV // BRANCH_W)),
                  pl.BlockSpec((1, tb, BRANCH_W), lambda bi, c: (bi, c, C_GOG // BRANCH_W)),
                  pl.BlockSpec((1, tb, SMALL_W), lambda bi, c: (bi, c, 0)),
                  pl.BlockSpec((SMALL_W, gw), lambda bi, c: (0, 0)),
                  pl.BlockSpec((1, gw), lambda bi, c: (0, 0)),
                  pl.BlockSpec((1, BRANCH_W), lambda bi, c: (0, 0)),
                  st_in] + [st_ea] * (nl - 1),
        out_specs=[pl.BlockSpec((1, L, BRANCH_W), lambda bi, c: (bi, c, 0)), st],
        scratch_shapes=[pltpu.VMEM((L, gw), F32), pltpu.VMEM((L, gw), F32)],
        compiler_params=_cparams(2),
        name="gla_scan",
    )(proj3, proj3, proj3, proj3, small3, lw["g_a2p"], lw["g_a_b"], lw["g_norm_g"], s0,
      *earlier_s)


PACK_TN = 512
PACK_SRC_W = PACK_TN + 128
W_R0, W_M0, W_G0, W_T0 = 0, R_COLS, R_COLS + 4104, R_COLS + 4104 + 3088
PACK_RUNS = ((C_RWKV, W_R0, 3072), (C_MQK, W_M0, 3072), (C_MO, W_M0 + 3080, 1024),
             (C_GQ, W_G0, 2048), (C_GOG, W_G0 + 2064, 1024 + 3 * D_MODEL))


def _pack_kernel(win_ref, shift_ref, src_ref, tail_ref, small_ref, o_ref, *, n_win):
    del win_ref
    j = pl.program_id(2)

    @pl.when(j < n_win)
    def _():
        o_ref[0] = pltpu.roll(src_ref[0], PACK_SRC_W - shift_ref[j], axis=1)[:, :PACK_TN]

    @pl.when(j == n_win)
    def _():
        o_ref[0] = tail_ref[0]

    @pl.when(j > n_win)
    def _():
        o_ref[0] = small_ref[0]


def _pack_w_in(w, *, tk=1024):
    depth, k, n_in = w.shape
    starts = []
    for p0, s0, width in PACK_RUNS:
        assert p0 == len(starts) * PACK_TN and width % PACK_TN == 0
        starts += list(range(s0, s0 + width, PACK_TN))
    assert len(starts) * PACK_TN == C_SMALL and starts[-1] + PACK_TN == n_in
    tail = w[..., starts[-1]:]
    starts = starts[:-1]
    n_win = len(starts)
    assert all(s // 128 * 128 + PACK_SRC_W <= n_in for s in starts)
    win = jnp.asarray([s // 128 for s in starts] + [0, 0], jnp.int32)
    shift = jnp.asarray([s % 128 for s in starts] + [0, 0], jnp.int32)
    small = jnp.concatenate(
        [w[..., W_R0 + 3072:W_R0 + 3264], w[..., W_M0 + 3072:W_M0 + 3080],
         w[..., W_G0 + 2048:W_G0 + 2064],
         jnp.zeros((depth, k, PACK_TN - (R_COLS - 3072) - 2 * M_HEADS - G_LR), w.dtype)], axis=-1)
    tile = pl.BlockSpec((1, tk, PACK_TN), lambda l, i, j, win, shift: (l, i, 0))
    grid_spec = pltpu.PrefetchScalarGridSpec(
        num_scalar_prefetch=2,
        grid=(depth, k // tk, n_win + 2),
        in_specs=[pl.BlockSpec((pl.Element(1), pl.Element(tk), pl.Element(PACK_SRC_W)),
                               lambda l, i, j, win, shift: (l, i * tk, win[j] * 128)),
                  tile, tile],
        out_specs=pl.BlockSpec((1, tk, PACK_TN), lambda l, i, j, win, shift: (l, i, j)),
    )
    return pl.pallas_call(
        functools.partial(_pack_kernel, n_win=n_win),
        out_shape=jax.ShapeDtypeStruct((depth, k, N_PACKED), w.dtype),
        grid_spec=grid_spec,
        compiler_params=_cparams(3),
        name="pack_w_in",
    )(win, shift, w, tail, small)


def _rows_padded(w, row0, total):
    return jnp.pad(w, ((row0, total - row0 - w.shape[0]), (0, 0)))


def _layer_weights(l, P):
    mu = P["rwkv_mu"][l]
    mu_p = jnp.concatenate([mu, jnp.zeros((3 * BRANCH_W + SMALL_W - R_COLS,), F32)]).reshape(1, -1)
    row = lambda a: a.reshape(1, -1)
    return {
        "norm1_g": P["norm1_g"][l], "gate_b": P["gate_b"][l],
        "mu_p": mu_p, "w0": row(P["rwkv_w0"][l]), "a0": row(P["rwkv_a0"][l]),
        "k_k": row(P["rwkv_k_k"][l]), "k_a": row(P["rwkv_k_a"][l]), "r_k": row(P["rwkv_r_k"][l]),
        "w2p": _rows_padded(P["rwkv_w2"][l], 0, SMALL_W),
        "a2p": _rows_padded(P["rwkv_a2"][l], R_LORA, SMALL_W),
        "g2p": _rows_padded(P["rwkv_g2"][l], 2 * R_LORA, SMALL_W),
        "ln_g": row(P["rwkv_ln_g"][l]), "ln_b": row(P["rwkv_ln_b"][l]),
        "conv_w": P["mlstm_conv_w"][l], "conv_b": row(P["mlstm_conv_b"][l]),
        "i_b": row(P["mlstm_i_b"][l]), "f_b": row(P["mlstm_f_b"][l]),
        "m_norm_g": row(P["mlstm_norm_g"][l]),
        "g_a2p": _rows_padded(P["gla_a2"][l], S_GXA, SMALL_W), "g_a_b": row(P["gla_a_b"][l]),
        "g_norm_g": row(P["gla_norm_g"][l]),
        "norm2_g": P["norm2_g"][l],
    }


def _layer(x2, bsz, t, states, l, lw, big, ones_bd, cfg, earlier):
    ea_wkv, ea_c, ea_s = earlier
    rw_prev, rw_s, m_conv, m_c, m_n, m_m, g_s = states
    m = bsz * t
    L, tm = cfg["L"], cfg["tm"]
    proj, small = _rms_matmul(x2, lw["norm1_g"], big["w_in"], l, tm=tm, tn=512,
                              main_dtype=cfg["proj_dtype"])
    proj3 = proj.reshape(bsz, t, C_SMALL)
    small3 = small.reshape(bsz, t, -1)

    o_r, rw_s_new, rw_prev_new = _rwkv_mixer(proj3, small3, rw_prev[l], rw_s, l, lw, ones_bd,
                                             ea_wkv, L=L, tm=cfg["tm_prep"])

    o_m, m_c_new, m_n_new, m_m_new = _mlstm_mixer(proj3, small3, m_conv, m_c, m_n, m_m, l, lw,
                                                  ea_c, L=L)
    qk_tail = proj3[:, -min(t, CONV_W - 1):, C_MQK:C_MQK + 2 * BRANCH_W].astype(F32)
    m_conv_new = jnp.concatenate([m_conv[l], qk_tail], axis=1)[:, -(CONV_W - 1):]
    o_g, g_s_new = _gla_mixer(proj3, small3, g_s, l, lw, ea_s, L=L)
    o_m = o_m[:, :t].reshape(m, BRANCH_W)
    o_g = o_g[:, :t].reshape(m, BRANCH_W)

    merged = _merge(o_r, o_m, o_g, big["w_branch"], l, proj, lw["gate_b"],
                    tm=cfg["tm_merge"], tn=512)
    x2 = _matmul_residual(merged, big["w_out"], l, x2, tm=tm, tn=512, tk=D_MODEL)
    hidden = _rms_swiglu(x2, lw["norm2_g"], big["w_gu"], l, tm=tm, tn=512)
    x2 = _matmul_residual(hidden, big["w_down"], l, x2, tm=tm, tn=512, tk=2816)
    return x2, (rw_prev_new, rw_s_new, m_conv_new, m_c_new, m_n_new, m_m_new, g_s_new)


def _trunk(x, states, layer_ws, big, final_g, ones_bd, cfg):
    bsz, t, d = x.shape
    x2 = x.reshape(bsz * t, d)
    per_layer = []
    for l in range(DEPTH):
        is_last = l == DEPTH - 1
        earlier = tuple([st[i] for st in per_layer] if is_last else [] for i in BIG_STATES)
        x2, new = _layer(x2, bsz, t, states, l, layer_ws[l], big, ones_bd, cfg, earlier)
        per_layer.append(new)
    new_states = [per_layer[-1][i] if i in BIG_STATES
                  else jnp.stack([st[i] for st in per_layer], axis=0) for i in range(len(states))]
    y = _rmsnorm(x2, final_g, tm=cfg["tm_norm"]).reshape(bsz, t, d)
    return y, new_states


BIG_STATES = (1, 3, 6)

PROMPT_CFG = dict(L=64, tm=1024, tm_prep=256, tm_merge=512, tm_norm=512, proj_dtype=BF16)
SAMPLE_CFG = dict(L=16, tm=128, tm_prep=128, tm_merge=128, tm_norm=128, proj_dtype=F32)


def _zero_states(bsz):
    return (jnp.zeros((DEPTH, bsz, R_COLS), F32),
            jnp.zeros((DEPTH, bsz, R_HEADS, R_HEAD, R_HEAD), F32),
            jnp.zeros((DEPTH, bsz, CONV_W - 1, 2 * BRANCH_W), F32),
            jnp.zeros((DEPTH, bsz, M_HEADS, M_DK, M_DK), F32),
            jnp.zeros((DEPTH, bsz, M_HEADS, M_DK), F32),
            jnp.zeros((DEPTH, bsz, M_HEADS), F32),
            jnp.zeros((DEPTH, bsz, G_HEADS, G_DK, G_DV), F32))


def kernel(x_prompt, x_sample, state_rwkv_shift, state_rwkv_wkv, state_mlstm_conv, state_mlstm_C, state_mlstm_n, state_mlstm_m, state_gla_S, norm1_g, w_in, gate_b, rwkv_mu, rwkv_w0, rwkv_w2, rwkv_a0, rwkv_a2, rwkv_g2, rwkv_k_k, rwkv_k_a, rwkv_r_k, rwkv_ln_g, rwkv_ln_b, mlstm_conv_w, mlstm_conv_b, mlstm_i_b, mlstm_f_b, mlstm_norm_g, gla_a2, gla_a_b, gla_norm_g, w_branch, w_out, norm2_g, ffn_w_gu, ffn_w_down, final_norm_g):
    P = dict(norm1_g=norm1_g, w_in=w_in, gate_b=gate_b, rwkv_mu=rwkv_mu, rwkv_w0=rwkv_w0,
             rwkv_w2=rwkv_w2, rwkv_a0=rwkv_a0, rwkv_a2=rwkv_a2, rwkv_g2=rwkv_g2,
             rwkv_k_k=rwkv_k_k, rwkv_k_a=rwkv_k_a, rwkv_r_k=rwkv_r_k, rwkv_ln_g=rwkv_ln_g,
             rwkv_ln_b=rwkv_ln_b, mlstm_conv_w=mlstm_conv_w, mlstm_conv_b=mlstm_conv_b,
             mlstm_i_b=mlstm_i_b, mlstm_f_b=mlstm_f_b, mlstm_norm_g=mlstm_norm_g,
             gla_a2=gla_a2, gla_a_b=gla_a_b, gla_norm_g=gla_norm_g, w_branch=w_branch,
             w_out=w_out, norm2_g=norm2_g, ffn_w_gu=ffn_w_gu, ffn_w_down=ffn_w_down)
    layer_ws = [_layer_weights(l, P) for l in range(DEPTH)]
    big = dict(w_in=_pack_w_in(w_in), w_branch=w_branch, w_out=w_out, w_gu=ffn_w_gu,
               w_down=ffn_w_down)
    head_of_lane = jnp.arange(128) // R_HEAD
    ones_bd = (head_of_lane[:, None] == head_of_lane[None, :]).astype(BF16)

    y_p, p_states = _trunk(x_prompt, _zero_states(x_prompt.shape[0]), layer_ws, big,
                           final_norm_g, ones_bd, PROMPT_CFG)
    s_states = (state_rwkv_shift, state_rwkv_wkv, state_mlstm_conv, state_mlstm_C,
                state_mlstm_n, state_mlstm_m, state_gla_S)
    y_s, s_states = _trunk(x_sample, s_states, layer_ws, big, final_norm_g, ones_bd, SAMPLE_CFG)
    return (y_p, y_s, *p_states, *s_states)
```

```python
import functools

import jax
import jax.numpy as jnp
from jax import lax
from jax.experimental import pallas as pl
from jax.experimental.pallas import tpu as pltpu

F32 = jnp.float32
BF16 = jnp.bfloat16

D_MODEL = 2048
DEPTH = 2
BRANCH_W = 1024
R_HEADS, R_HEAD = 16, 64
R_PAIRS = R_HEADS // 2
R_LORA = 64
R_COLS = 3 * BRANCH_W + 3 * R_LORA
R_GN_EPS = 64e-5
M_HEADS, M_DK = 4, 256
CONV_W = 4
G_HEADS, G_DK, G_DV = 4, 128, 256
G_LR = 16
G_GATE_NORM = 16.0
D_FF = 5632
EPS = 1e-6
NEG = -1e30

C_RWKV = 0
C_MQK = 3072
C_MV = 5120
C_MO = 6144
C_GQ = 7168
C_GV = 8192
C_GOG = 9216
C_GATE = 10240
C_SMALL = 16384
SMALL_W = 256
S_I, S_F, S_GXA = 192, 196, 200
MAIN_TN = 512
W_R0, W_M0, W_G0, W_T0 = 0, R_COLS, R_COLS + 4104, R_COLS + 4104 + 3088
MAIN_RUNS = ((C_RWKV, W_R0, 3072), (C_MQK, W_M0, 3072), (C_MO, W_M0 + 3080, 1024),
             (C_GQ, W_G0, 2048), (C_GOG, W_G0 + 2064, 1024 + 3 * D_MODEL))

VMEM_LIMIT = 56 * 1024 * 1024


def _cparams(n_axes):
    return pltpu.CompilerParams(dimension_semantics=("arbitrary",) * n_axes,
                                vmem_limit_bytes=VMEM_LIMIT)


def _dot(a, b):
    return jnp.dot(a.astype(BF16), b.astype(BF16), preferred_element_type=F32)


def _dot_nt(a, b):
    return lax.dot_general(a.astype(BF16), b.astype(BF16), (((1,), (1,)), ((), ())),
                           preferred_element_type=F32)


def _dot_tn(a, b):
    return lax.dot_general(a.astype(BF16), b.astype(BF16), (((0,), (0,)), ((), ())),
                           preferred_element_type=F32)


def _dot_hi(a, b):
    return jnp.dot(a, b, precision=lax.Precision.HIGHEST, preferred_element_type=F32)


def _cumsum_rows(x, single_step):
    n = x.shape[0]
    if single_step:
        row = lax.broadcasted_iota(jnp.int32, x.shape, 0)
        return jnp.where(row == 0, x, x[0:1, :])
    ti = lax.broadcasted_iota(jnp.int32, (n, n), 0)
    si = lax.broadcasted_iota(jnp.int32, (n, n), 1)
    return _dot_hi((si <= ti).astype(F32), x)


def _segsum(y, ones_blockdiag):
    hi = y.astype(BF16)
    lo = (y - hi.astype(F32)).astype(BF16)
    return (jnp.dot(hi, ones_blockdiag, preferred_element_type=F32)
            + jnp.dot(lo, ones_blockdiag, preferred_element_type=F32))


def _sigmoid(x):
    return 1.0 / (1.0 + jnp.exp(-x))


def _silu(x):
    return x * _sigmoid(x)


def _log_sigmoid(x):
    return -_softplus(-x)


def _softplus(x):
    return jnp.maximum(x, 0.0) + jnp.log(1.0 + jnp.exp(-jnp.abs(x)))


def _rms_mm_kernel(start_ref, x_ref, g_ref, wt_ref, wsm_ref, o_ref, osm_ref, h_scr, *, n_main):
    del start_ref
    j = pl.program_id(1)

    @pl.when(j == 0)
    def _():
        x = x_ref[...]
        ms = jnp.mean(x * x, axis=-1, keepdims=True)
        h_scr[...] = (x * lax.rsqrt(ms + EPS) * g_ref[...]).astype(BF16)

    @pl.when(j < n_main)
    def _():
        o_ref[...] = _dot_nt(h_scr[...], wt_ref[0]).astype(o_ref.dtype)

    @pl.when(j >= n_main)
    def _():
        osm_ref[...] = _dot_nt(h_scr[...], wsm_ref[0])


def _rms_matmul(x, g, wt, wt_small, l, *, tm, main_dtype):
    m, k = x.shape
    n_in = wt.shape[1]
    tn = MAIN_TN
    starts = []
    for p0, s0, width in MAIN_RUNS:
        assert p0 == len(starts) * tn and width % tn == 0 and s0 % 8 == 0
        starts += list(range(s0, s0 + width, tn))
    n_main = len(starts)
    assert n_main * tn == C_SMALL and starts[-1] + tn == n_in
    grid_spec = pltpu.PrefetchScalarGridSpec(
        num_scalar_prefetch=1,
        grid=(m // tm, n_main + 1),
        in_specs=[pl.BlockSpec((tm, k), lambda i, j, st: (i, 0)),
                  pl.BlockSpec((1, k), lambda i, j, st: (0, 0)),
                  pl.BlockSpec((pl.Element(1), pl.Element(tn), pl.Element(k)),
                               lambda i, j, st: (l, st[j] * 8, 0)),
                  pl.BlockSpec((1, tn, k), lambda i, j, st: (l, 0, 0))],
        out_specs=[pl.BlockSpec((tm, tn), lambda i, j, st: (i, jnp.minimum(j, n_main - 1))),
                   pl.BlockSpec((tm, tn), lambda i, j, st: (i, 0))],
        scratch_shapes=[pltpu.VMEM((tm, k), BF16)],
    )
    return pl.pallas_call(
        functools.partial(_rms_mm_kernel, n_main=n_main),
        out_shape=[jax.ShapeDtypeStruct((m, C_SMALL), main_dtype),
                   jax.ShapeDtypeStruct((m, tn), F32)],
        grid_spec=grid_spec,
        compiler_params=_cparams(2),
        name="rms_in_proj",
    )(jnp.asarray([s // 8 for s in starts] + [0], jnp.int32), x, g.reshape(1, k), wt, wt_small)


def _rms_swiglu_kernel(x_ref, g_ref, wg_ref, wu_ref, o_ref, h_scr):
    @pl.when(pl.program_id(1) == 0)
    def _():
        x = x_ref[...]
        ms = jnp.mean(x * x, axis=-1, keepdims=True)
        h_scr[...] = (x * lax.rsqrt(ms + EPS) * g_ref[...]).astype(BF16)

    h = h_scr[...]
    gg = jnp.dot(h, wg_ref[0].astype(BF16), preferred_element_type=F32)
    uu = jnp.dot(h, wu_ref[0].astype(BF16), preferred_element_type=F32)
    o_ref[...] = (_silu(gg) * uu).astype(BF16)


def _rms_swiglu(x, g, w_gu, l, *, tm, tn):
    m, k = x.shape
    nj = D_FF // tn
    return pl.pallas_call(
        _rms_swiglu_kernel,
        out_shape=jax.ShapeDtypeStruct((m, D_FF), BF16),
        grid=(m // tm, nj),
        in_specs=[pl.BlockSpec((tm, k), lambda i, j: (i, 0)),
                  pl.BlockSpec((1, k), lambda i, j: (0, 0)),
                  pl.BlockSpec((1, k, tn), lambda i, j: (l, 0, j)),
                  pl.BlockSpec((1, k, tn), lambda i, j: (l, 0, j + nj))],
        out_specs=pl.BlockSpec((tm, tn), lambda i, j: (i, j)),
        scratch_shapes=[pltpu.VMEM((tm, k), BF16)],
        compiler_params=_cparams(2),
        name="rms_ffn_swiglu",
    )(x, g.reshape(1, k), w_gu, w_gu)


def _mm_res_kernel(a_ref, w_ref, res_ref, o_ref, acc_ref, *, nk):
    kk = pl.program_id(2)

    @pl.when(kk == 0)
    def _():
        acc_ref[...] = jnp.zeros_like(acc_ref)

    acc_ref[...] += jnp.dot(a_ref[...], w_ref[0].astype(BF16), preferred_element_type=F32)

    @pl.when(kk == nk - 1)
    def _():
        o_ref[...] = acc_ref[...] + res_ref[...]


def _matmul_residual(a, w, l, res, *, tm, tn, tk):
    m, k = a.shape
    n = w.shape[2]
    nk = k // tk
    return pl.pallas_call(
        functools.partial(_mm_res_kernel, nk=nk),
        out_shape=jax.ShapeDtypeStruct((m, n), F32),
        grid=(m // tm, n // tn, nk),
        in_specs=[pl.BlockSpec((tm, tk), lambda i, j, kk: (i, kk)),
                  pl.BlockSpec((1, tk, tn), lambda i, j, kk: (l, kk, j)),
                  pl.BlockSpec((tm, tn), lambda i, j, kk: (i, j))],
        out_specs=pl.BlockSpec((tm, tn), lambda i, j, kk: (i, j)),
        scratch_shapes=[pltpu.VMEM((tm, tn), F32)],
        compiler_params=_cparams(3),
        name="matmul_residual",
    )(a, w, res)


def _merge_kernel(or_ref, om_ref, og_ref, wr_ref, wm_ref, wg_ref,
                  gr_ref, gm_ref, gg_ref, br_ref, bm_ref, bg_ref, o_ref):
    acc = _sigmoid(gr_ref[...].astype(F32) + br_ref[0]) * jnp.dot(
        or_ref[...], wr_ref[0, 0].astype(BF16), preferred_element_type=F32)
    acc += _sigmoid(gm_ref[...].astype(F32) + bm_ref[0]) * jnp.dot(
        om_ref[...], wm_ref[0, 0].astype(BF16), preferred_element_type=F32)
    acc += _sigmoid(gg_ref[...].astype(F32) + bg_ref[0]) * jnp.dot(
        og_ref[...], wg_ref[0, 0].astype(BF16), preferred_element_type=F32)
    o_ref[...] = acc.astype(BF16)


def _merge(o_r, o_m, o_g, w_branch, l, proj, gate_b, *, tm, tn):
    m = o_r.shape[0]
    gate_blk = C_GATE // tn
    per = D_MODEL // tn
    o_spec = pl.BlockSpec((tm, BRANCH_W), lambda i, j: (i, 0))

    def w_spec(b):
        return pl.BlockSpec((1, 1, BRANCH_W, tn), lambda i, j: (l, b, 0, j))

    def g_spec(b):
        return pl.BlockSpec((tm, tn), lambda i, j: (i, gate_blk + b * per + j))

    def b_spec(b):
        return pl.BlockSpec((1, 1, tn), lambda i, j: (b, 0, j))

    gate_b = gate_b.reshape(3, 1, D_MODEL)

    return pl.pallas_call(
        _merge_kernel,
        out_shape=jax.ShapeDtypeStruct((m, D_MODEL), BF16),
        grid=(m // tm, per),
        in_specs=[o_spec, o_spec, o_spec, w_spec(0), w_spec(1), w_spec(2),
                  g_spec(0), g_spec(1), g_spec(2), b_spec(0), b_spec(1), b_spec(2)],
        out_specs=pl.BlockSpec((tm, tn), lambda i, j: (i, j)),
        compiler_params=_cparams(2),
        name="gated_merge",
    )(o_r, o_m, o_g, w_branch, w_branch, w_branch, proj, proj, proj, gate_b, gate_b, gate_b)


def _rmsnorm_kernel(x_ref, g_ref, o_ref):
    x = x_ref[...]
    ms = jnp.mean(x * x, axis=-1, keepdims=True)
    o_ref[...] = x * lax.rsqrt(ms + EPS) * g_ref[...]


def _rmsnorm(x, g, *, tm):
    m, k = x.shape
    return pl.pallas_call(
        _rmsnorm_kernel,
        out_shape=jax.ShapeDtypeStruct((m, k), F32),
        grid=(m // tm,),
        in_specs=[pl.BlockSpec((tm, k), lambda i: (i, 0)),
                  pl.BlockSpec((1, k), lambda i: (0, 0))],
        out_specs=pl.BlockSpec((tm, k), lambda i: (i, 0)),
        compiler_params=_cparams(1),
        name="final_rmsnorm",
    )(x, g.reshape(1, k))


def _rwkv_prep_kernel(pr_ref, pk_ref, pv_ref, ps_ref, qr_ref, qk_ref, qv_ref, qs_ref,
                      mur_ref, muk_ref, muv_ref, mus_ref,
                      w0_ref, a0_ref, kk_ref, ka_ref, rk_ref,
                      w2_ref, a2_ref, g2_ref, j_ref,
                      r_out, k_out, v_out, kkn_out, b_out, ld_out, g_out, bon_out,
                      *scratch, tm, explicit_prev):
    def shift(p_ref, q_ref, mu_ref, scr):
        p = p_ref[0].astype(F32)
        if explicit_prev:
            prev = q_ref[0]
        else:
            @pl.when(pl.program_id(1) == 0)
            def _():
                scr[7:8, :] = q_ref[0]

            scr[8:8 + tm, :] = p
            prev = scr[7:7 + tm, :]
            scr[7:8, :] = p[tm - 1:tm, :]
        return p + (prev - p) * mu_ref[...]

    scr = scratch if scratch else (None,) * 4
    xr = shift(pr_ref, qr_ref, mur_ref, scr[0])
    xk = shift(pk_ref, qk_ref, muk_ref, scr[1])
    xv = shift(pv_ref, qv_ref, muv_ref, scr[2])
    xs = shift(ps_ref, qs_ref, mus_ref, scr[3])

    w = -_softplus(-(w0_ref[...] + _dot(jnp.tanh(xs), w2_ref[...]))) - 0.5
    ld_out[0] = -jnp.exp(w)
    a = _sigmoid(a0_ref[...] + _dot(xs, a2_ref[...]))
    g_out[0] = _dot(_sigmoid(xs), g2_ref[...])

    ones_bd = j_ref[...]
    kkr = xk * kk_ref[...]
    k2 = xk * (1.0 + (a - 1.0) * ka_ref[...])
    rkk = xr * k2 * rk_ref[...]
    for p in range(R_PAIRS):
        sl = slice(p * 128, (p + 1) * 128)
        kb = kkr[:, sl]
        nrm = jnp.sqrt(_segsum(kb * kb, ones_bd))
        kn = kb / jnp.maximum(nrm, 1e-12)
        kkn_out[0, :, sl] = kn
        b_out[0, :, sl] = kn * a[:, sl]
        bon_out[0, :, sl] = _segsum(rkk[:, sl], ones_bd) * xv[:, sl]
    r_out[0] = xr
    k_out[0] = k2
    v_out[0] = xv


def _rwkv_prep(proj3, small3, prev3, lw, ones_bd, *, tm):
    bsz, t, _ = proj3.shape
    explicit_prev = prev3.shape[1] == t
    tq = tm if explicit_prev else 1
    qmap = (lambda blk: (lambda bi, i: (bi, i, blk))) if explicit_prev else (
        lambda blk: (lambda bi, i: (bi, 0, blk)))
    small_blk = 3 * BRANCH_W // SMALL_W
    big = lambda blk: pl.BlockSpec((1, tm, BRANCH_W), lambda bi, i: (bi, i, blk))
    vec = lambda blk: pl.BlockSpec((1, BRANCH_W), lambda bi, i: (0, blk))
    full = lambda shape: pl.BlockSpec(shape, lambda bi, i: (0, 0))
    out = jax.ShapeDtypeStruct((bsz, t, BRANCH_W), F32)
    scratch = [] if explicit_prev else (
        [pltpu.VMEM((tm + 8, BRANCH_W), F32)] * 3 + [pltpu.VMEM((tm + 8, SMALL_W), F32)])
    return pl.pallas_call(
        functools.partial(_rwkv_prep_kernel, tm=tm, explicit_prev=explicit_prev),
        out_shape=[out] * 8,
        grid=(bsz, t // tm),
        in_specs=[big(0), big(1), big(2),
                  pl.BlockSpec((1, tm, SMALL_W), lambda bi, i: (bi, i, 0)),
                  pl.BlockSpec((1, tq, BRANCH_W), qmap(0)),
                  pl.BlockSpec((1, tq, BRANCH_W), qmap(1)),
                  pl.BlockSpec((1, tq, BRANCH_W), qmap(2)),
                  pl.BlockSpec((1, tq, SMALL_W), qmap(small_blk)),
                  vec(0), vec(1), vec(2),
                  pl.BlockSpec((1, SMALL_W), lambda bi, i: (0, small_blk)),
                  vec(0), vec(0), vec(0), vec(0), vec(0),
                  full((SMALL_W, BRANCH_W)), full((SMALL_W, BRANCH_W)), full((SMALL_W, BRANCH_W)),
                  full((128, 128))],
        out_specs=[pl.BlockSpec((1, tm, BRANCH_W), lambda bi, i: (bi, i, 0))] * 8,
        scratch_shapes=scratch,
        compiler_params=_cparams(2),
        name="rwkv_prep",
    )(proj3, proj3, proj3, small3, prev3, prev3, prev3, prev3,
      lw["mu_p"], lw["mu_p"], lw["mu_p"], lw["mu_p"],
      lw["w0"], lw["a0"], lw["k_k"], lw["k_a"], lw["r_k"],
      lw["w2p"], lw["a2p"], lw["g2p"], ones_bd)


def _rwkv_scan_kernel(r_ref, k_ref, v_ref, kk_ref, b_ref, ld_ref, g_ref, bon_ref,
                      lng_ref, lnb_ref, j_ref, s0_ref, *rest, L, group, n_earlier, single_step):
    earlier = rest[:n_earlier]
    o_ref, st_ref, s_ref = rest[n_earlier:]
    c_id = pl.program_id(1)

    @pl.when(c_id == 0)
    def _():
        z = jnp.zeros((R_HEAD, R_HEAD), F32)
        for p in range(R_PAIRS):
            top = jnp.concatenate([s0_ref[0, 0, 2 * p], z], axis=1)
            bot = jnp.concatenate([z, s0_ref[0, 0, 2 * p + 1]], axis=1)
            s_ref[p] = jnp.concatenate([top, bot], axis=0)

    ld_all = ld_ref[0]
    ti = lax.broadcasted_iota(jnp.int32, (L, L), 0)
    si = lax.broadcasted_iota(jnp.int32, (L, L), 1)
    cs_all = _cumsum_rows(ld_all, single_step)
    ec_all = jnp.exp(cs_all)
    enc_all = jnp.exp(-cs_all)
    ecm_all = jnp.exp(cs_all - ld_all)
    c_last_all = cs_all[L - 1:L, :]
    e_tail_all = jnp.exp(c_last_all - cs_all)
    g_last_all = jnp.exp(c_last_all)

    lane = lax.broadcasted_iota(jnp.int32, (L, 128), 1)
    head_a = lane < R_HEAD

    def stack(x):
        return jnp.concatenate([jnp.where(head_a, x, 0.0), jnp.where(head_a, 0.0, x)], axis=0)

    P2 = 2 * L
    ri = lax.broadcasted_iota(jnp.int32, (P2, P2), 0)
    ci = lax.broadcasted_iota(jnp.int32, (P2, P2), 1)
    strict = ri > ci
    incl = ri >= ci
    eye = jnp.where(ri == ci, 1.0, 0.0)
    ones_bd = j_ref[...]
    inv_n = 1.0 / R_HEAD

    cat0 = lambda a, b: jnp.concatenate([a, b], axis=0)
    cat1 = lambda a, b: jnp.concatenate([a, b], axis=1)

    for g0 in range(0, R_PAIRS, group):
        pairs = list(range(g0, g0 + group))
        sls = [slice(p * 128, (p + 1) * 128) for p in pairs]
        each = lambda f: [f(i) for i in range(group)]

        S = each(lambda i: s_ref[pairs[i]])
        Rs = each(lambda i: stack(r_ref[0, :, sls[i]] * ec_all[:, sls[i]]))
        Bs = each(lambda i: stack(kk_ref[0, :, sls[i]] * ecm_all[:, sls[i]]))
        Ks = each(lambda i: stack(k_ref[0, :, sls[i]] * enc_all[:, sls[i]]))
        As = each(lambda i: stack(-(b_ref[0, :, sls[i]] * enc_all[:, sls[i]])))
        Vs = each(lambda i: stack(v_ref[0, :, sls[i]]))
        Kt = each(lambda i: stack(k_ref[0, :, sls[i]] * e_tail_all[:, sls[i]]))
        At = each(lambda i: stack(-(b_ref[0, :, sls[i]] * e_tail_all[:, sls[i]])))

        if P2 % 128 == 0:
            sc = each(lambda i: _dot_nt(cat0(Bs[i], Rs[i]), cat0(As[i], Ks[i])))
            s_ba = each(lambda i: sc[i][:P2, :P2])
            s_bk = each(lambda i: sc[i][:P2, P2:])
            s_ra = each(lambda i: sc[i][P2:, :P2])
            s_rk = each(lambda i: sc[i][P2:, P2:])
        else:
            s_ba = each(lambda i: _dot_nt(Bs[i], As[i]))
            s_bk = each(lambda i: _dot_nt(Bs[i], Ks[i]))
            s_ra = each(lambda i: _dot_nt(Rs[i], As[i]))
            s_rk = each(lambda i: _dot_nt(Rs[i], Ks[i]))
        Nm = each(lambda i: jnp.where(strict, s_ba[i], 0.0))
        Mbk = each(lambda i: jnp.where(strict, s_bk[i], 0.0))
        Mra = each(lambda i: jnp.where(incl, s_ra[i], 0.0))
        Mrk = each(lambda i: jnp.where(incl, s_rk[i], 0.0))

        Tm = each(lambda i: eye + Nm[i])
        Pw = Nm
        span = 2
        while span < L and not single_step:
            Pw = [_dot(x, x) for x in Pw]
            Tm = each(lambda i: Tm[i] + _dot(Tm[i], Pw[i]))
            span *= 2

        mv = each(lambda i: _dot(cat0(Mbk[i], Mrk[i]), Vs[i]))
        tb = each(lambda i: _dot(Tm[i], cat1(Bs[i], mv[i][:P2])))
        mu = each(lambda i: _dot(Mra[i], tb[i]))
        Ro = each(lambda i: Rs[i] + mu[i][:, :128])
        uo = each(lambda i: _dot_nt(cat0(tb[i][:, :128], Ro[i]), S[i]))
        U = each(lambda i: uo[i][:P2] + tb[i][:, 128:])
        O = each(lambda i: uo[i][P2:] + mv[i][P2:] + mu[i][:, 128:])
        for i, p in enumerate(pairs):
            s_ref[p] = S[i] * g_last_all[:, sls[i]] + _dot_tn(
                cat0(U[i], Vs[i]), cat0(At[i], Kt[i]))

        for i in range(group):
            sl = sls[i]
            out = O[i][:L] + O[i][L:]
            mean = _segsum(out, ones_bd) * inv_n
            d = out - mean
            var = _segsum(d * d, ones_bd) * inv_n
            y = d * lax.rsqrt(var + R_GN_EPS) * lng_ref[:, sl] + lnb_ref[:, sl]
            o_ref[0, :, sl] = ((y + bon_ref[0, :, sl]) * g_ref[0, :, sl]).astype(BF16)

    @pl.when(c_id == pl.num_programs(1) - 1)
    def _():
        for p in range(R_PAIRS):
            sp = s_ref[p]
            st_ref[n_earlier, 0, 2 * p] = sp[:R_HEAD, :R_HEAD]
            st_ref[n_earlier, 0, 2 * p + 1] = sp[R_HEAD:, R_HEAD:]
        for i, e_ref in enumerate(earlier):
            st_ref[i] = e_ref[0]


def _rwkv_scan(seqs, g, bonus, ln_g, ln_b, ones_bd, s0, l, earlier_s, *, L, t_valid):
    bsz, t, _ = seqs[0].shape
    nl = len(earlier_s) + 1
    seq_spec = pl.BlockSpec((1, L, BRANCH_W), lambda bi, c: (bi, c, 0))
    vec_spec = pl.BlockSpec((1, BRANCH_W), lambda bi, c: (0, 0))
    st_ea = pl.BlockSpec((1, 1, R_HEADS, R_HEAD, R_HEAD), lambda bi, c: (0, bi, 0, 0, 0))
    return pl.pallas_call(
        functools.partial(_rwkv_scan_kernel, L=L, group=R_PAIRS, n_earlier=nl - 1,
                          single_step=t_valid == 1),
        out_shape=[jax.ShapeDtypeStruct((bsz, t, BRANCH_W), BF16),
                   jax.ShapeDtypeStruct((nl, bsz, R_HEADS, R_HEAD, R_HEAD), F32)],
        grid=(bsz, t // L),
        in_specs=[seq_spec] * 8 + [
            vec_spec, vec_spec, pl.BlockSpec((128, 128), lambda bi, c: (0, 0)),
            pl.BlockSpec((1, 1, R_HEADS, R_HEAD, R_HEAD), lambda bi, c: (l, bi, 0, 0, 0))]
        + [st_ea] * (nl - 1),
        out_specs=[seq_spec,
                   pl.BlockSpec((nl, 1, R_HEADS, R_HEAD, R_HEAD), lambda bi, c: (0, bi, 0, 0, 0))],
        scratch_shapes=[pltpu.VMEM((R_PAIRS, 128, 128), F32)],
        compiler_params=_cparams(2),
        name="rwkv_scan",
    )(*seqs, g, bonus, ln_g, ln_b, ones_bd, s0, *earlier_s)


def _rwkv_mixer(proj3, small3, prev, s0, l, lw, ones_bd, earlier_s, *, L, tm):
    bsz, t, _ = proj3.shape
    new_shift = jnp.concatenate(
        [proj3[:, -1, :3 * BRANCH_W].astype(F32), small3[:, -1, :R_COLS - 3 * BRANCH_W]], axis=-1)
    prev3 = jnp.pad(prev, ((0, 0), (0, 3 * BRANCH_W + SMALL_W - R_COLS)))[:, None, :]
    if t == 1:
        outs = _rwkv_prep(proj3.reshape(1, bsz, -1), small3.reshape(1, bsz, -1),
                          prev3.reshape(1, bsz, -1), lw, ones_bd, tm=tm)
        outs = [o.reshape(bsz, 1, BRANCH_W) for o in outs]
    else:
        outs = _rwkv_prep(proj3, small3, prev3, lw, ones_bd, tm=tm)
    tp = -(-t // L) * L
    if tp != t:
        outs = [jnp.pad(o, ((0, 0), (0, tp - t), (0, 0))) for o in outs]
    r, k2, v, kkn, b, ld, g, bonus = outs
    o_r, s_new = _rwkv_scan((r, k2, v, kkn, b, ld), g, bonus, lw["ln_g"], lw["ln_b"],
                            ones_bd, s0, l, earlier_s, L=L, t_valid=t)
    return o_r[:, :t].reshape(bsz * t, BRANCH_W), s_new, new_shift


def _mlstm_kernel(q_ref, k_ref, v_ref, o_ref, sm_ref, cq_ref, ck_ref, wq_ref, wk_ref,
                  bq_ref, bk_ref, ib_ref, fb_ref, ng_ref, c0_ref, n0_ref, m0_ref,
                  *rest, L, t_valid, n_earlier):
    earlier = rest[:n_earlier]
    out_ref, c_ref, n_ref, m_ref, qs_ref, ks_ref = rest[n_earlier:]
    last = n_earlier
    c_id = pl.program_id(1)

    @pl.when(c_id == 0)
    def _():
        for i, e_ref in enumerate(earlier):
            c_ref[i] = e_ref[0]
        c_ref[last] = c0_ref[0]
        n_ref[...] = n0_ref[0]
        m_ref[...] = m0_ref[0]
        qs_ref[5:8, :] = cq_ref[0, 0]
        ks_ref[5:8, :] = ck_ref[0, 0]

    qs_ref[8:8 + L, :] = _rows(q_ref, L)
    ks_ref[8:8 + L, :] = _rows(k_ref, L)
    conv_q = bq_ref[...]
    conv_k = bk_ref[...]
    for j in range(CONV_W):
        conv_q = conv_q + qs_ref[5 + j:5 + j + L, :] * wq_ref[j:j + 1, :]
        conv_k = conv_k + ks_ref[5 + j:5 + j + L, :] * wk_ref[j:j + 1, :]
    if L >= CONV_W - 1:
        tail_q = qs_ref[5 + L:8 + L, :]
        tail_k = ks_ref[5 + L:8 + L, :]
        qs_ref[5:8, :] = tail_q
        ks_ref[5:8, :] = tail_k

    row = lax.broadcasted_iota(jnp.int32, (L, 1), 0)
    valid = (c_id * L + row) < t_valid
    q_all = jnp.where(valid, _silu(conv_q), 0.0)
    k_all = jnp.where(valid, _silu(conv_k) * (M_DK ** -0.5), 0.0)
    v_all = jnp.where(valid, _rows(v_ref, L), 0.0)
    gate_o = _sigmoid(_rows(o_ref, L))

    sm = _rows(sm_ref, L)
    lane = lax.broadcasted_iota(jnp.int32, (L, SMALL_W), 1)
    head_lane = lane < M_HEADS
    i_pre = jnp.where(head_lane, pltpu.roll(sm, SMALL_W - S_I, axis=1), 0.0)
    f_pre = jnp.where(head_lane, pltpu.roll(sm, SMALL_W - S_F, axis=1), 0.0)
    ig4 = jnp.where(valid & head_lane, i_pre + ib_ref[...], NEG)
    lf4 = jnp.where(valid & head_lane, _log_sigmoid(f_pre + fb_ref[...]), 0.0)

    ti = lax.broadcasted_iota(jnp.int32, (L, L), 0)
    si = lax.broadcasted_iota(jnp.int32, (L, L), 1)
    causal = si <= ti
    diag = ti == si
    ones_l = jnp.ones((L, L), F32)
    F4 = _cumsum_rows(lf4, t_valid == 1)
    gmf4 = ig4 - F4

    heads = range(M_HEADS)
    each = lambda f: [f(h) for h in heads]
    sls = [slice(h * M_DK, (h + 1) * M_DK) for h in heads]
    rowsum = lambda x: jnp.sum(x, axis=-1, keepdims=True)
    q = each(lambda h: q_all[:, sls[h]])
    k = each(lambda h: k_all[:, sls[h]])
    v = each(lambda h: v_all[:, sls[h]])
    F = each(lambda h: rowsum(jnp.where(lane == h, F4, 0.0)))
    ig = each(lambda h: rowsum(jnp.where(lane == h, ig4, 0.0)))
    gmf = each(lambda h: rowsum(jnp.where(lane == h, gmf4, 0.0)))
    if t_valid == 1:
        g_row = each(lambda h: jnp.where(si == 0, gmf[h][0:1, :], NEG))
    else:
        g_row = each(lambda h: _dot_hi(
            ones_l, jnp.where(diag, jnp.broadcast_to(gmf[h], (L, L)), 0.0)))
    Dm = each(lambda h: jnp.where(causal, F[h] + g_row[h], NEG))

    C = each(lambda h: c_ref[last, 0, h])
    n = each(lambda h: n_ref[0, h])
    m_prev = each(lambda h: m_ref[0, h])
    inter = each(lambda h: F[h] + m_prev[h])
    m_t = each(lambda h: jnp.maximum(inter[h], jnp.max(Dm[h], axis=-1, keepdims=True)))
    w_inter = each(lambda h: jnp.exp(inter[h] - m_t[h]))
    Sm = each(lambda h: _dot_nt(q[h], k[h]) * jnp.exp(Dm[h] - m_t[h]))
    num = each(lambda h: w_inter[h] * _dot(q[h], C[h]) + _dot(Sm[h], v[h]))
    den = each(lambda h: w_inter[h] * rowsum(q[h] * n[h]) + rowsum(Sm[h]))
    hh = each(lambda h: num[h] / jnp.maximum(jnp.abs(den[h]), jnp.exp(-m_t[h])))

    FL = each(lambda h: F[h][L - 1:L, :])
    g_s = each(lambda h: FL[h] - F[h] + ig[h])
    m_new = each(lambda h: jnp.maximum(FL[h] + m_prev[h], jnp.max(g_s[h], axis=0, keepdims=True)))
    a_c = each(lambda h: jnp.exp(FL[h] + m_prev[h] - m_new[h]))
    kw = each(lambda h: k[h] * jnp.exp(g_s[h] - m_new[h]))
    for h in heads:
        c_ref[last, 0, h] = a_c[h] * C[h] + _dot_tn(kw[h], v[h])
        n_ref[0, h] = a_c[h] * n[h] + jnp.sum(kw[h], axis=0, keepdims=True)
        m_ref[0, h] = m_new[h]
        hn = hh[h] * lax.rsqrt(jnp.mean(hh[h] * hh[h], axis=-1, keepdims=True) + EPS)
        out_ref[0, :, sls[h]] = (gate_o[:, sls[h]] * hn * ng_ref[:, sls[h]]).astype(BF16)


def _rows(ref, L):
    x = ref[0].astype(F32)
    if x.shape[0] == L:
        return x
    assert x.shape[0] == 1
    row = lax.broadcasted_iota(jnp.int32, (L, x.shape[1]), 0)
    return jnp.where(row == 0, x, 0.0)


def _mlstm_mixer(proj3, small3, conv_buf, c0, n0, m0, l, lw, earlier_c, *, L):
    bsz, t, _ = proj3.shape
    tb = min(t, L)
    nc = -(-t // L)
    nl = len(earlier_c) + 1
    seq = lambda col: pl.BlockSpec((1, tb, BRANCH_W), lambda bi, c: (bi, c, col // BRANCH_W))
    st_c = pl.BlockSpec((nl, 1, M_HEADS, M_DK, M_DK), lambda bi, c: (0, bi, 0, 0, 0))
    ea_c = pl.BlockSpec((1, 1, M_HEADS, M_DK, M_DK), lambda bi, c: (0, bi, 0, 0, 0))
    st_n = pl.BlockSpec((1, M_HEADS, 1, M_DK), lambda bi, c: (bi, 0, 0, 0))
    st_m = pl.BlockSpec((1, M_HEADS, 1, 1), lambda bi, c: (bi, 0, 0, 0))
    in_c = pl.BlockSpec((1, 1, M_HEADS, M_DK, M_DK), lambda bi, c: (l, bi, 0, 0, 0))
    in_n = pl.BlockSpec((1, 1, M_HEADS, 1, M_DK), lambda bi, c: (l, bi, 0, 0, 0))
    in_m = pl.BlockSpec((1, 1, M_HEADS, 1, 1), lambda bi, c: (l, bi, 0, 0, 0))
    conv = lambda blk: pl.BlockSpec((1, 1, CONV_W - 1, BRANCH_W), lambda bi, c: (l, bi, 0, blk))
    cw = lambda blk: pl.BlockSpec((CONV_W, BRANCH_W), lambda bi, c: (0, blk))
    vec = lambda blk: pl.BlockSpec((1, BRANCH_W), lambda bi, c: (0, blk))
    hb = pl.BlockSpec((1, SMALL_W), lambda bi, c: (0, 0))
    pad_heads = lambda a: jnp.pad(a, ((0, 0), (0, SMALL_W - M_HEADS)))
    out, c_new, n_new, m_new = pl.pallas_call(
        functools.partial(_mlstm_kernel, L=L, t_valid=t, n_earlier=nl - 1),
        out_shape=[jax.ShapeDtypeStruct((bsz, nc * L, BRANCH_W), BF16),
                   jax.ShapeDtypeStruct((nl, bsz, M_HEADS, M_DK, M_DK), F32),
                   jax.ShapeDtypeStruct((bsz, M_HEADS, 1, M_DK), F32),
                   jax.ShapeDtypeStruct((bsz, M_HEADS, 1, 1), F32)],
        grid=(bsz, nc),
        in_specs=[seq(C_MQK), seq(C_MQK + BRANCH_W), seq(C_MV), seq(C_MO),
                  pl.BlockSpec((1, tb, SMALL_W), lambda bi, c: (bi, c, 0)),
                  conv(0), conv(1), cw(0), cw(1), vec(0), vec(1), hb, hb, vec(0),
                  in_c, in_n, in_m] + [ea_c] * (nl - 1),
        out_specs=[pl.BlockSpec((1, L, BRANCH_W), lambda bi, c: (bi, c, 0)), st_c, st_n, st_m],
        scratch_shapes=[pltpu.VMEM((L + 8, BRANCH_W), F32), pltpu.VMEM((L + 8, BRANCH_W), F32)],
        compiler_params=_cparams(2),
        name="mlstm_scan",
    )(proj3, proj3, proj3, proj3, small3, conv_buf, conv_buf,
      lw["conv_w"], lw["conv_w"], lw["conv_b"], lw["conv_b"],
      pad_heads(lw["i_b"]), pad_heads(lw["f_b"]),
      lw["m_norm_g"], c0, n0.reshape(DEPTH, bsz, M_HEADS, 1, M_DK),
      m0.reshape(DEPTH, bsz, M_HEADS, 1, 1), *earlier_c)
    return out, c_new, n_new.reshape(bsz, M_HEADS, M_DK), m_new.reshape(bsz, M_HEADS)


def _gla_kernel(q_ref, k_ref, v_ref, og_ref, sm_ref, a2_ref, ab_ref, ng_ref, s0_ref,
                *rest, L, t_valid, n_earlier):
    earlier = rest[:n_earlier]
    out_ref, s_ref, b_scr, q_scr = rest[n_earlier:]
    last = n_earlier
    c_id = pl.program_id(1)

    @pl.when(c_id == 0)
    def _():
        for i, e_ref in enumerate(earlier):
            s_ref[i] = e_ref[0]
        s_ref[last] = s0_ref[0]

    row = lax.broadcasted_iota(jnp.int32, (L, 1), 0)
    valid = (c_id * L + row) < t_valid
    q_all = jnp.where(valid, _rows(q_ref, L) * (G_DK ** -0.5), 0.0)
    k_all = jnp.where(valid, _rows(k_ref, L), 0.0)
    v_all = jnp.where(valid, _rows(v_ref, L), 0.0)
    gate_o = _silu(_rows(og_ref, L))
    lg = _log_sigmoid(_dot(_rows(sm_ref, L), a2_ref[...]) + ab_ref[...]) * (1.0 / G_GATE_NORM)
    lg = jnp.where(valid, lg, 0.0)

    ti = lax.broadcasted_iota(jnp.int32, (L, L), 0)
    si = lax.broadcasted_iota(jnp.int32, (L, L), 1)
    b_all = _cumsum_rows(lg, t_valid == 1)
    b_scr[...] = b_all
    q_scr[...] = q_all
    eb_all = jnp.exp(b_all)
    b_last_all = b_all[L - 1:L, :]
    e_tail_all = jnp.exp(b_last_all - b_all)
    ones_lv = jnp.ones((L, G_DV), F32)

    n_t = L if t_valid >= L else t_valid
    s_col = {rows: lax.broadcasted_iota(jnp.int32, (rows, 1), 0) for rows in range(8, L + 1, 8)}
    t_lane = {rows: lax.broadcasted_iota(jnp.int32, (rows, L), 1) for rows in range(8, L + 1, 8)}

    for h in range(G_HEADS):
        sl = slice(h * G_DK, (h + 1) * G_DK)
        sv = slice(h * G_DV, (h + 1) * G_DV)
        q, k, v, b = q_all[:, sl], k_all[:, sl], v_all[:, sv], b_all[:, sl]

        at = jnp.zeros((L, L), F32)
        for t in range(n_t):
            rows = 8 * (t // 8 + 1)
            bt = b_scr[t:t + 1, sl]
            qt = q_scr[t:t + 1, sl]
            e = jnp.exp(jnp.where(s_col[rows] <= t, bt - b[:rows], NEG))
            col = jnp.sum(qt * k[:rows] * e, axis=-1, keepdims=True)
            top = jnp.where(t_lane[rows] == t, col, at[:rows])
            at = top if rows == L else jnp.concatenate([top, at[rows:]], axis=0)

        S = s_ref[last, 0, h]
        o = _dot(q * eb_all[:, sl], S) + _dot_tn(at, v)
        decay = jnp.exp(lax.dot_general(lg[:, sl], ones_lv, (((0,), (0,)), ((), ())),
                                        precision=lax.Precision.HIGHEST,
                                        preferred_element_type=F32))
        s_ref[last, 0, h] = S * decay + _dot_tn(k * e_tail_all[:, sl], v)

        on = o * lax.rsqrt(jnp.mean(o * o, axis=-1, keepdims=True) + EPS) * ng_ref[:, sv]
        out_ref[0, :, sv] = (on * gate_o[:, sv]).astype(BF16)


def _gla_mixer(proj3, small3, s0, l, lw, earlier_s, *, L):
    bsz, t, _ = proj3.shape
    tb = min(t, L)
    nc = -(-t // L)
    nl = len(earlier_s) + 1
    gw = G_HEADS * G_DK
    st = pl.BlockSpec((nl, 1, G_HEADS, G_DK, G_DV), lambda bi, c: (0, bi, 0, 0, 0))
    st_ea = pl.BlockSpec((1, 1, G_HEADS, G_DK, G_DV), lambda bi, c: (0, bi, 0, 0, 0))
    st_in = pl.BlockSpec((1, 1, G_HEADS, G_DK, G_DV), lambda bi, c: (l, bi, 0, 0, 0))
    return pl.pallas_call(
        functools.partial(_gla_kernel, L=L, t_valid=t, n_earlier=nl - 1),
        out_shape=[jax.ShapeDtypeStruct((bsz, nc * L, BRANCH_W), BF16),
                   jax.ShapeDtypeStruct((nl, bsz, G_HEADS, G_DK, G_DV), F32)],
        grid=(bsz, nc),
        in_specs=[pl.BlockSpec((1, tb, gw), lambda bi, c: (bi, c, C_GQ // gw)),
                  pl.BlockSpec((1, tb, gw), lambda bi, c: (bi, c, C_GQ // gw + 1)),
                  pl.BlockSpec((1, tb, BRANCH_W), lambda bi, c: (bi, c, C_GV // BRANCH_W)),
                  pl.BlockSpec((1, tb, BRANCH_W), lambda bi, c: (bi, c, C_GOG // BRANCH_W)),
                  pl.BlockSpec((1, tb, SMALL_W), lambda bi, c: (bi, c, 0)),
                  pl.BlockSpec((SMALL_W, gw), lambda bi, c: (0, 0)),
                  pl.BlockSpec((1, gw), lambda bi, c: (0, 0)),
                  pl.BlockSpec((1, BRANCH_W), lambda bi, c: (0, 0)),
                  st_in] + [st_ea] * (nl - 1),
        out_specs=[pl.BlockSpec((1, L, BRANCH_W), lambda bi, c: (bi, c, 0)), st],
        scratch_shapes=[pltpu.VMEM((L, gw), F32), pltpu.VMEM((L, gw), F32)],
        compiler_params=_cparams(2),
        name="gla_scan",
    )(proj3, proj3, proj3, proj3, small3, lw["g_a2p"], lw["g_a_b"], lw["g_norm_g"], s0,
      *earlier_s)


def _small_group_rows(wt):
    depth, _, k = wt.shape
    used = (R_COLS - 3072) + 2 * M_HEADS + G_LR
    return jnp.concatenate(
        [wt[:, W_R0 + 3072:W_R0 + 3264], wt[:, W_M0 + 3072:W_M0 + 3080],
         wt[:, W_G0 + 2048:W_G0 + 2064], jnp.zeros((depth, MAIN_TN - used, k), wt.dtype)], axis=1)


def _rows_padded(w, row0, total):
    return jnp.pad(w, ((row0, total - row0 - w.shape[0]), (0, 0)))


def _layer_weights(l, P):
    mu = P["rwkv_mu"][l]
    mu_p = jnp.concatenate([mu, jnp.zeros((3 * BRANCH_W + SMALL_W - R_COLS,), F32)]).reshape(1, -1)
    row = lambda a: a.reshape(1, -1)
    return {
        "norm1_g": P["norm1_g"][l], "gate_b": P["gate_b"][l],
        "mu_p": mu_p, "w0": row(P["rwkv_w0"][l]), "a0": row(P["rwkv_a0"][l]),
        "k_k": row(P["rwkv_k_k"][l]), "k_a": row(P["rwkv_k_a"][l]), "r_k": row(P["rwkv_r_k"][l]),
        "w2p": _rows_padded(P["rwkv_w2"][l], 0, SMALL_W),
        "a2p": _rows_padded(P["rwkv_a2"][l], R_LORA, SMALL_W),
        "g2p": _rows_padded(P["rwkv_g2"][l], 2 * R_LORA, SMALL_W),
        "ln_g": row(P["rwkv_ln_g"][l]), "ln_b": row(P["rwkv_ln_b"][l]),
        "conv_w": P["mlstm_conv_w"][l], "conv_b": row(P["mlstm_conv_b"][l]),
        "i_b": row(P["mlstm_i_b"][l]), "f_b": row(P["mlstm_f_b"][l]),
        "m_norm_g": row(P["mlstm_norm_g"][l]),
        "g_a2p": _rows_padded(P["gla_a2"][l], S_GXA, SMALL_W), "g_a_b": row(P["gla_a_b"][l]),
        "g_norm_g": row(P["gla_norm_g"][l]),
        "norm2_g": P["norm2_g"][l],
    }


def _layer(x2, bsz, t, states, l, lw, big, ones_bd, cfg, earlier):
    ea_wkv, ea_c, ea_s = earlier
    rw_prev, rw_s, m_conv, m_c, m_n, m_m, g_s = states
    m = bsz * t
    L, tm = cfg["L"], cfg["tm"]
    proj, small = _rms_matmul(x2, lw["norm1_g"], big["w_in_t"], big["w_in_t_small"], l, tm=tm,
                              main_dtype=cfg["proj_dtype"])
    proj3 = proj.reshape(bsz, t, C_SMALL)
    small3 = small.reshape(bsz, t, -1)

    o_r, rw_s_new, rw_prev_new = _rwkv_mixer(proj3, small3, rw_prev[l], rw_s, l, lw, ones_bd,
                                             ea_wkv, L=L, tm=cfg["tm_prep"])

    o_m, m_c_new, m_n_new, m_m_new = _mlstm_mixer(proj3, small3, m_conv, m_c, m_n, m_m, l, lw,
                                                  ea_c, L=L)
    qk_tail = proj3[:, -min(t, CONV_W - 1):, C_MQK:C_MQK + 2 * BRANCH_W].astype(F32)
    m_conv_new = jnp.concatenate([m_conv[l], qk_tail], axis=1)[:, -(CONV_W - 1):]
    o_g, g_s_new = _gla_mixer(proj3, small3, g_s, l, lw, ea_s, L=L)
    o_m = o_m[:, :t].reshape(m, BRANCH_W)
    o_g = o_g[:, :t].reshape(m, BRANCH_W)

    merged = _merge(o_r, o_m, o_g, big["w_branch"], l, proj, lw["gate_b"],
                    tm=cfg["tm_merge"], tn=512)
    x2 = _matmul_residual(merged, big["w_out"], l, x2, tm=tm, tn=512, tk=D_MODEL)
    hidden = _rms_swiglu(x2, lw["norm2_g"], big["w_gu"], l, tm=tm, tn=512)
    x2 = _matmul_residual(hidden, big["w_down"], l, x2, tm=tm, tn=512, tk=2816)
    return x2, (rw_prev_new, rw_s_new, m_conv_new, m_c_new, m_n_new, m_m_new, g_s_new)


def _trunk(x, states, layer_ws, big, final_g, ones_bd, cfg):
    bsz, t, d = x.shape
    x2 = x.reshape(bsz * t, d)
    per_layer = []
    for l in range(DEPTH):
        is_last = l == DEPTH - 1
        earlier = tuple([st[i] for st in per_layer] if is_last else [] for i in BIG_STATES)
        x2, new = _layer(x2, bsz, t, states, l, layer_ws[l], big, ones_bd, cfg, earlier)
        per_layer.append(new)
    new_states = [per_layer[-1][i] if i in BIG_STATES
                  else jnp.stack([st[i] for st in per_layer], axis=0) for i in range(len(states))]
    y = _rmsnorm(x2, final_g, tm=cfg["tm_norm"]).reshape(bsz, t, d)
    return y, new_states


BIG_STATES = (1, 3, 6)

PROMPT_CFG = dict(L=64, tm=1024, tm_prep=256, tm_merge=512, tm_norm=512, proj_dtype=BF16)
SAMPLE_CFG = dict(L=16, tm=128, tm_prep=128, tm_merge=128, tm_norm=128, proj_dtype=F32)


def _zero_states(bsz):
    return (jnp.zeros((DEPTH, bsz, R_COLS), F32),
            jnp.zeros((DEPTH, bsz, R_HEADS, R_HEAD, R_HEAD), F32),
            jnp.zeros((DEPTH, bsz, CONV_W - 1, 2 * BRANCH_W), F32),
            jnp.zeros((DEPTH, bsz, M_HEADS, M_DK, M_DK), F32),
            jnp.zeros((DEPTH, bsz, M_HEADS, M_DK), F32),
            jnp.zeros((DEPTH, bsz, M_HEADS), F32),
            jnp.zeros((DEPTH, bsz, G_HEADS, G_DK, G_DV), F32))


def kernel(x_prompt, x_sample, state_rwkv_shift, state_rwkv_wkv, state_mlstm_conv, state_mlstm_C, state_mlstm_n, state_mlstm_m, state_gla_S, norm1_g, w_in, gate_b, rwkv_mu, rwkv_w0, rwkv_w2, rwkv_a0, rwkv_a2, rwkv_g2, rwkv_k_k, rwkv_k_a, rwkv_r_k, rwkv_ln_g, rwkv_ln_b, mlstm_conv_w, mlstm_conv_b, mlstm_i_b, mlstm_f_b, mlstm_norm_g, gla_a2, gla_a_b, gla_norm_g, w_branch, w_out, norm2_g, ffn_w_gu, ffn_w_down, final_norm_g):
    P = dict(norm1_g=norm1_g, w_in=w_in, gate_b=gate_b, rwkv_mu=rwkv_mu, rwkv_w0=rwkv_w0,
             rwkv_w2=rwkv_w2, rwkv_a0=rwkv_a0, rwkv_a2=rwkv_a2, rwkv_g2=rwkv_g2,
             rwkv_k_k=rwkv_k_k, rwkv_k_a=rwkv_k_a, rwkv_r_k=rwkv_r_k, rwkv_ln_g=rwkv_ln_g,
             rwkv_ln_b=rwkv_ln_b, mlstm_conv_w=mlstm_conv_w, mlstm_conv_b=mlstm_conv_b,
             mlstm_i_b=mlstm_i_b, mlstm_f_b=mlstm_f_b, mlstm_norm_g=mlstm_norm_g,
             gla_a2=gla_a2, gla_a_b=gla_a_b, gla_norm_g=gla_norm_g, w_branch=w_branch,
             w_out=w_out, norm2_g=norm2_g, ffn_w_gu=ffn_w_gu, ffn_w_down=ffn_w_down)
    layer_ws = [_layer_weights(l, P) for l in range(DEPTH)]
    w_in_t = jnp.swapaxes(w_in, 1, 2)
    big = dict(w_in_t=w_in_t, w_in_t_small=_small_group_rows(w_in_t), w_branch=w_branch,
               w_out=w_out, w_gu=ffn_w_gu, w_down=ffn_w_down)
    head_of_lane = jnp.arange(128) // R_HEAD
    ones_bd = (head_of_lane[:, None] == head_of_lane[None, :]).astype(BF16)

    y_p, p_states = _trunk(x_prompt, _zero_states(x_prompt.shape[0]), layer_ws, big,
                           final_norm_g, ones_bd, PROMPT_CFG)
    s_states = (state_rwkv_shift, state_rwkv_wkv, state_mlstm_conv, state_mlstm_C,
                state_mlstm_n, state_mlstm_m, state_gla_S)
    y_s, s_states = _trunk(x_sample, s_states, layer_ws, big, final_norm_g, ones_bd, SAMPLE_CFG)
    return (y_p, y_s, *p_states, *s_states)
```

```python
import functools

import jax
import jax.numpy as jnp
from jax import lax
from jax.experimental import pallas as pl
from jax.experimental.pallas import tpu as pltpu

F32 = jnp.float32
BF16 = jnp.bfloat16

D_MODEL = 2048
DEPTH = 2
BRANCH_W = 1024
R_HEADS, R_HEAD = 16, 64
R_PAIRS = R_HEADS // 2
R_LORA = 64
R_COLS = 3 * BRANCH_W + 3 * R_LORA
R_GN_EPS = 64e-5
M_HEADS, M_DK = 4, 256
CONV_W = 4
G_HEADS, G_DK, G_DV = 4, 128, 256
G_LR = 16
G_GATE_NORM = 16.0
D_FF = 5632
EPS = 1e-6
NEG = -1e30
GLA_SUB = 16

C_RWKV = 0
C_MQK = 3072
C_MV = 5120
C_MO = 6144
C_GQ = 7168
C_GV = 8192
C_GOG = 9216
C_GATE = 10240
C_SMALL = 16384
SMALL_W = 256
S_I, S_F, S_GXA = 192, 196, 200
MAIN_TN = 512
W_R0, W_M0, W_G0, W_T0 = 0, R_COLS, R_COLS + 4104, R_COLS + 4104 + 3088
MAIN_RUNS = ((C_RWKV, W_R0, 3072), (C_MQK, W_M0, 3072), (C_MO, W_M0 + 3080, 1024),
             (C_GQ, W_G0, 2048), (C_GOG, W_G0 + 2064, 1024 + 3 * D_MODEL))

VMEM_LIMIT = 56 * 1024 * 1024


def _cparams(n_axes):
    return pltpu.CompilerParams(dimension_semantics=("arbitrary",) * n_axes,
                                vmem_limit_bytes=VMEM_LIMIT)


def _dot(a, b):
    return jnp.dot(a.astype(BF16), b.astype(BF16), preferred_element_type=F32)


def _dot_nt(a, b):
    return lax.dot_general(a.astype(BF16), b.astype(BF16), (((1,), (1,)), ((), ())),
                           preferred_element_type=F32)


def _dot_tn(a, b):
    return lax.dot_general(a.astype(BF16), b.astype(BF16), (((0,), (0,)), ((), ())),
                           preferred_element_type=F32)


def _dot_hi(a, b):
    return jnp.dot(a, b, precision=lax.Precision.HIGHEST, preferred_element_type=F32)


def _cumsum_rows(x, single_step):
    n = x.shape[0]
    if single_step:
        row = lax.broadcasted_iota(jnp.int32, x.shape, 0)
        return jnp.where(row == 0, x, x[0:1, :])
    ti = lax.broadcasted_iota(jnp.int32, (n, n), 0)
    si = lax.broadcasted_iota(jnp.int32, (n, n), 1)
    return _dot_hi((si <= ti).astype(F32), x)


def _segsum(y, ones_blockdiag):
    hi = y.astype(BF16)
    lo = (y - hi.astype(F32)).astype(BF16)
    return (jnp.dot(hi, ones_blockdiag, preferred_element_type=F32)
            + jnp.dot(lo, ones_blockdiag, preferred_element_type=F32))


def _sigmoid(x):
    return 1.0 / (1.0 + jnp.exp(-x))


def _silu(x):
    return x * _sigmoid(x)


def _log_sigmoid(x):
    return -_softplus(-x)


def _softplus(x):
    return jnp.maximum(x, 0.0) + jnp.log(1.0 + jnp.exp(-jnp.abs(x)))


def _rms_mm_kernel(start_ref, x_ref, g_ref, wt_ref, wsm_ref, o_ref, osm_ref, h_scr, *, n_main):
    del start_ref
    j = pl.program_id(1)

    @pl.when(j == 0)
    def _():
        x = x_ref[...]
        ms = jnp.mean(x * x, axis=-1, keepdims=True)
        h_scr[...] = (x * lax.rsqrt(ms + EPS) * g_ref[...]).astype(BF16)

    @pl.when(j < n_main)
    def _():
        o_ref[...] = _dot_nt(h_scr[...], wt_ref[0]).astype(o_ref.dtype)

    @pl.when(j >= n_main)
    def _():
        osm_ref[...] = _dot_nt(h_scr[...], wsm_ref[0])


def _rms_matmul(x, g, wt, wt_small, l, *, tm, main_dtype):
    m, k = x.shape
    n_in = wt.shape[1]
    tn = MAIN_TN
    starts = []
    for p0, s0, width in MAIN_RUNS:
        assert p0 == len(starts) * tn and width % tn == 0 and s0 % 8 == 0
        starts += list(range(s0, s0 + width, tn))
    n_main = len(starts)
    assert n_main * tn == C_SMALL and starts[-1] + tn == n_in
    grid_spec = pltpu.PrefetchScalarGridSpec(
        num_scalar_prefetch=1,
        grid=(m // tm, n_main + 1),
        in_specs=[pl.BlockSpec((tm, k), lambda i, j, st: (i, 0)),
                  pl.BlockSpec((1, k), lambda i, j, st: (0, 0)),
                  pl.BlockSpec((pl.Element(1), pl.Element(tn), pl.Element(k)),
                               lambda i, j, st: (l, st[j] * 8, 0)),
                  pl.BlockSpec((1, tn, k), lambda i, j, st: (l, 0, 0))],
        out_specs=[pl.BlockSpec((tm, tn), lambda i, j, st: (i, jnp.minimum(j, n_main - 1))),
                   pl.BlockSpec((tm, tn), lambda i, j, st: (i, 0))],
        scratch_shapes=[pltpu.VMEM((tm, k), BF16)],
    )
    return pl.pallas_call(
        functools.partial(_rms_mm_kernel, n_main=n_main),
        out_shape=[jax.ShapeDtypeStruct((m, C_SMALL), main_dtype),
                   jax.ShapeDtypeStruct((m, tn), F32)],
        grid_spec=grid_spec,
        compiler_params=_cparams(2),
        name="rms_in_proj",
    )(jnp.asarray([s // 8 for s in starts] + [0], jnp.int32), x, g.reshape(1, k), wt, wt_small)


def _rms_swiglu_kernel(x_ref, g_ref, wg_ref, wu_ref, o_ref, h_scr):
    @pl.when(pl.program_id(1) == 0)
    def _():
        x = x_ref[...]
        ms = jnp.mean(x * x, axis=-1, keepdims=True)
        h_scr[...] = (x * lax.rsqrt(ms + EPS) * g_ref[...]).astype(BF16)

    h = h_scr[...]
    gg = jnp.dot(h, wg_ref[0].astype(BF16), preferred_element_type=F32)
    uu = jnp.dot(h, wu_ref[0].astype(BF16), preferred_element_type=F32)
    o_ref[...] = (_silu(gg) * uu).astype(BF16)


def _rms_swiglu(x, g, w_gu, l, *, tm, tn):
    m, k = x.shape
    nj = D_FF // tn
    return pl.pallas_call(
        _rms_swiglu_kernel,
        out_shape=jax.ShapeDtypeStruct((m, D_FF), BF16),
        grid=(m // tm, nj),
        in_specs=[pl.BlockSpec((tm, k), lambda i, j: (i, 0)),
                  pl.BlockSpec((1, k), lambda i, j: (0, 0)),
                  pl.BlockSpec((1, k, tn), lambda i, j: (l, 0, j)),
                  pl.BlockSpec((1, k, tn), lambda i, j: (l, 0, j + nj))],
        out_specs=pl.BlockSpec((tm, tn), lambda i, j: (i, j)),
        scratch_shapes=[pltpu.VMEM((tm, k), BF16)],
        compiler_params=_cparams(2),
        name="rms_ffn_swiglu",
    )(x, g.reshape(1, k), w_gu, w_gu)


def _mm_res_kernel(a_ref, w_ref, res_ref, o_ref, acc_ref, *, nk):
    kk = pl.program_id(2)

    @pl.when(kk == 0)
    def _():
        acc_ref[...] = jnp.zeros_like(acc_ref)

    acc_ref[...] += jnp.dot(a_ref[...], w_ref[0].astype(BF16), preferred_element_type=F32)

    @pl.when(kk == nk - 1)
    def _():
        o_ref[...] = acc_ref[...] + res_ref[...]


def _matmul_residual(a, w, l, res, *, tm, tn, tk):
    m, k = a.shape
    n = w.shape[2]
    nk = k // tk
    return pl.pallas_call(
        functools.partial(_mm_res_kernel, nk=nk),
        out_shape=jax.ShapeDtypeStruct((m, n), F32),
        grid=(m // tm, n // tn, nk),
        in_specs=[pl.BlockSpec((tm, tk), lambda i, j, kk: (i, kk)),
                  pl.BlockSpec((1, tk, tn), lambda i, j, kk: (l, kk, j)),
                  pl.BlockSpec((tm, tn), lambda i, j, kk: (i, j))],
        out_specs=pl.BlockSpec((tm, tn), lambda i, j, kk: (i, j)),
        scratch_shapes=[pltpu.VMEM((tm, tn), F32)],
        compiler_params=_cparams(3),
        name="matmul_residual",
    )(a, w, res)


def _merge_kernel(or_ref, om_ref, og_ref, wr_ref, wm_ref, wg_ref,
                  gr_ref, gm_ref, gg_ref, br_ref, bm_ref, bg_ref, o_ref):
    acc = _sigmoid(gr_ref[...].astype(F32) + br_ref[0]) * jnp.dot(
        or_ref[...], wr_ref[0, 0].astype(BF16), preferred_element_type=F32)
    acc += _sigmoid(gm_ref[...].astype(F32) + bm_ref[0]) * jnp.dot(
        om_ref[...], wm_ref[0, 0].astype(BF16), preferred_element_type=F32)
    acc += _sigmoid(gg_ref[...].astype(F32) + bg_ref[0]) * jnp.dot(
        og_ref[...], wg_ref[0, 0].astype(BF16), preferred_element_type=F32)
    o_ref[...] = acc.astype(BF16)


def _merge(o_r, o_m, o_g, w_branch, l, proj, gate_b, *, tm, tn):
    m = o_r.shape[0]
    gate_blk = C_GATE // tn
    per = D_MODEL // tn
    o_spec = pl.BlockSpec((tm, BRANCH_W), lambda i, j: (i, 0))

    def w_spec(b):
        return pl.BlockSpec((1, 1, BRANCH_W, tn), lambda i, j: (l, b, 0, j))

    def g_spec(b):
        return pl.BlockSpec((tm, tn), lambda i, j: (i, gate_blk + b * per + j))

    def b_spec(b):
        return pl.BlockSpec((1, 1, tn), lambda i, j: (b, 0, j))

    gate_b = gate_b.reshape(3, 1, D_MODEL)

    return pl.pallas_call(
        _merge_kernel,
        out_shape=jax.ShapeDtypeStruct((m, D_MODEL), BF16),
        grid=(m // tm, per),
        in_specs=[o_spec, o_spec, o_spec, w_spec(0), w_spec(1), w_spec(2),
                  g_spec(0), g_spec(1), g_spec(2), b_spec(0), b_spec(1), b_spec(2)],
        out_specs=pl.BlockSpec((tm, tn), lambda i, j: (i, j)),
        compiler_params=_cparams(2),
        name="gated_merge",
    )(o_r, o_m, o_g, w_branch, w_branch, w_branch, proj, proj, proj, gate_b, gate_b, gate_b)


def _rmsnorm_kernel(x_ref, g_ref, o_ref):
    x = x_ref[...]
    ms = jnp.mean(x * x, axis=-1, keepdims=True)
    o_ref[...] = x * lax.rsqrt(ms + EPS) * g_ref[...]


def _rmsnorm(x, g, *, tm):
    m, k = x.shape
    return pl.pallas_call(
        _rmsnorm_kernel,
        out_shape=jax.ShapeDtypeStruct((m, k), F32),
        grid=(m // tm,),
        in_specs=[pl.BlockSpec((tm, k), lambda i: (i, 0)),
                  pl.BlockSpec((1, k), lambda i: (0, 0))],
        out_specs=pl.BlockSpec((tm, k), lambda i: (i, 0)),
        compiler_params=_cparams(1),
        name="final_rmsnorm",
    )(x, g.reshape(1, k))


def _rwkv_prep_kernel(pr_ref, pk_ref, pv_ref, ps_ref, qr_ref, qk_ref, qv_ref, qs_ref,
                      mur_ref, muk_ref, muv_ref, mus_ref,
                      w0_ref, a0_ref, kk_ref, ka_ref, rk_ref,
                      w2_ref, a2_ref, g2_ref, j_ref,
                      r_out, k_out, v_out, kkn_out, b_out, ld_out, g_out, bon_out,
                      *scratch, tm, explicit_prev):
    def shift(p_ref, q_ref, mu_ref, scr):
        p = p_ref[0].astype(F32)
        if explicit_prev:
            prev = q_ref[0]
        else:
            @pl.when(pl.program_id(1) == 0)
            def _():
                scr[7:8, :] = q_ref[0]

            scr[8:8 + tm, :] = p
            prev = scr[7:7 + tm, :]
            scr[7:8, :] = p[tm - 1:tm, :]
        return p + (prev - p) * mu_ref[...]

    scr = scratch if scratch else (None,) * 4
    xr = shift(pr_ref, qr_ref, mur_ref, scr[0])
    xk = shift(pk_ref, qk_ref, muk_ref, scr[1])
    xv = shift(pv_ref, qv_ref, muv_ref, scr[2])
    xs = shift(ps_ref, qs_ref, mus_ref, scr[3])

    w = -_softplus(-(w0_ref[...] + _dot(jnp.tanh(xs), w2_ref[...]))) - 0.5
    ld_out[0] = -jnp.exp(w)
    a = _sigmoid(a0_ref[...] + _dot(xs, a2_ref[...]))
    g_out[0] = _dot(_sigmoid(xs), g2_ref[...])

    ones_bd = j_ref[...]
    kkr = xk * kk_ref[...]
    k2 = xk * (1.0 + (a - 1.0) * ka_ref[...])
    rkk = xr * k2 * rk_ref[...]
    for p in range(R_PAIRS):
        sl = slice(p * 128, (p + 1) * 128)
        kb = kkr[:, sl]
        nrm = jnp.sqrt(_segsum(kb * kb, ones_bd))
        kn = kb / jnp.maximum(nrm, 1e-12)
        kkn_out[0, :, sl] = kn
        b_out[0, :, sl] = kn * a[:, sl]
        bon_out[0, :, sl] = _segsum(rkk[:, sl], ones_bd) * xv[:, sl]
    r_out[0] = xr
    k_out[0] = k2
    v_out[0] = xv


def _rwkv_prep(proj3, small3, prev3, lw, ones_bd, *, tm):
    bsz, t, _ = proj3.shape
    explicit_prev = prev3.shape[1] == t
    tq = tm if explicit_prev else 1
    qmap = (lambda blk: (lambda bi, i: (bi, i, blk))) if explicit_prev else (
        lambda blk: (lambda bi, i: (bi, 0, blk)))
    small_blk = 3 * BRANCH_W // SMALL_W
    big = lambda blk: pl.BlockSpec((1, tm, BRANCH_W), lambda bi, i: (bi, i, blk))
    vec = lambda blk: pl.BlockSpec((1, BRANCH_W), lambda bi, i: (0, blk))
    full = lambda shape: pl.BlockSpec(shape, lambda bi, i: (0, 0))
    out = jax.ShapeDtypeStruct((bsz, t, BRANCH_W), F32)
    scratch = [] if explicit_prev else (
        [pltpu.VMEM((tm + 8, BRANCH_W), F32)] * 3 + [pltpu.VMEM((tm + 8, SMALL_W), F32)])
    return pl.pallas_call(
        functools.partial(_rwkv_prep_kernel, tm=tm, explicit_prev=explicit_prev),
        out_shape=[out] * 8,
        grid=(bsz, t // tm),
        in_specs=[big(0), big(1), big(2),
                  pl.BlockSpec((1, tm, SMALL_W), lambda bi, i: (bi, i, 0)),
                  pl.BlockSpec((1, tq, BRANCH_W), qmap(0)),
                  pl.BlockSpec((1, tq, BRANCH_W), qmap(1)),
                  pl.BlockSpec((1, tq, BRANCH_W), qmap(2)),
                  pl.BlockSpec((1, tq, SMALL_W), qmap(small_blk)),
                  vec(0), vec(1), vec(2),
                  pl.BlockSpec((1, SMALL_W), lambda bi, i: (0, small_blk)),
                  vec(0), vec(0), vec(0), vec(0), vec(0),
                  full((SMALL_W, BRANCH_W)), full((SMALL_W, BRANCH_W)), full((SMALL_W, BRANCH_W)),
                  full((128, 128))],
        out_specs=[pl.BlockSpec((1, tm, BRANCH_W), lambda bi, i: (bi, i, 0))] * 8,
        scratch_shapes=scratch,
        compiler_params=_cparams(2),
        name="rwkv_prep",
    )(proj3, proj3, proj3, small3, prev3, prev3, prev3, prev3,
      lw["mu_p"], lw["mu_p"], lw["mu_p"], lw["mu_p"],
      lw["w0"], lw["a0"], lw["k_k"], lw["k_a"], lw["r_k"],
      lw["w2p"], lw["a2p"], lw["g2p"], ones_bd)


def _rwkv_scan_kernel(r_ref, k_ref, v_ref, kk_ref, b_ref, ld_ref, g_ref, bon_ref,
                      lng_ref, lnb_ref, j_ref, s0_ref, *rest, L, group, n_earlier, single_step):
    earlier = rest[:n_earlier]
    o_ref, st_ref, s_ref = rest[n_earlier:]
    c_id = pl.program_id(1)

    @pl.when(c_id == 0)
    def _():
        z = jnp.zeros((R_HEAD, R_HEAD), F32)
        for p in range(R_PAIRS):
            top = jnp.concatenate([s0_ref[0, 0, 2 * p], z], axis=1)
            bot = jnp.concatenate([z, s0_ref[0, 0, 2 * p + 1]], axis=1)
            s_ref[p] = jnp.concatenate([top, bot], axis=0)

    ld_all = ld_ref[0]
    ti = lax.broadcasted_iota(jnp.int32, (L, L), 0)
    si = lax.broadcasted_iota(jnp.int32, (L, L), 1)
    cs_all = _cumsum_rows(ld_all, single_step)
    ec_all = jnp.exp(cs_all)
    enc_all = jnp.exp(-cs_all)
    ecm_all = jnp.exp(cs_all - ld_all)
    c_last_all = cs_all[L - 1:L, :]
    e_tail_all = jnp.exp(c_last_all - cs_all)
    g_last_all = jnp.exp(c_last_all)

    lane = lax.broadcasted_iota(jnp.int32, (L, 128), 1)
    head_a = lane < R_HEAD

    def stack(x):
        return jnp.concatenate([jnp.where(head_a, x, 0.0), jnp.where(head_a, 0.0, x)], axis=0)

    P2 = 2 * L
    ri = lax.broadcasted_iota(jnp.int32, (P2, P2), 0)
    ci = lax.broadcasted_iota(jnp.int32, (P2, P2), 1)
    strict = ri > ci
    incl = ri >= ci
    eye = jnp.where(ri == ci, 1.0, 0.0)
    ones_bd = j_ref[...]
    inv_n = 1.0 / R_HEAD

    cat0 = lambda a, b: jnp.concatenate([a, b], axis=0)
    cat1 = lambda a, b: jnp.concatenate([a, b], axis=1)

    for g0 in range(0, R_PAIRS, group):
        pairs = list(range(g0, g0 + group))
        sls = [slice(p * 128, (p + 1) * 128) for p in pairs]
        each = lambda f: [f(i) for i in range(group)]

        S = each(lambda i: s_ref[pairs[i]])
        Rs = each(lambda i: stack(r_ref[0, :, sls[i]] * ec_all[:, sls[i]]))
        Bs = each(lambda i: stack(kk_ref[0, :, sls[i]] * ecm_all[:, sls[i]]))
        Ks = each(lambda i: stack(k_ref[0, :, sls[i]] * enc_all[:, sls[i]]))
        As = each(lambda i: stack(-(b_ref[0, :, sls[i]] * enc_all[:, sls[i]])))
        Vs = each(lambda i: stack(v_ref[0, :, sls[i]]))
        Kt = each(lambda i: stack(k_ref[0, :, sls[i]] * e_tail_all[:, sls[i]]))
        At = each(lambda i: stack(-(b_ref[0, :, sls[i]] * e_tail_all[:, sls[i]])))

        if P2 % 128 == 0:
            sc = each(lambda i: _dot_nt(cat0(Bs[i], Rs[i]), cat0(As[i], Ks[i])))
            s_ba = each(lambda i: sc[i][:P2, :P2])
            s_bk = each(lambda i: sc[i][:P2, P2:])
            s_ra = each(lambda i: sc[i][P2:, :P2])
            s_rk = each(lambda i: sc[i][P2:, P2:])
        else:
            s_ba = each(lambda i: _dot_nt(Bs[i], As[i]))
            s_bk = each(lambda i: _dot_nt(Bs[i], Ks[i]))
            s_ra = each(lambda i: _dot_nt(Rs[i], As[i]))
            s_rk = each(lambda i: _dot_nt(Rs[i], Ks[i]))
        Nm = each(lambda i: jnp.where(strict, s_ba[i], 0.0))
        Mbk = each(lambda i: jnp.where(strict, s_bk[i], 0.0))
        Mra = each(lambda i: jnp.where(incl, s_ra[i], 0.0))
        Mrk = each(lambda i: jnp.where(incl, s_rk[i], 0.0))

        Tm = each(lambda i: eye + Nm[i])
        Pw = Nm
        span = 2
        while span < L and not single_step:
            Pw = [_dot(x, x) for x in Pw]
            Tm = each(lambda i: Tm[i] + _dot(Tm[i], Pw[i]))
            span *= 2

        mv = each(lambda i: _dot(cat0(Mbk[i], Mrk[i]), Vs[i]))
        tb = each(lambda i: _dot(Tm[i], cat1(Bs[i], mv[i][:P2])))
        mu = each(lambda i: _dot(Mra[i], tb[i]))
        Ro = each(lambda i: Rs[i] + mu[i][:, :128])
        uo = each(lambda i: _dot_nt(cat0(tb[i][:, :128], Ro[i]), S[i]))
        U = each(lambda i: uo[i][:P2] + tb[i][:, 128:])
        O = each(lambda i: uo[i][P2:] + mv[i][P2:] + mu[i][:, 128:])
        for i, p in enumerate(pairs):
            s_ref[p] = S[i] * g_last_all[:, sls[i]] + _dot_tn(
                cat0(U[i], Vs[i]), cat0(At[i], Kt[i]))

        for i in range(group):
            sl = sls[i]
            out = O[i][:L] + O[i][L:]
            mean = _segsum(out, ones_bd) * inv_n
            d = out - mean
            var = _segsum(d * d, ones_bd) * inv_n
            y = d * lax.rsqrt(var + R_GN_EPS) * lng_ref[:, sl] + lnb_ref[:, sl]
            o_ref[0, :, sl] = ((y + bon_ref[0, :, sl]) * g_ref[0, :, sl]).astype(BF16)

    @pl.when(c_id == pl.num_programs(1) - 1)
    def _():
        for p in range(R_PAIRS):
            sp = s_ref[p]
            st_ref[n_earlier, 0, 2 * p] = sp[:R_HEAD, :R_HEAD]
            st_ref[n_earlier, 0, 2 * p + 1] = sp[R_HEAD:, R_HEAD:]
        for i, e_ref in enumerate(earlier):
            st_ref[i] = e_ref[0]


def _rwkv_scan(seqs, g, bonus, ln_g, ln_b, ones_bd, s0, l, earlier_s, *, L, t_valid):
    bsz, t, _ = seqs[0].shape
    nl = len(earlier_s) + 1
    seq_spec = pl.BlockSpec((1, L, BRANCH_W), lambda bi, c: (bi, c, 0))
    vec_spec = pl.BlockSpec((1, BRANCH_W), lambda bi, c: (0, 0))
    st_ea = pl.BlockSpec((1, 1, R_HEADS, R_HEAD, R_HEAD), lambda bi, c: (0, bi, 0, 0, 0))
    return pl.pallas_call(
        functools.partial(_rwkv_scan_kernel, L=L, group=R_PAIRS, n_earlier=nl - 1,
                          single_step=t_valid == 1),
        out_shape=[jax.ShapeDtypeStruct((bsz, t, BRANCH_W), BF16),
                   jax.ShapeDtypeStruct((nl, bsz, R_HEADS, R_HEAD, R_HEAD), F32)],
        grid=(bsz, t // L),
        in_specs=[seq_spec] * 8 + [
            vec_spec, vec_spec, pl.BlockSpec((128, 128), lambda bi, c: (0, 0)),
            pl.BlockSpec((1, 1, R_HEADS, R_HEAD, R_HEAD), lambda bi, c: (l, bi, 0, 0, 0))]
        + [st_ea] * (nl - 1),
        out_specs=[seq_spec,
                   pl.BlockSpec((nl, 1, R_HEADS, R_HEAD, R_HEAD), lambda bi, c: (0, bi, 0, 0, 0))],
        scratch_shapes=[pltpu.VMEM((R_PAIRS, 128, 128), F32)],
        compiler_params=_cparams(2),
        name="rwkv_scan",
    )(*seqs, g, bonus, ln_g, ln_b, ones_bd, s0, *earlier_s)


def _rwkv_mixer(proj3, small3, prev, s0, l, lw, ones_bd, earlier_s, *, L, tm):
    bsz, t, _ = proj3.shape
    new_shift = jnp.concatenate(
        [proj3[:, -1, :3 * BRANCH_W].astype(F32), small3[:, -1, :R_COLS - 3 * BRANCH_W]], axis=-1)
    prev3 = jnp.pad(prev, ((0, 0), (0, 3 * BRANCH_W + SMALL_W - R_COLS)))[:, None, :]
    if t == 1:
        outs = _rwkv_prep(proj3.reshape(1, bsz, -1), small3.reshape(1, bsz, -1),
                          prev3.reshape(1, bsz, -1), lw, ones_bd, tm=tm)
        outs = [o.reshape(bsz, 1, BRANCH_W) for o in outs]
    else:
        outs = _rwkv_prep(proj3, small3, prev3, lw, ones_bd, tm=tm)
    tp = -(-t // L) * L
    if tp != t:
        outs = [jnp.pad(o, ((0, 0), (0, tp - t), (0, 0))) for o in outs]
    r, k2, v, kkn, b, ld, g, bonus = outs
    o_r, s_new = _rwkv_scan((r, k2, v, kkn, b, ld), g, bonus, lw["ln_g"], lw["ln_b"],
                            ones_bd, s0, l, earlier_s, L=L, t_valid=t)
    return o_r[:, :t].reshape(bsz * t, BRANCH_W), s_new, new_shift


def _mlstm_kernel(q_ref, k_ref, v_ref, o_ref, sm_ref, cq_ref, ck_ref, wq_ref, wk_ref,
                  bq_ref, bk_ref, ib_ref, fb_ref, ng_ref, c0_ref, n0_ref, m0_ref,
                  *rest, L, t_valid, n_earlier):
    earlier = rest[:n_earlier]
    out_ref, c_ref, n_ref, m_ref, qs_ref, ks_ref = rest[n_earlier:]
    last = n_earlier
    c_id = pl.program_id(1)

    @pl.when(c_id == 0)
    def _():
        for i, e_ref in enumerate(earlier):
            c_ref[i] = e_ref[0]
        c_ref[last] = c0_ref[0]
        n_ref[...] = n0_ref[0]
        m_ref[...] = m0_ref[0]
        qs_ref[5:8, :] = cq_ref[0, 0]
        ks_ref[5:8, :] = ck_ref[0, 0]

    qs_ref[8:8 + L, :] = _rows(q_ref, L)
    ks_ref[8:8 + L, :] = _rows(k_ref, L)
    conv_q = bq_ref[...]
    conv_k = bk_ref[...]
    for j in range(CONV_W):
        conv_q = conv_q + qs_ref[5 + j:5 + j + L, :] * wq_ref[j:j + 1, :]
        conv_k = conv_k + ks_ref[5 + j:5 + j + L, :] * wk_ref[j:j + 1, :]
    if L >= CONV_W - 1:
        tail_q = qs_ref[5 + L:8 + L, :]
        tail_k = ks_ref[5 + L:8 + L, :]
        qs_ref[5:8, :] = tail_q
        ks_ref[5:8, :] = tail_k

    row = lax.broadcasted_iota(jnp.int32, (L, 1), 0)
    valid = (c_id * L + row) < t_valid
    q_all = jnp.where(valid, _silu(conv_q), 0.0)
    k_all = jnp.where(valid, _silu(conv_k) * (M_DK ** -0.5), 0.0)
    v_all = jnp.where(valid, _rows(v_ref, L), 0.0)
    gate_o = _sigmoid(_rows(o_ref, L))

    sm = _rows(sm_ref, L)
    lane = lax.broadcasted_iota(jnp.int32, (L, SMALL_W), 1)
    head_lane = lane < M_HEADS
    i_pre = jnp.where(head_lane, pltpu.roll(sm, SMALL_W - S_I, axis=1), 0.0)
    f_pre = jnp.where(head_lane, pltpu.roll(sm, SMALL_W - S_F, axis=1), 0.0)
    ig4 = jnp.where(valid & head_lane, i_pre + ib_ref[...], NEG)
    lf4 = jnp.where(valid & head_lane, _log_sigmoid(f_pre + fb_ref[...]), 0.0)

    ti = lax.broadcasted_iota(jnp.int32, (L, L), 0)
    si = lax.broadcasted_iota(jnp.int32, (L, L), 1)
    causal = si <= ti
    diag = ti == si
    ones_l = jnp.ones((L, L), F32)
    F4 = _cumsum_rows(lf4, t_valid == 1)
    gmf4 = ig4 - F4

    heads = range(M_HEADS)
    each = lambda f: [f(h) for h in heads]
    sls = [slice(h * M_DK, (h + 1) * M_DK) for h in heads]
    rowsum = lambda x: jnp.sum(x, axis=-1, keepdims=True)
    q = each(lambda h: q_all[:, sls[h]])
    k = each(lambda h: k_all[:, sls[h]])
    v = each(lambda h: v_all[:, sls[h]])
    F = each(lambda h: rowsum(jnp.where(lane == h, F4, 0.0)))
    ig = each(lambda h: rowsum(jnp.where(lane == h, ig4, 0.0)))
    gmf = each(lambda h: rowsum(jnp.where(lane == h, gmf4, 0.0)))
    if t_valid == 1:
        g_row = each(lambda h: jnp.where(si == 0, gmf[h][0:1, :], NEG))
    else:
        g_row = each(lambda h: _dot_hi(
            ones_l, jnp.where(diag, jnp.broadcast_to(gmf[h], (L, L)), 0.0)))
    Dm = each(lambda h: jnp.where(causal, F[h] + g_row[h], NEG))

    C = each(lambda h: c_ref[last, 0, h])
    n = each(lambda h: n_ref[0, h])
    m_prev = each(lambda h: m_ref[0, h])
    inter = each(lambda h: F[h] + m_prev[h])
    m_t = each(lambda h: jnp.maximum(inter[h], jnp.max(Dm[h], axis=-1, keepdims=True)))
    w_inter = each(lambda h: jnp.exp(inter[h] - m_t[h]))
    Sm = each(lambda h: _dot_nt(q[h], k[h]) * jnp.exp(Dm[h] - m_t[h]))
    num = each(lambda h: w_inter[h] * _dot(q[h], C[h]) + _dot(Sm[h], v[h]))
    den = each(lambda h: w_inter[h] * rowsum(q[h] * n[h]) + rowsum(Sm[h]))
    hh = each(lambda h: num[h] / jnp.maximum(jnp.abs(den[h]), jnp.exp(-m_t[h])))

    FL = each(lambda h: F[h][L - 1:L, :])
    g_s = each(lambda h: FL[h] - F[h] + ig[h])
    m_new = each(lambda h: jnp.maximum(FL[h] + m_prev[h], jnp.max(g_s[h], axis=0, keepdims=True)))
    a_c = each(lambda h: jnp.exp(FL[h] + m_prev[h] - m_new[h]))
    kw = each(lambda h: k[h] * jnp.exp(g_s[h] - m_new[h]))
    for h in heads:
        c_ref[last, 0, h] = a_c[h] * C[h] + _dot_tn(kw[h], v[h])
        n_ref[0, h] = a_c[h] * n[h] + jnp.sum(kw[h], axis=0, keepdims=True)
        m_ref[0, h] = m_new[h]
        hn = hh[h] * lax.rsqrt(jnp.mean(hh[h] * hh[h], axis=-1, keepdims=True) + EPS)
        out_ref[0, :, sls[h]] = (gate_o[:, sls[h]] * hn * ng_ref[:, sls[h]]).astype(BF16)


def _rows(ref, L):
    x = ref[0].astype(F32)
    if x.shape[0] == L:
        return x
    assert x.shape[0] == 1
    row = lax.broadcasted_iota(jnp.int32, (L, x.shape[1]), 0)
    return jnp.where(row == 0, x, 0.0)


def _mlstm_mixer(proj3, small3, conv_buf, c0, n0, m0, l, lw, earlier_c, *, L):
    bsz, t, _ = proj3.shape
    tb = min(t, L)
    nc = -(-t // L)
    nl = len(earlier_c) + 1
    seq = lambda col: pl.BlockSpec((1, tb, BRANCH_W), lambda bi, c: (bi, c, col // BRANCH_W))
    st_c = pl.BlockSpec((nl, 1, M_HEADS, M_DK, M_DK), lambda bi, c: (0, bi, 0, 0, 0))
    ea_c = pl.BlockSpec((1, 1, M_HEADS, M_DK, M_DK), lambda bi, c: (0, bi, 0, 0, 0))
    st_n = pl.BlockSpec((1, M_HEADS, 1, M_DK), lambda bi, c: (bi, 0, 0, 0))
    st_m = pl.BlockSpec((1, M_HEADS, 1, 1), lambda bi, c: (bi, 0, 0, 0))
    in_c = pl.BlockSpec((1, 1, M_HEADS, M_DK, M_DK), lambda bi, c: (l, bi, 0, 0, 0))
    in_n = pl.BlockSpec((1, 1, M_HEADS, 1, M_DK), lambda bi, c: (l, bi, 0, 0, 0))
    in_m = pl.BlockSpec((1, 1, M_HEADS, 1, 1), lambda bi, c: (l, bi, 0, 0, 0))
    conv = lambda blk: pl.BlockSpec((1, 1, CONV_W - 1, BRANCH_W), lambda bi, c: (l, bi, 0, blk))
    cw = lambda blk: pl.BlockSpec((CONV_W, BRANCH_W), lambda bi, c: (0, blk))
    vec = lambda blk: pl.BlockSpec((1, BRANCH_W), lambda bi, c: (0, blk))
    hb = pl.BlockSpec((1, SMALL_W), lambda bi, c: (0, 0))
    pad_heads = lambda a: jnp.pad(a, ((0, 0), (0, SMALL_W - M_HEADS)))
    out, c_new, n_new, m_new = pl.pallas_call(
        functools.partial(_mlstm_kernel, L=L, t_valid=t, n_earlier=nl - 1),
        out_shape=[jax.ShapeDtypeStruct((bsz, nc * L, BRANCH_W), BF16),
                   jax.ShapeDtypeStruct((nl, bsz, M_HEADS, M_DK, M_DK), F32),
                   jax.ShapeDtypeStruct((bsz, M_HEADS, 1, M_DK), F32),
                   jax.ShapeDtypeStruct((bsz, M_HEADS, 1, 1), F32)],
        grid=(bsz, nc),
        in_specs=[seq(C_MQK), seq(C_MQK + BRANCH_W), seq(C_MV), seq(C_MO),
                  pl.BlockSpec((1, tb, SMALL_W), lambda bi, c: (bi, c, 0)),
                  conv(0), conv(1), cw(0), cw(1), vec(0), vec(1), hb, hb, vec(0),
                  in_c, in_n, in_m] + [ea_c] * (nl - 1),
        out_specs=[pl.BlockSpec((1, L, BRANCH_W), lambda bi, c: (bi, c, 0)), st_c, st_n, st_m],
        scratch_shapes=[pltpu.VMEM((L + 8, BRANCH_W), F32), pltpu.VMEM((L + 8, BRANCH_W), F32)],
        compiler_params=_cparams(2),
        name="mlstm_scan",
    )(proj3, proj3, proj3, proj3, small3, conv_buf, conv_buf,
      lw["conv_w"], lw["conv_w"], lw["conv_b"], lw["conv_b"],
      pad_heads(lw["i_b"]), pad_heads(lw["f_b"]),
      lw["m_norm_g"], c0, n0.reshape(DEPTH, bsz, M_HEADS, 1, M_DK),
      m0.reshape(DEPTH, bsz, M_HEADS, 1, 1), *earlier_c)
    return out, c_new, n_new.reshape(bsz, M_HEADS, M_DK), m_new.reshape(bsz, M_HEADS)


def _gla_kernel(q_ref, k_ref, v_ref, og_ref, sm_ref, a2_ref, ab_ref, ng_ref, s0_ref,
                *rest, L, t_valid, n_earlier):
    earlier = rest[:n_earlier]
    out_ref, s_ref, b_scr, q_scr = rest[n_earlier:]
    last = n_earlier
    c_id = pl.program_id(1)

    @pl.when(c_id == 0)
    def _():
        for i, e_ref in enumerate(earlier):
            s_ref[i] = e_ref[0]
        s_ref[last] = s0_ref[0]

    row = lax.broadcasted_iota(jnp.int32, (L, 1), 0)
    valid = (c_id * L + row) < t_valid
    q_all = jnp.where(valid, _rows(q_ref, L) * (G_DK ** -0.5), 0.0)
    k_all = jnp.where(valid, _rows(k_ref, L), 0.0)
    v_all = jnp.where(valid, _rows(v_ref, L), 0.0)
    gate_o = _silu(_rows(og_ref, L))
    lg = _log_sigmoid(_dot(_rows(sm_ref, L), a2_ref[...]) + ab_ref[...]) * (1.0 / G_GATE_NORM)
    lg = jnp.where(valid, lg, 0.0)

    ti = lax.broadcasted_iota(jnp.int32, (L, L), 0)
    si = lax.broadcasted_iota(jnp.int32, (L, L), 1)
    b_all = _cumsum_rows(lg, t_valid == 1)
    b_scr[...] = b_all
    q_scr[...] = q_all
    eb_all = jnp.exp(b_all)
    b_last_all = b_all[L - 1:L, :]
    e_tail_all = jnp.exp(b_last_all - b_all)
    ones_lv = jnp.ones((L, G_DV), F32)

    n_t = L if t_valid >= L else t_valid
    SB = GLA_SUB
    s_col = {rows: lax.broadcasted_iota(jnp.int32, (rows, 1), 0) for rows in (8, SB)}
    t_lane = {rows: lax.broadcasted_iota(jnp.int32, (rows, SB), 1) for rows in (8, SB)}

    heads = range(G_HEADS)
    sls = [slice(h * G_DK, (h + 1) * G_DK) for h in heads]
    svs = [slice(h * G_DV, (h + 1) * G_DV) for h in heads]

    o_parts = [[] for _ in heads]
    for r0 in range(0, L, SB):
        n_sub = max(0, min(SB, n_t - r0))
        for h in heads:
            sl, sv = sls[h], svs[h]
            ki, bi, vi = k_all[r0:r0 + SB, sl], b_all[r0:r0 + SB, sl], v_all[r0:r0 + SB, sv]
            att = jnp.zeros((SB, SB), F32)
            for tl in range(n_sub):
                rows = 8 * (tl // 8 + 1)
                bt = b_scr[r0 + tl:r0 + tl + 1, sl]
                qt = q_scr[r0 + tl:r0 + tl + 1, sl]
                e = jnp.exp(jnp.where(s_col[rows] <= tl, bt - bi[:rows], NEG))
                col = jnp.sum(qt * ki[:rows] * e, axis=-1, keepdims=True)
                top = jnp.where(t_lane[rows] == tl, col, att[:rows])
                att = top if rows == SB else jnp.concatenate([top, att[rows:]], axis=0)
            o_i = _dot_tn(att, vi)
            if r0 > 0 and n_sub > 0:
                ref = b_scr[r0 - 1:r0, sl]
                a_off = _dot_nt(q_all[r0:r0 + SB, sl] * jnp.exp(bi - ref),
                                k_all[:r0, sl] * jnp.exp(ref - b_all[:r0, sl]))
                o_i = o_i + _dot(a_off, v_all[:r0, sv])
            o_parts[h].append(o_i)

    for h in heads:
        sl, sv = sls[h], svs[h]
        q, k, v = q_all[:, sl], k_all[:, sl], v_all[:, sv]
        o_intra = o_parts[h][0] if len(o_parts[h]) == 1 else jnp.concatenate(o_parts[h], axis=0)

        S = s_ref[last, 0, h]
        o = _dot(q * eb_all[:, sl], S) + o_intra
        decay = jnp.exp(lax.dot_general(lg[:, sl], ones_lv, (((0,), (0,)), ((), ())),
                                        precision=lax.Precision.HIGHEST,
                                        preferred_element_type=F32))
        s_ref[last, 0, h] = S * decay + _dot_tn(k * e_tail_all[:, sl], v)

        on = o * lax.rsqrt(jnp.mean(o * o, axis=-1, keepdims=True) + EPS) * ng_ref[:, sv]
        out_ref[0, :, sv] = (on * gate_o[:, sv]).astype(BF16)


def _gla_mixer(proj3, small3, s0, l, lw, earlier_s, *, L):
    bsz, t, _ = proj3.shape
    tb = min(t, L)
    nc = -(-t // L)
    nl = len(earlier_s) + 1
    gw = G_HEADS * G_DK
    st = pl.BlockSpec((nl, 1, G_HEADS, G_DK, G_DV), lambda bi, c: (0, bi, 0, 0, 0))
    st_ea = pl.BlockSpec((1, 1, G_HEADS, G_DK, G_DV), lambda bi, c: (0, bi, 0, 0, 0))
    st_in = pl.BlockSpec((1, 1, G_HEADS, G_DK, G_DV), lambda bi, c: (l, bi, 0, 0, 0))
    return pl.pallas_call(
        functools.partial(_gla_kernel, L=L, t_valid=t, n_earlier=nl - 1),
        out_shape=[jax.ShapeDtypeStruct((bsz, nc * L, BRANCH_W), BF16),
                   jax.ShapeDtypeStruct((nl, bsz, G_HEADS, G_DK, G_DV), F32)],
        grid=(bsz, nc),
        in_specs=[pl.BlockSpec((1, tb, gw), lambda bi, c: (bi, c, C_GQ // gw)),
                  pl.BlockSpec((1, tb, gw), lambda bi, c: (bi, c, C_GQ // gw + 1)),
                  pl.BlockSpec((1, tb, BRANCH_W), lambda bi, c: (bi, c, C_GV // BRANCH_W)),
                  pl.BlockSpec((1, tb, BRANCH_W), lambda bi, c: (bi, c, C_GOG // BRANCH_W)),
                  pl.BlockSpec((1, tb, SMALL_W), lambda bi, c: (bi, c, 0)),
                  pl.BlockSpec((SMALL_W, gw), lambda bi, c: (0, 0)),
                  pl.BlockSpec((1, gw), lambda bi, c: (0, 0)),
                  pl.BlockSpec((1, BRANCH_W), lambda bi, c: (0, 0)),
                  st_in] + [st_ea] * (nl - 1),
        out_specs=[pl.BlockSpec((1, L, BRANCH_W), lambda bi, c: (bi, c, 0)), st],
        scratch_shapes=[pltpu.VMEM((L, gw), F32), pltpu.VMEM((L, gw), F32)],
        compiler_params=_cparams(2),
        name="gla_scan",
    )(proj3, proj3, proj3, proj3, small3, lw["g_a2p"], lw["g_a_b"], lw["g_norm_g"], s0,
      *earlier_s)


def _small_group_rows(wt):
    depth, _, k = wt.shape
    used = (R_COLS - 3072) + 2 * M_HEADS + G_LR
    return jnp.concatenate(
        [wt[:, W_R0 + 3072:W_R0 + 3264], wt[:, W_M0 + 3072:W_M0 + 3080],
         wt[:, W_G0 + 2048:W_G0 + 2064], jnp.zeros((depth, MAIN_TN - used, k), wt.dtype)], axis=1)


def _rows_padded(w, row0, total):
    return jnp.pad(w, ((row0, total - row0 - w.shape[0]), (0, 0)))


def _layer_weights(l, P):
    mu = P["rwkv_mu"][l]
    mu_p = jnp.concatenate([mu, jnp.zeros((3 * BRANCH_W + SMALL_W - R_COLS,), F32)]).reshape(1, -1)
    row = lambda a: a.reshape(1, -1)
    return {
        "norm1_g": P["norm1_g"][l], "gate_b": P["gate_b"][l],
        "mu_p": mu_p, "w0": row(P["rwkv_w0"][l]), "a0": row(P["rwkv_a0"][l]),
        "k_k": row(P["rwkv_k_k"][l]), "k_a": row(P["rwkv_k_a"][l]), "r_k": row(P["rwkv_r_k"][l]),
        "w2p": _rows_padded(P["rwkv_w2"][l], 0, SMALL_W),
        "a2p": _rows_padded(P["rwkv_a2"][l], R_LORA, SMALL_W),
        "g2p": _rows_padded(P["rwkv_g2"][l], 2 * R_LORA, SMALL_W),
        "ln_g": row(P["rwkv_ln_g"][l]), "ln_b": row(P["rwkv_ln_b"][l]),
        "conv_w": P["mlstm_conv_w"][l], "conv_b": row(P["mlstm_conv_b"][l]),
        "i_b": row(P["mlstm_i_b"][l]), "f_b": row(P["mlstm_f_b"][l]),
        "m_norm_g": row(P["mlstm_norm_g"][l]),
        "g_a2p": _rows_padded(P["gla_a2"][l], S_GXA, SMALL_W), "g_a_b": row(P["gla_a_b"][l]),
        "g_norm_g": row(P["gla_norm_g"][l]),
        "norm2_g": P["norm2_g"][l],
    }


def _layer(x2, bsz, t, states, l, lw, big, ones_bd, cfg, earlier):
    ea_wkv, ea_c, ea_s = earlier
    rw_prev, rw_s, m_conv, m_c, m_n, m_m, g_s = states
    m = bsz * t
    L, tm = cfg["L"], cfg["tm"]
    proj, small = _rms_matmul(x2, lw["norm1_g"], big["w_in_t"], big["w_in_t_small"], l, tm=tm,
                              main_dtype=cfg["proj_dtype"])
    proj3 = proj.reshape(bsz, t, C_SMALL)
    small3 = small.reshape(bsz, t, -1)

    o_r, rw_s_new, rw_prev_new = _rwkv_mixer(proj3, small3, rw_prev[l], rw_s, l, lw, ones_bd,
                                             ea_wkv, L=cfg["L_rwkv"], tm=cfg["tm_prep"])

    o_m, m_c_new, m_n_new, m_m_new = _mlstm_mixer(proj3, small3, m_conv, m_c, m_n, m_m, l, lw,
                                                  ea_c, L=L)
    qk_tail = proj3[:, -min(t, CONV_W - 1):, C_MQK:C_MQK + 2 * BRANCH_W].astype(F32)
    m_conv_new = jnp.concatenate([m_conv[l], qk_tail], axis=1)[:, -(CONV_W - 1):]
    o_g, g_s_new = _gla_mixer(proj3, small3, g_s, l, lw, ea_s, L=L)
    o_m = o_m[:, :t].reshape(m, BRANCH_W)
    o_g = o_g[:, :t].reshape(m, BRANCH_W)

    merged = _merge(o_r, o_m, o_g, big["w_branch"], l, proj, lw["gate_b"],
                    tm=cfg["tm_merge"], tn=512)
    x2 = _matmul_residual(merged, big["w_out"], l, x2, tm=tm, tn=512, tk=D_MODEL)
    hidden = _rms_swiglu(x2, lw["norm2_g"], big["w_gu"], l, tm=tm, tn=512)
    x2 = _matmul_residual(hidden, big["w_down"], l, x2, tm=tm, tn=512, tk=2816)
    return x2, (rw_prev_new, rw_s_new, m_conv_new, m_c_new, m_n_new, m_m_new, g_s_new)


def _trunk(x, states, layer_ws, big, final_g, ones_bd, cfg):
    bsz, t, d = x.shape
    x2 = x.reshape(bsz * t, d)
    per_layer = []
    for l in range(DEPTH):
        is_last = l == DEPTH - 1
        earlier = tuple([st[i] for st in per_layer] if is_last else [] for i in BIG_STATES)
        x2, new = _layer(x2, bsz, t, states, l, layer_ws[l], big, ones_bd, cfg, earlier)
        per_layer.append(new)
    new_states = [per_layer[-1][i] if i in BIG_STATES
                  else jnp.stack([st[i] for st in per_layer], axis=0) for i in range(len(states))]
    y = _rmsnorm(x2, final_g, tm=cfg["tm_norm"]).reshape(bsz, t, d)
    return y, new_states


BIG_STATES = (1, 3, 6)

PROMPT_CFG = dict(L=64, L_rwkv=64, tm=1024, tm_prep=256, tm_merge=1024, tm_norm=512, proj_dtype=BF16)
SAMPLE_CFG = dict(L=16, L_rwkv=16, tm=128, tm_prep=128, tm_merge=128, tm_norm=128, proj_dtype=F32)


def _zero_states(bsz):
    return (jnp.zeros((DEPTH, bsz, R_COLS), F32),
            jnp.zeros((DEPTH, bsz, R_HEADS, R_HEAD, R_HEAD), F32),
            jnp.zeros((DEPTH, bsz, CONV_W - 1, 2 * BRANCH_W), F32),
            jnp.zeros((DEPTH, bsz, M_HEADS, M_DK, M_DK), F32),
            jnp.zeros((DEPTH, bsz, M_HEADS, M_DK), F32),
            jnp.zeros((DEPTH, bsz, M_HEADS), F32),
            jnp.zeros((DEPTH, bsz, G_HEADS, G_DK, G_DV), F32))


def kernel(x_prompt, x_sample, state_rwkv_shift, state_rwkv_wkv, state_mlstm_conv, state_mlstm_C, state_mlstm_n, state_mlstm_m, state_gla_S, norm1_g, w_in, gate_b, rwkv_mu, rwkv_w0, rwkv_w2, rwkv_a0, rwkv_a2, rwkv_g2, rwkv_k_k, rwkv_k_a, rwkv_r_k, rwkv_ln_g, rwkv_ln_b, mlstm_conv_w, mlstm_conv_b, mlstm_i_b, mlstm_f_b, mlstm_norm_g, gla_a2, gla_a_b, gla_norm_g, w_branch, w_out, norm2_g, ffn_w_gu, ffn_w_down, final_norm_g):
    P = dict(norm1_g=norm1_g, w_in=w_in, gate_b=gate_b, rwkv_mu=rwkv_mu, rwkv_w0=rwkv_w0,
             rwkv_w2=rwkv_w2, rwkv_a0=rwkv_a0, rwkv_a2=rwkv_a2, rwkv_g2=rwkv_g2,
             rwkv_k_k=rwkv_k_k, rwkv_k_a=rwkv_k_a, rwkv_r_k=rwkv_r_k, rwkv_ln_g=rwkv_ln_g,
             rwkv_ln_b=rwkv_ln_b, mlstm_conv_w=mlstm_conv_w, mlstm_conv_b=mlstm_conv_b,
             mlstm_i_b=mlstm_i_b, mlstm_f_b=mlstm_f_b, mlstm_norm_g=mlstm_norm_g,
             gla_a2=gla_a2, gla_a_b=gla_a_b, gla_norm_g=gla_norm_g, w_branch=w_branch,
             w_out=w_out, norm2_g=norm2_g, ffn_w_gu=ffn_w_gu, ffn_w_down=ffn_w_down)
    layer_ws = [_layer_weights(l, P) for l in range(DEPTH)]
    w_in_t = jnp.swapaxes(w_in, 1, 2)
    big = dict(w_in_t=w_in_t, w_in_t_small=_small_group_rows(w_in_t), w_branch=w_branch,
               w_out=w_out, w_gu=ffn_w_gu, w_down=ffn_w_down)
    head_of_lane = jnp.arange(128) // R_HEAD
    ones_bd = (head_of_lane[:, None] == head_of_lane[None, :]).astype(BF16)

    y_p, p_states = _trunk(x_prompt, _zero_states(x_prompt.shape[0]), layer_ws, big,
                           final_norm_g, ones_bd, PROMPT_CFG)
    s_states = (state_rwkv_shift, state_rwkv_wkv, state_mlstm_conv, state_mlstm_C,
                state_mlstm_n, state_mlstm_m, state_gla_S)
    y_s, s_states = _trunk(x_sample, s_states, layer_ws, big, final_norm_g, ones_bd, SAMPLE_CFG)
    return (y_p, y_s, *p_states, *s_states)
```

```python
import functools

import jax
import jax.numpy as jnp
from jax import lax
from jax.experimental import pallas as pl
from jax.experimental.pallas import tpu as pltpu

F32 = jnp.float32
BF16 = jnp.bfloat16

D_MODEL = 2048
DEPTH = 2
BRANCH_W = 1024
R_HEADS, R_HEAD = 16, 64
R_PAIRS = R_HEADS // 2
R_LORA = 64
R_COLS = 3 * BRANCH_W + 3 * R_LORA
R_GN_EPS = 64e-5
M_HEADS, M_DK = 4, 256
CONV_W = 4
G_HEADS, G_DK, G_DV = 4, 128, 256
G_LR = 16
G_GATE_NORM = 16.0
D_FF = 5632
EPS = 1e-6
NEG = -1e30
GLA_SUB = 16

C_RWKV = 0
C_MQK = 3072
C_MV = 5120
C_MO = 6144
C_GQ = 7168
C_GV = 8192
C_GOG = 9216
C_GATE = 10240
C_SMALL = 16384
SMALL_W = 256
S_I, S_F, S_GXA = 192, 196, 200
MAIN_TN = 512
W_R0, W_M0, W_G0, W_T0 = 0, R_COLS, R_COLS + 4104, R_COLS + 4104 + 3088
MAIN_RUNS = ((C_RWKV, W_R0, 3072), (C_MQK, W_M0, 3072), (C_MO, W_M0 + 3080, 1024),
             (C_GQ, W_G0, 2048), (C_GOG, W_G0 + 2064, 1024 + 3 * D_MODEL))

VMEM_LIMIT = 56 * 1024 * 1024


def _cparams(n_axes):
    return pltpu.CompilerParams(dimension_semantics=("arbitrary",) * n_axes,
                                vmem_limit_bytes=VMEM_LIMIT)


def _dot(a, b):
    return jnp.dot(a.astype(BF16), b.astype(BF16), preferred_element_type=F32)


def _dot_nt(a, b):
    return lax.dot_general(a.astype(BF16), b.astype(BF16), (((1,), (1,)), ((), ())),
                           preferred_element_type=F32)


def _dot_tn(a, b):
    return lax.dot_general(a.astype(BF16), b.astype(BF16), (((0,), (0,)), ((), ())),
                           preferred_element_type=F32)


def _dot_hi(a, b):
    return jnp.dot(a, b, precision=lax.Precision.HIGHEST, preferred_element_type=F32)


def _cumsum_rows(x, single_step):
    n = x.shape[0]
    if single_step:
        row = lax.broadcasted_iota(jnp.int32, x.shape, 0)
        return jnp.where(row == 0, x, x[0:1, :])
    ti = lax.broadcasted_iota(jnp.int32, (n, n), 0)
    si = lax.broadcasted_iota(jnp.int32, (n, n), 1)
    return _dot_hi((si <= ti).astype(F32), x)


def _segsum(y, ones_blockdiag):
    hi = y.astype(BF16)
    lo = (y - hi.astype(F32)).astype(BF16)
    return (jnp.dot(hi, ones_blockdiag, preferred_element_type=F32)
            + jnp.dot(lo, ones_blockdiag, preferred_element_type=F32))


def _sigmoid(x):
    return 1.0 / (1.0 + jnp.exp(-x))


def _silu(x):
    return x * _sigmoid(x)


def _log_sigmoid(x):
    return -_softplus(-x)


def _softplus(x):
    return jnp.maximum(x, 0.0) + jnp.log(1.0 + jnp.exp(-jnp.abs(x)))


def _rms_mm_kernel(start_ref, x_ref, g_ref, wt_ref, wsm_ref, o_ref, osm_ref, h_scr, *, n_main):
    del start_ref
    j = pl.program_id(1)

    @pl.when(j == 0)
    def _():
        x = x_ref[...]
        ms = jnp.mean(x * x, axis=-1, keepdims=True)
        h_scr[...] = (x * lax.rsqrt(ms + EPS) * g_ref[...]).astype(BF16)

    @pl.when(j < n_main)
    def _():
        o_ref[...] = _dot_nt(h_scr[...], wt_ref[0]).astype(o_ref.dtype)

    @pl.when(j >= n_main)
    def _():
        osm_ref[...] = _dot_nt(h_scr[...], wsm_ref[0])


def _rms_matmul(x, g, wt, wt_small, l, *, tm, main_dtype):
    m, k = x.shape
    n_in = wt.shape[1]
    tn = MAIN_TN
    starts = []
    for p0, s0, width in MAIN_RUNS:
        assert p0 == len(starts) * tn and width % tn == 0 and s0 % 8 == 0
        starts += list(range(s0, s0 + width, tn))
    n_main = len(starts)
    assert n_main * tn == C_SMALL and starts[-1] + tn == n_in
    grid_spec = pltpu.PrefetchScalarGridSpec(
        num_scalar_prefetch=1,
        grid=(m // tm, n_main + 1),
        in_specs=[pl.BlockSpec((tm, k), lambda i, j, st: (i, 0)),
                  pl.BlockSpec((1, k), lambda i, j, st: (0, 0)),
                  pl.BlockSpec((pl.Element(1), pl.Element(tn), pl.Element(k)),
                               lambda i, j, st: (l, st[j] * 8, 0)),
                  pl.BlockSpec((1, tn, k), lambda i, j, st: (l, 0, 0))],
        out_specs=[pl.BlockSpec((tm, tn), lambda i, j, st: (i, jnp.minimum(j, n_main - 1))),
                   pl.BlockSpec((tm, tn), lambda i, j, st: (i, 0))],
        scratch_shapes=[pltpu.VMEM((tm, k), BF16)],
    )
    return pl.pallas_call(
        functools.partial(_rms_mm_kernel, n_main=n_main),
        out_shape=[jax.ShapeDtypeStruct((m, C_SMALL), main_dtype),
                   jax.ShapeDtypeStruct((m, tn), F32)],
        grid_spec=grid_spec,
        compiler_params=_cparams(2),
        name="rms_in_proj",
    )(jnp.asarray([s // 8 for s in starts] + [0], jnp.int32), x, g.reshape(1, k), wt, wt_small)


def _rms_swiglu_kernel(x_ref, g_ref, wg_ref, wu_ref, o_ref, h_scr):
    @pl.when(pl.program_id(1) == 0)
    def _():
        x = x_ref[...]
        ms = jnp.mean(x * x, axis=-1, keepdims=True)
        h_scr[...] = (x * lax.rsqrt(ms + EPS) * g_ref[...]).astype(BF16)

    h = h_scr[...]
    gg = jnp.dot(h, wg_ref[0].astype(BF16), preferred_element_type=F32)
    uu = jnp.dot(h, wu_ref[0].astype(BF16), preferred_element_type=F32)
    o_ref[...] = (_silu(gg) * uu).astype(BF16)


def _rms_swiglu(x, g, w_gu, l, *, tm, tn):
    m, k = x.shape
    nj = D_FF // tn
    return pl.pallas_call(
        _rms_swiglu_kernel,
        out_shape=jax.ShapeDtypeStruct((m, D_FF), BF16),
        grid=(m // tm, nj),
        in_specs=[pl.BlockSpec((tm, k), lambda i, j: (i, 0)),
                  pl.BlockSpec((1, k), lambda i, j: (0, 0)),
                  pl.BlockSpec((1, k, tn), lambda i, j: (l, 0, j)),
                  pl.BlockSpec((1, k, tn), lambda i, j: (l, 0, j + nj))],
        out_specs=pl.BlockSpec((tm, tn), lambda i, j: (i, j)),
        scratch_shapes=[pltpu.VMEM((tm, k), BF16)],
        compiler_params=_cparams(2),
        name="rms_ffn_swiglu",
    )(x, g.reshape(1, k), w_gu, w_gu)


def _mm_res_kernel(a_ref, w_ref, res_ref, o_ref, acc_ref, *, nk):
    kk = pl.program_id(2)

    @pl.when(kk == 0)
    def _():
        acc_ref[...] = jnp.zeros_like(acc_ref)

    acc_ref[...] += jnp.dot(a_ref[...], w_ref[0].astype(BF16), preferred_element_type=F32)

    @pl.when(kk == nk - 1)
    def _():
        o_ref[...] = acc_ref[...] + res_ref[...]


def _matmul_residual(a, w, l, res, *, tm, tn, tk):
    m, k = a.shape
    n = w.shape[2]
    nk = k // tk
    return pl.pallas_call(
        functools.partial(_mm_res_kernel, nk=nk),
        out_shape=jax.ShapeDtypeStruct((m, n), F32),
        grid=(m // tm, n // tn, nk),
        in_specs=[pl.BlockSpec((tm, tk), lambda i, j, kk: (i, kk)),
                  pl.BlockSpec((1, tk, tn), lambda i, j, kk: (l, kk, j)),
                  pl.BlockSpec((tm, tn), lambda i, j, kk: (i, j))],
        out_specs=pl.BlockSpec((tm, tn), lambda i, j, kk: (i, j)),
        scratch_shapes=[pltpu.VMEM((tm, tn), F32)],
        compiler_params=_cparams(3),
        name="matmul_residual",
    )(a, w, res)


def _merge_kernel(or_ref, om_ref, og_ref, wr_ref, wm_ref, wg_ref,
                  gr_ref, gm_ref, gg_ref, br_ref, bm_ref, bg_ref, o_ref):
    acc = _sigmoid(gr_ref[...].astype(F32) + br_ref[0]) * jnp.dot(
        or_ref[...], wr_ref[0, 0].astype(BF16), preferred_element_type=F32)
    acc += _sigmoid(gm_ref[...].astype(F32) + bm_ref[0]) * jnp.dot(
        om_ref[...], wm_ref[0, 0].astype(BF16), preferred_element_type=F32)
    acc += _sigmoid(gg_ref[...].astype(F32) + bg_ref[0]) * jnp.dot(
        og_ref[...], wg_ref[0, 0].astype(BF16), preferred_element_type=F32)
    o_ref[...] = acc.astype(BF16)


def _merge(o_r, o_m, o_g, w_branch, l, proj, gate_b, *, tm, tn):
    m = o_r.shape[0]
    gate_blk = C_GATE // tn
    per = D_MODEL // tn
    o_spec = pl.BlockSpec((tm, BRANCH_W), lambda i, j: (i, 0))

    def w_spec(b):
        return pl.BlockSpec((1, 1, BRANCH_W, tn), lambda i, j: (l, b, 0, j))

    def g_spec(b):
        return pl.BlockSpec((tm, tn), lambda i, j: (i, gate_blk + b * per + j))

    def b_spec(b):
        return pl.BlockSpec((1, 1, tn), lambda i, j: (b, 0, j))

    gate_b = gate_b.reshape(3, 1, D_MODEL)

    return pl.pallas_call(
        _merge_kernel,
        out_shape=jax.ShapeDtypeStruct((m, D_MODEL), BF16),
        grid=(m // tm, per),
        in_specs=[o_spec, o_spec, o_spec, w_spec(0), w_spec(1), w_spec(2),
                  g_spec(0), g_spec(1), g_spec(2), b_spec(0), b_spec(1), b_spec(2)],
        out_specs=pl.BlockSpec((tm, tn), lambda i, j: (i, j)),
        compiler_params=_cparams(2),
        name="gated_merge",
    )(o_r, o_m, o_g, w_branch, w_branch, w_branch, proj, proj, proj, gate_b, gate_b, gate_b)


def _rmsnorm_kernel(x_ref, g_ref, o_ref):
    x = x_ref[...]
    ms = jnp.mean(x * x, axis=-1, keepdims=True)
    o_ref[...] = x * lax.rsqrt(ms + EPS) * g_ref[...]


def _rmsnorm(x, g, *, tm):
    m, k = x.shape
    return pl.pallas_call(
        _rmsnorm_kernel,
        out_shape=jax.ShapeDtypeStruct((m, k), F32),
        grid=(m // tm,),
        in_specs=[pl.BlockSpec((tm, k), lambda i: (i, 0)),
                  pl.BlockSpec((1, k), lambda i: (0, 0))],
        out_specs=pl.BlockSpec((tm, k), lambda i: (i, 0)),
        compiler_params=_cparams(1),
        name="final_rmsnorm",
    )(x, g.reshape(1, k))


def _rwkv_prep_kernel(pr_ref, pk_ref, pv_ref, ps_ref, qr_ref, qk_ref, qv_ref, qs_ref,
                      mur_ref, muk_ref, muv_ref, mus_ref,
                      w0_ref, a0_ref, kk_ref, ka_ref, rk_ref,
                      w2_ref, a2_ref, g2_ref, j_ref,
                      r_out, k_out, v_out, kkn_out, b_out, ld_out, g_out, bon_out,
                      *scratch, tm, explicit_prev):
    def shift(p_ref, q_ref, mu_ref, scr):
        p = p_ref[0].astype(F32)
        if explicit_prev:
            prev = q_ref[0]
        else:
            @pl.when(pl.program_id(1) == 0)
            def _():
                scr[7:8, :] = q_ref[0]

            scr[8:8 + tm, :] = p
            prev = scr[7:7 + tm, :]
            scr[7:8, :] = p[tm - 1:tm, :]
        return p + (prev - p) * mu_ref[...]

    scr = scratch if scratch else (None,) * 4
    xr = shift(pr_ref, qr_ref, mur_ref, scr[0])
    xk = shift(pk_ref, qk_ref, muk_ref, scr[1])
    xv = shift(pv_ref, qv_ref, muv_ref, scr[2])
    xs = shift(ps_ref, qs_ref, mus_ref, scr[3])

    w = -_softplus(-(w0_ref[...] + _dot(jnp.tanh(xs), w2_ref[...]))) - 0.5
    ld_out[0] = -jnp.exp(w)
    a = _sigmoid(a0_ref[...] + _dot(xs, a2_ref[...]))
    g_out[0] = _dot(_sigmoid(xs), g2_ref[...])

    ones_bd = j_ref[...]
    kkr = xk * kk_ref[...]
    k2 = xk * (1.0 + (a - 1.0) * ka_ref[...])
    rkk = xr * k2 * rk_ref[...]
    for p in range(R_PAIRS):
        sl = slice(p * 128, (p + 1) * 128)
        kb = kkr[:, sl]
        nrm = jnp.sqrt(_segsum(kb * kb, ones_bd))
        kn = kb / jnp.maximum(nrm, 1e-12)
        kkn_out[0, :, sl] = kn
        b_out[0, :, sl] = kn * a[:, sl]
        bon_out[0, :, sl] = _segsum(rkk[:, sl], ones_bd) * xv[:, sl]
    r_out[0] = xr
    k_out[0] = k2
    v_out[0] = xv


def _rwkv_prep(proj3, small3, prev3, lw, ones_bd, *, tm):
    bsz, t, _ = proj3.shape
    explicit_prev = prev3.shape[1] == t
    tq = tm if explicit_prev else 1
    qmap = (lambda blk: (lambda bi, i: (bi, i, blk))) if explicit_prev else (
        lambda blk: (lambda bi, i: (bi, 0, blk)))
    small_blk = 3 * BRANCH_W // SMALL_W
    big = lambda blk: pl.BlockSpec((1, tm, BRANCH_W), lambda bi, i: (bi, i, blk))
    vec = lambda blk: pl.BlockSpec((1, BRANCH_W), lambda bi, i: (0, blk))
    full = lambda shape: pl.BlockSpec(shape, lambda bi, i: (0, 0))
    out = jax.ShapeDtypeStruct((bsz, t, BRANCH_W), F32)
    scratch = [] if explicit_prev else (
        [pltpu.VMEM((tm + 8, BRANCH_W), F32)] * 3 + [pltpu.VMEM((tm + 8, SMALL_W), F32)])
    return pl.pallas_call(
        functools.partial(_rwkv_prep_kernel, tm=tm, explicit_prev=explicit_prev),
        out_shape=[out] * 8,
        grid=(bsz, t // tm),
        in_specs=[big(0), big(1), big(2),
                  pl.BlockSpec((1, tm, SMALL_W), lambda bi, i: (bi, i, 0)),
                  pl.BlockSpec((1, tq, BRANCH_W), qmap(0)),
                  pl.BlockSpec((1, tq, BRANCH_W), qmap(1)),
                  pl.BlockSpec((1, tq, BRANCH_W), qmap(2)),
                  pl.BlockSpec((1, tq, SMALL_W), qmap(small_blk)),
                  vec(0), vec(1), vec(2),
                  pl.BlockSpec((1, SMALL_W), lambda bi, i: (0, small_blk)),
                  vec(0), vec(0), vec(0), vec(0), vec(0),
                  full((SMALL_W, BRANCH_W)), full((SMALL_W, BRANCH_W)), full((SMALL_W, BRANCH_W)),
                  full((128, 128))],
        out_specs=[pl.BlockSpec((1, tm, BRANCH_W), lambda bi, i: (bi, i, 0))] * 8,
        scratch_shapes=scratch,
        compiler_params=_cparams(2),
        name="rwkv_prep",
    )(proj3, proj3, proj3, small3, prev3, prev3, prev3, prev3,
      lw["mu_p"], lw["mu_p"], lw["mu_p"], lw["mu_p"],
      lw["w0"], lw["a0"], lw["k_k"], lw["k_a"], lw["r_k"],
      lw["w2p"], lw["a2p"], lw["g2p"], ones_bd)


def _rwkv_scan_kernel(r_ref, k_ref, v_ref, kk_ref, b_ref, ld_ref, g_ref, bon_ref,
                      lng_ref, lnb_ref, j_ref, s0_ref, *rest, L, group, n_earlier, single_step):
    earlier = rest[:n_earlier]
    o_ref, st_ref, s_ref = rest[n_earlier:]
    c_id = pl.program_id(1)
    bb = r_ref.shape[0]
    seqs = range(bb)

    @pl.when(c_id == 0)
    def _():
        z = jnp.zeros((R_HEAD, R_HEAD), F32)
        for bi in seqs:
            for p in range(R_PAIRS):
                top = jnp.concatenate([s0_ref[0, bi, 2 * p], z], axis=1)
                bot = jnp.concatenate([z, s0_ref[0, bi, 2 * p + 1]], axis=1)
                s_ref[bi * R_PAIRS + p] = jnp.concatenate([top, bot], axis=0)

    def rows(ref, bi):
        x = ref[bi]
        if x.shape[0] == L:
            return x
        assert x.shape[0] == 1
        return jnp.where(lax.broadcasted_iota(jnp.int32, (L, x.shape[1]), 0) == 0, x, 0.0)

    r_all = [rows(r_ref, bi) for bi in seqs]
    k_all = [rows(k_ref, bi) for bi in seqs]
    v_all = [rows(v_ref, bi) for bi in seqs]
    kk_all = [rows(kk_ref, bi) for bi in seqs]
    b_all = [rows(b_ref, bi) for bi in seqs]
    g_all = [rows(g_ref, bi) for bi in seqs]
    bon_all = [rows(bon_ref, bi) for bi in seqs]
    ld_all = [rows(ld_ref, bi) for bi in seqs]
    cs_all = [_cumsum_rows(x, single_step) for x in ld_all]
    ec_all = [jnp.exp(c) for c in cs_all]
    enc_all = [jnp.exp(-c) for c in cs_all]
    ecm_all = [jnp.exp(c - x) for c, x in zip(cs_all, ld_all)]
    c_last_all = [c[L - 1:L, :] for c in cs_all]
    e_tail_all = [jnp.exp(cl - c) for cl, c in zip(c_last_all, cs_all)]
    g_last_all = [jnp.exp(cl) for cl in c_last_all]

    lane = lax.broadcasted_iota(jnp.int32, (L, 128), 1)
    head_a = lane < R_HEAD

    def stack(x):
        return jnp.concatenate([jnp.where(head_a, x, 0.0), jnp.where(head_a, 0.0, x)], axis=0)

    P2 = 2 * L
    ri = lax.broadcasted_iota(jnp.int32, (P2, P2), 0)
    ci = lax.broadcasted_iota(jnp.int32, (P2, P2), 1)
    strict = ri > ci
    incl = ri >= ci
    eye = jnp.where(ri == ci, 1.0, 0.0)
    ones_bd = j_ref[...]
    inv_n = 1.0 / R_HEAD

    cat0 = lambda a, b: jnp.concatenate([a, b], axis=0)
    cat1 = lambda a, b: jnp.concatenate([a, b], axis=1)

    n_units = bb * R_PAIRS
    assert n_units % group == 0
    for g0 in range(0, n_units, group):
        units = list(range(g0, g0 + group))
        sq = [u // R_PAIRS for u in units]
        sls = [slice((u % R_PAIRS) * 128, (u % R_PAIRS + 1) * 128) for u in units]
        each = lambda f: [f(i) for i in range(group)]
        sel = lambda xs, i: xs[sq[i]][:, sls[i]]

        S = each(lambda i: s_ref[units[i]])
        Rs = each(lambda i: stack(sel(r_all, i) * sel(ec_all, i)))
        Bs = each(lambda i: stack(sel(kk_all, i) * sel(ecm_all, i)))
        Ks = each(lambda i: stack(sel(k_all, i) * sel(enc_all, i)))
        As = each(lambda i: stack(-(sel(b_all, i) * sel(enc_all, i))))
        Vs = each(lambda i: stack(sel(v_all, i)))
        Kt = each(lambda i: stack(sel(k_all, i) * sel(e_tail_all, i)))
        At = each(lambda i: stack(-(sel(b_all, i) * sel(e_tail_all, i))))

        if P2 % 128 == 0:
            sc = each(lambda i: _dot_nt(cat0(Bs[i], Rs[i]), cat0(As[i], Ks[i])))
            s_ba = each(lambda i: sc[i][:P2, :P2])
            s_bk = each(lambda i: sc[i][:P2, P2:])
            s_ra = each(lambda i: sc[i][P2:, :P2])
            s_rk = each(lambda i: sc[i][P2:, P2:])
        else:
            s_ba = each(lambda i: _dot_nt(Bs[i], As[i]))
            s_bk = each(lambda i: _dot_nt(Bs[i], Ks[i]))
            s_ra = each(lambda i: _dot_nt(Rs[i], As[i]))
            s_rk = each(lambda i: _dot_nt(Rs[i], Ks[i]))
        Nm = each(lambda i: jnp.where(strict, s_ba[i], 0.0))
        Mbk = each(lambda i: jnp.where(strict, s_bk[i], 0.0))
        Mra = each(lambda i: jnp.where(incl, s_ra[i], 0.0))
        Mrk = each(lambda i: jnp.where(incl, s_rk[i], 0.0))

        Tm = each(lambda i: eye + Nm[i])
        Pw = Nm
        span = 2
        while span < L and not single_step:
            Pw = [_dot(x, x) for x in Pw]
            Tm = each(lambda i: Tm[i] + _dot(Tm[i], Pw[i]))
            span *= 2

        mv = each(lambda i: _dot(cat0(Mbk[i], Mrk[i]), Vs[i]))
        tb = each(lambda i: _dot(Tm[i], cat1(Bs[i], mv[i][:P2])))
        mu = each(lambda i: _dot(Mra[i], tb[i]))
        Ro = each(lambda i: Rs[i] + mu[i][:, :128])
        uo = each(lambda i: _dot_nt(cat0(tb[i][:, :128], Ro[i]), S[i]))
        U = each(lambda i: uo[i][:P2] + tb[i][:, 128:])
        O = each(lambda i: uo[i][P2:] + mv[i][P2:] + mu[i][:, 128:])
        for i, u in enumerate(units):
            s_ref[u] = S[i] * sel(g_last_all, i) + _dot_tn(
                cat0(U[i], Vs[i]), cat0(At[i], Kt[i]))

        for i in range(group):
            sl = sls[i]
            out = O[i][:L] + O[i][L:]
            mean = _segsum(out, ones_bd) * inv_n
            d = out - mean
            var = _segsum(d * d, ones_bd) * inv_n
            y = d * lax.rsqrt(var + R_GN_EPS) * lng_ref[:, sl] + lnb_ref[:, sl]
            o_ref[sq[i], :, sl] = ((y + sel(bon_all, i)) * sel(g_all, i)).astype(BF16)

    @pl.when(c_id == pl.num_programs(1) - 1)
    def _():
        for bi in seqs:
            for p in range(R_PAIRS):
                sp = s_ref[bi * R_PAIRS + p]
                st_ref[n_earlier, bi, 2 * p] = sp[:R_HEAD, :R_HEAD]
                st_ref[n_earlier, bi, 2 * p + 1] = sp[R_HEAD:, R_HEAD:]
        for i, e_ref in enumerate(earlier):
            st_ref[i] = e_ref[0]


def _rwkv_scan(seqs, g, bonus, ln_g, ln_b, ones_bd, s0, l, earlier_s, *, L, bb):
    bsz, t, _ = seqs[0].shape
    tb = min(t, L)
    nc = -(-t // L)
    nl = len(earlier_s) + 1
    seq_spec = pl.BlockSpec((bb, tb, BRANCH_W), lambda bi, c: (bi, c, 0))
    vec_spec = pl.BlockSpec((1, BRANCH_W), lambda bi, c: (0, 0))
    st_ea = pl.BlockSpec((1, bb, R_HEADS, R_HEAD, R_HEAD), lambda bi, c: (0, bi, 0, 0, 0))
    return pl.pallas_call(
        functools.partial(_rwkv_scan_kernel, L=L, group=bb * R_PAIRS, n_earlier=nl - 1,
                          single_step=t == 1),
        out_shape=[jax.ShapeDtypeStruct((bsz, nc * L, BRANCH_W), BF16),
                   jax.ShapeDtypeStruct((nl, bsz, R_HEADS, R_HEAD, R_HEAD), F32)],
        grid=(bsz // bb, nc),
        in_specs=[seq_spec] * 8 + [
            vec_spec, vec_spec, pl.BlockSpec((128, 128), lambda bi, c: (0, 0)),
            pl.BlockSpec((1, bb, R_HEADS, R_HEAD, R_HEAD), lambda bi, c: (l, bi, 0, 0, 0))]
        + [st_ea] * (nl - 1),
        out_specs=[pl.BlockSpec((bb, L, BRANCH_W), lambda bi, c: (bi, c, 0)),
                   pl.BlockSpec((nl, bb, R_HEADS, R_HEAD, R_HEAD), lambda bi, c: (0, bi, 0, 0, 0))],
        scratch_shapes=[pltpu.VMEM((bb * R_PAIRS, 128, 128), F32)],
        compiler_params=_cparams(2),
        name="rwkv_scan",
    )(*seqs, g, bonus, ln_g, ln_b, ones_bd, s0, *earlier_s)


def _rwkv_mixer(proj3, small3, prev, s0, l, lw, ones_bd, earlier_s, *, L, tm, bb):
    bsz, t, _ = proj3.shape
    new_shift = jnp.concatenate(
        [proj3[:, -1, :3 * BRANCH_W].astype(F32), small3[:, -1, :R_COLS - 3 * BRANCH_W]], axis=-1)
    prev3 = jnp.pad(prev, ((0, 0), (0, 3 * BRANCH_W + SMALL_W - R_COLS)))[:, None, :]
    if t == 1:
        outs = _rwkv_prep(proj3.reshape(1, bsz, -1), small3.reshape(1, bsz, -1),
                          prev3.reshape(1, bsz, -1), lw, ones_bd, tm=tm)
        outs = [o.reshape(bsz, 1, BRANCH_W) for o in outs]
    else:
        outs = _rwkv_prep(proj3, small3, prev3, lw, ones_bd, tm=tm)
    r, k2, v, kkn, b, ld, g, bonus = outs
    o_r, s_new = _rwkv_scan((r, k2, v, kkn, b, ld), g, bonus, lw["ln_g"], lw["ln_b"],
                            ones_bd, s0, l, earlier_s, L=L, bb=bb)
    return o_r[:, :t].reshape(bsz * t, BRANCH_W), s_new, new_shift


def _mlstm_kernel(q_ref, k_ref, v_ref, o_ref, sm_ref, cq_ref, ck_ref, wq_ref, wk_ref,
                  bq_ref, bk_ref, ib_ref, fb_ref, ng_ref, c0_ref, n0_ref, m0_ref,
                  *rest, L, t_valid, n_earlier):
    earlier = rest[:n_earlier]
    out_ref, c_ref, n_ref, m_ref, qs_ref, ks_ref = rest[n_earlier:]
    last = n_earlier
    c_id = pl.program_id(1)

    @pl.when(c_id == 0)
    def _():
        for i, e_ref in enumerate(earlier):
            c_ref[i] = e_ref[0]
        c_ref[last] = c0_ref[0]
        n_ref[...] = n0_ref[0]
        m_ref[...] = m0_ref[0]
        qs_ref[5:8, :] = cq_ref[0, 0]
        ks_ref[5:8, :] = ck_ref[0, 0]

    qs_ref[8:8 + L, :] = _rows(q_ref, L)
    ks_ref[8:8 + L, :] = _rows(k_ref, L)
    conv_q = bq_ref[...]
    conv_k = bk_ref[...]
    for j in range(CONV_W):
        conv_q = conv_q + qs_ref[5 + j:5 + j + L, :] * wq_ref[j:j + 1, :]
        conv_k = conv_k + ks_ref[5 + j:5 + j + L, :] * wk_ref[j:j + 1, :]
    if L >= CONV_W - 1:
        tail_q = qs_ref[5 + L:8 + L, :]
        tail_k = ks_ref[5 + L:8 + L, :]
        qs_ref[5:8, :] = tail_q
        ks_ref[5:8, :] = tail_k

    row = lax.broadcasted_iota(jnp.int32, (L, 1), 0)
    valid = (c_id * L + row) < t_valid
    q_all = jnp.where(valid, _silu(conv_q), 0.0)
    k_all = jnp.where(valid, _silu(conv_k) * (M_DK ** -0.5), 0.0)
    v_all = jnp.where(valid, _rows(v_ref, L), 0.0)
    gate_o = _sigmoid(_rows(o_ref, L))

    sm = _rows(sm_ref, L)
    lane = lax.broadcasted_iota(jnp.int32, (L, SMALL_W), 1)
    head_lane = lane < M_HEADS
    i_pre = jnp.where(head_lane, pltpu.roll(sm, SMALL_W - S_I, axis=1), 0.0)
    f_pre = jnp.where(head_lane, pltpu.roll(sm, SMALL_W - S_F, axis=1), 0.0)
    ig4 = jnp.where(valid & head_lane, i_pre + ib_ref[...], NEG)
    lf4 = jnp.where(valid & head_lane, _log_sigmoid(f_pre + fb_ref[...]), 0.0)

    ti = lax.broadcasted_iota(jnp.int32, (L, L), 0)
    si = lax.broadcasted_iota(jnp.int32, (L, L), 1)
    causal = si <= ti
    diag = ti == si
    ones_l = jnp.ones((L, L), F32)
    F4 = _cumsum_rows(lf4, t_valid == 1)
    gmf4 = ig4 - F4

    heads = range(M_HEADS)
    each = lambda f: [f(h) for h in heads]
    sls = [slice(h * M_DK, (h + 1) * M_DK) for h in heads]
    rowsum = lambda x: jnp.sum(x, axis=-1, keepdims=True)
    q = each(lambda h: q_all[:, sls[h]])
    k = each(lambda h: k_all[:, sls[h]])
    v = each(lambda h: v_all[:, sls[h]])
    F = each(lambda h: rowsum(jnp.where(lane == h, F4, 0.0)))
    ig = each(lambda h: rowsum(jnp.where(lane == h, ig4, 0.0)))
    gmf = each(lambda h: rowsum(jnp.where(lane == h, gmf4, 0.0)))
    if t_valid == 1:
        g_row = each(lambda h: jnp.where(si == 0, gmf[h][0:1, :], NEG))
    else:
        g_row = each(lambda h: _dot_hi(
            ones_l, jnp.where(diag, jnp.broadcast_to(gmf[h], (L, L)), 0.0)))
    Dm = each(lambda h: jnp.where(causal, F[h] + g_row[h], NEG))

    C = each(lambda h: c_ref[last, 0, h])
    n = each(lambda h: n_ref[0, h])
    m_prev = each(lambda h: m_ref[0, h])
    inter = each(lambda h: F[h] + m_prev[h])
    m_t = each(lambda h: jnp.maximum(inter[h], jnp.max(Dm[h], axis=-1, keepdims=True)))
    w_inter = each(lambda h: jnp.exp(inter[h] - m_t[h]))
    Sm = each(lambda h: _dot_nt(q[h], k[h]) * jnp.exp(Dm[h] - m_t[h]))
    num = each(lambda h: w_inter[h] * _dot(q[h], C[h]) + _dot(Sm[h], v[h]))
    den = each(lambda h: w_inter[h] * rowsum(q[h] * n[h]) + rowsum(Sm[h]))
    hh = each(lambda h: num[h] / jnp.maximum(jnp.abs(den[h]), jnp.exp(-m_t[h])))

    FL = each(lambda h: F[h][L - 1:L, :])
    g_s = each(lambda h: FL[h] - F[h] + ig[h])
    m_new = each(lambda h: jnp.maximum(FL[h] + m_prev[h], jnp.max(g_s[h], axis=0, keepdims=True)))
    a_c = each(lambda h: jnp.exp(FL[h] + m_prev[h] - m_new[h]))
    kw = each(lambda h: k[h] * jnp.exp(g_s[h] - m_new[h]))
    for h in heads:
        c_ref[last, 0, h] = a_c[h] * C[h] + _dot_tn(kw[h], v[h])
        n_ref[0, h] = a_c[h] * n[h] + jnp.sum(kw[h], axis=0, keepdims=True)
        m_ref[0, h] = m_new[h]
        hn = hh[h] * lax.rsqrt(jnp.mean(hh[h] * hh[h], axis=-1, keepdims=True) + EPS)
        out_ref[0, :, sls[h]] = (gate_o[:, sls[h]] * hn * ng_ref[:, sls[h]]).astype(BF16)


def _rows(ref, L):
    x = ref[0].astype(F32)
    if x.shape[0] == L:
        return x
    assert x.shape[0] == 1
    row = lax.broadcasted_iota(jnp.int32, (L, x.shape[1]), 0)
    return jnp.where(row == 0, x, 0.0)


def _mlstm_mixer(proj3, small3, conv_buf, c0, n0, m0, l, lw, earlier_c, *, L):
    bsz, t, _ = proj3.shape
    tb = min(t, L)
    nc = -(-t // L)
    nl = len(earlier_c) + 1
    seq = lambda col: pl.BlockSpec((1, tb, BRANCH_W), lambda bi, c: (bi, c, col // BRANCH_W))
    st_c = pl.BlockSpec((nl, 1, M_HEADS, M_DK, M_DK), lambda bi, c: (0, bi, 0, 0, 0))
    ea_c = pl.BlockSpec((1, 1, M_HEADS, M_DK, M_DK), lambda bi, c: (0, bi, 0, 0, 0))
    st_n = pl.BlockSpec((1, M_HEADS, 1, M_DK), lambda bi, c: (bi, 0, 0, 0))
    st_m = pl.BlockSpec((1, M_HEADS, 1, 1), lambda bi, c: (bi, 0, 0, 0))
    in_c = pl.BlockSpec((1, 1, M_HEADS, M_DK, M_DK), lambda bi, c: (l, bi, 0, 0, 0))
    in_n = pl.BlockSpec((1, 1, M_HEADS, 1, M_DK), lambda bi, c: (l, bi, 0, 0, 0))
    in_m = pl.BlockSpec((1, 1, M_HEADS, 1, 1), lambda bi, c: (l, bi, 0, 0, 0))
    conv = lambda blk: pl.BlockSpec((1, 1, CONV_W - 1, BRANCH_W), lambda bi, c: (l, bi, 0, blk))
    cw = lambda blk: pl.BlockSpec((CONV_W, BRANCH_W), lambda bi, c: (0, blk))
    vec = lambda blk: pl.BlockSpec((1, BRANCH_W), lambda bi, c: (0, blk))
    hb = pl.BlockSpec((1, SMALL_W), lambda bi, c: (0, 0))
    pad_heads = lambda a: jnp.pad(a, ((0, 0), (0, SMALL_W - M_HEADS)))
    out, c_new, n_new, m_new = pl.pallas_call(
        functools.partial(_mlstm_kernel, L=L, t_valid=t, n_earlier=nl - 1),
        out_shape=[jax.ShapeDtypeStruct((bsz, nc * L, BRANCH_W), BF16),
                   jax.ShapeDtypeStruct((nl, bsz, M_HEADS, M_DK, M_DK), F32),
                   jax.ShapeDtypeStruct((bsz, M_HEADS, 1, M_DK), F32),
                   jax.ShapeDtypeStruct((bsz, M_HEADS, 1, 1), F32)],
        grid=(bsz, nc),
        in_specs=[seq(C_MQK), seq(C_MQK + BRANCH_W), seq(C_MV), seq(C_MO),
                  pl.BlockSpec((1, tb, SMALL_W), lambda bi, c: (bi, c, 0)),
                  conv(0), conv(1), cw(0), cw(1), vec(0), vec(1), hb, hb, vec(0),
                  in_c, in_n, in_m] + [ea_c] * (nl - 1),
        out_specs=[pl.BlockSpec((1, L, BRANCH_W), lambda bi, c: (bi, c, 0)), st_c, st_n, st_m],
        scratch_shapes=[pltpu.VMEM((L + 8, BRANCH_W), F32), pltpu.VMEM((L + 8, BRANCH_W), F32)],
        compiler_params=_cparams(2),
        name="mlstm_scan",
    )(proj3, proj3, proj3, proj3, small3, conv_buf, conv_buf,
      lw["conv_w"], lw["conv_w"], lw["conv_b"], lw["conv_b"],
      pad_heads(lw["i_b"]), pad_heads(lw["f_b"]),
      lw["m_norm_g"], c0, n0.reshape(DEPTH, bsz, M_HEADS, 1, M_DK),
      m0.reshape(DEPTH, bsz, M_HEADS, 1, 1), *earlier_c)
    return out, c_new, n_new.reshape(bsz, M_HEADS, M_DK), m_new.reshape(bsz, M_HEADS)


def _gla_kernel(q_ref, k_ref, v_ref, og_ref, sm_ref, a2_ref, ab_ref, ng_ref, s0_ref,
                *rest, L, t_valid, n_earlier):
    earlier = rest[:n_earlier]
    out_ref, s_ref, b_scr, q_scr = rest[n_earlier:]
    last = n_earlier
    c_id = pl.program_id(1)

    @pl.when(c_id == 0)
    def _():
        for i, e_ref in enumerate(earlier):
            s_ref[i] = e_ref[0]
        s_ref[last] = s0_ref[0]

    row = lax.broadcasted_iota(jnp.int32, (L, 1), 0)
    valid = (c_id * L + row) < t_valid
    q_all = jnp.where(valid, _rows(q_ref, L) * (G_DK ** -0.5), 0.0)
    k_all = jnp.where(valid, _rows(k_ref, L), 0.0)
    v_all = jnp.where(valid, _rows(v_ref, L), 0.0)
    gate_o = _silu(_rows(og_ref, L))
    lg = _log_sigmoid(_dot(_rows(sm_ref, L), a2_ref[...]) + ab_ref[...]) * (1.0 / G_GATE_NORM)
    lg = jnp.where(valid, lg, 0.0)

    ti = lax.broadcasted_iota(jnp.int32, (L, L), 0)
    si = lax.broadcasted_iota(jnp.int32, (L, L), 1)
    b_all = _cumsum_rows(lg, t_valid == 1)
    b_scr[...] = b_all
    q_scr[...] = q_all
    eb_all = jnp.exp(b_all)
    b_last_all = b_all[L - 1:L, :]
    e_tail_all = jnp.exp(b_last_all - b_all)
    ones_lv = jnp.ones((L, G_DV), F32)

    n_t = L if t_valid >= L else t_valid
    SB = GLA_SUB
    s_col = {rows: lax.broadcasted_iota(jnp.int32, (rows, 1), 0) for rows in (8, SB)}
    t_lane = {rows: lax.broadcasted_iota(jnp.int32, (rows, SB), 1) for rows in (8, SB)}

    heads = range(G_HEADS)
    sls = [slice(h * G_DK, (h + 1) * G_DK) for h in heads]
    svs = [slice(h * G_DV, (h + 1) * G_DV) for h in heads]

    o_parts = [[] for _ in heads]
    for r0 in range(0, L, SB):
        n_sub = max(0, min(SB, n_t - r0))
        for h in heads:
            sl, sv = sls[h], svs[h]
            ki, bi, vi = k_all[r0:r0 + SB, sl], b_all[r0:r0 + SB, sl], v_all[r0:r0 + SB, sv]
            att = jnp.zeros((SB, SB), F32)
            for tl in range(n_sub):
                rows = 8 * (tl // 8 + 1)
                bt = b_scr[r0 + tl:r0 + tl + 1, sl]
                qt = q_scr[r0 + tl:r0 + tl + 1, sl]
                e = jnp.exp(jnp.where(s_col[rows] <= tl, bt - bi[:rows], NEG))
                col = jnp.sum(qt * ki[:rows] * e, axis=-1, keepdims=True)
                top = jnp.where(t_lane[rows] == tl, col, att[:rows])
                att = top if rows == SB else jnp.concatenate([top, att[rows:]], axis=0)
            o_i = _dot_tn(att, vi)
            if r0 > 0 and n_sub > 0:
                ref = b_scr[r0 - 1:r0, sl]
                a_off = _dot_nt(q_all[r0:r0 + SB, sl] * jnp.exp(bi - ref),
                                k_all[:r0, sl] * jnp.exp(ref - b_all[:r0, sl]))
                o_i = o_i + _dot(a_off, v_all[:r0, sv])
            o_parts[h].append(o_i)

    for h in heads:
        sl, sv = sls[h], svs[h]
        q, k, v = q_all[:, sl], k_all[:, sl], v_all[:, sv]
        o_intra = o_parts[h][0] if len(o_parts[h]) == 1 else jnp.concatenate(o_parts[h], axis=0)

        S = s_ref[last, 0, h]
        o = _dot(q * eb_all[:, sl], S) + o_intra
        decay = jnp.exp(lax.dot_general(lg[:, sl], ones_lv, (((0,), (0,)), ((), ())),
                                        precision=lax.Precision.HIGHEST,
                                        preferred_element_type=F32))
        s_ref[last, 0, h] = S * decay + _dot_tn(k * e_tail_all[:, sl], v)

        on = o * lax.rsqrt(jnp.mean(o * o, axis=-1, keepdims=True) + EPS) * ng_ref[:, sv]
        out_ref[0, :, sv] = (on * gate_o[:, sv]).astype(BF16)


def _gla_mixer(proj3, small3, s0, l, lw, earlier_s, *, L):
    bsz, t, _ = proj3.shape
    tb = min(t, L)
    nc = -(-t // L)
    nl = len(earlier_s) + 1
    gw = G_HEADS * G_DK
    st = pl.BlockSpec((nl, 1, G_HEADS, G_DK, G_DV), lambda bi, c: (0, bi, 0, 0, 0))
    st_ea = pl.BlockSpec((1, 1, G_HEADS, G_DK, G_DV), lambda bi, c: (0, bi, 0, 0, 0))
    st_in = pl.BlockSpec((1, 1, G_HEADS, G_DK, G_DV), lambda bi, c: (l, bi, 0, 0, 0))
    return pl.pallas_call(
        functools.partial(_gla_kernel, L=L, t_valid=t, n_earlier=nl - 1),
        out_shape=[jax.ShapeDtypeStruct((bsz, nc * L, BRANCH_W), BF16),
                   jax.ShapeDtypeStruct((nl, bsz, G_HEADS, G_DK, G_DV), F32)],
        grid=(bsz, nc),
        in_specs=[pl.BlockSpec((1, tb, gw), lambda bi, c: (bi, c, C_GQ // gw)),
                  pl.BlockSpec((1, tb, gw), lambda bi, c: (bi, c, C_GQ // gw + 1)),
                  pl.BlockSpec((1, tb, BRANCH_W), lambda bi, c: (bi, c, C_GV // BRANCH_W)),
                  pl.BlockSpec((1, tb, BRANCH_W), lambda bi, c: (bi, c, C_GOG // BRANCH_W)),
                  pl.BlockSpec((1, tb, SMALL_W), lambda bi, c: (bi, c, 0)),
                  pl.BlockSpec((SMALL_W, gw), lambda bi, c: (0, 0)),
                  pl.BlockSpec((1, gw), lambda bi, c: (0, 0)),
                  pl.BlockSpec((1, BRANCH_W), lambda bi, c: (0, 0)),
                  st_in] + [st_ea] * (nl - 1),
        out_specs=[pl.BlockSpec((1, L, BRANCH_W), lambda bi, c: (bi, c, 0)), st],
        scratch_shapes=[pltpu.VMEM((L, gw), F32), pltpu.VMEM((L, gw), F32)],
        compiler_params=_cparams(2),
        name="gla_scan",
    )(proj3, proj3, proj3, proj3, small3, lw["g_a2p"], lw["g_a_b"], lw["g_norm_g"], s0,
      *earlier_s)


def _small_group_rows(wt):
    depth, _, k = wt.shape
    used = (R_COLS - 3072) + 2 * M_HEADS + G_LR
    return jnp.concatenate(
        [wt[:, W_R0 + 3072:W_R0 + 3264], wt[:, W_M0 + 3072:W_M0 + 3080],
         wt[:, W_G0 + 2048:W_G0 + 2064], jnp.zeros((depth, MAIN_TN - used, k), wt.dtype)], axis=1)


def _rows_padded(w, row0, total):
    return jnp.pad(w, ((row0, total - row0 - w.shape[0]), (0, 0)))


def _layer_weights(l, P):
    mu = P["rwkv_mu"][l]
    mu_p = jnp.concatenate([mu, jnp.zeros((3 * BRANCH_W + SMALL_W - R_COLS,), F32)]).reshape(1, -1)
    row = lambda a: a.reshape(1, -1)
    return {
        "norm1_g": P["norm1_g"][l], "gate_b": P["gate_b"][l],
        "mu_p": mu_p, "w0": row(P["rwkv_w0"][l]), "a0": row(P["rwkv_a0"][l]),
        "k_k": row(P["rwkv_k_k"][l]), "k_a": row(P["rwkv_k_a"][l]), "r_k": row(P["rwkv_r_k"][l]),
        "w2p": _rows_padded(P["rwkv_w2"][l], 0, SMALL_W),
        "a2p": _rows_padded(P["rwkv_a2"][l], R_LORA, SMALL_W),
        "g2p": _rows_padded(P["rwkv_g2"][l], 2 * R_LORA, SMALL_W),
        "ln_g": row(P["rwkv_ln_g"][l]), "ln_b": row(P["rwkv_ln_b"][l]),
        "conv_w": P["mlstm_conv_w"][l], "conv_b": row(P["mlstm_conv_b"][l]),
        "i_b": row(P["mlstm_i_b"][l]), "f_b": row(P["mlstm_f_b"][l]),
        "m_norm_g": row(P["mlstm_norm_g"][l]),
        "g_a2p": _rows_padded(P["gla_a2"][l], S_GXA, SMALL_W), "g_a_b": row(P["gla_a_b"][l]),
        "g_norm_g": row(P["gla_norm_g"][l]),
        "norm2_g": P["norm2_g"][l],
    }


def _layer(x2, bsz, t, states, l, lw, big, ones_bd, cfg, earlier):
    ea_wkv, ea_c, ea_s = earlier
    rw_prev, rw_s, m_conv, m_c, m_n, m_m, g_s = states
    m = bsz * t
    L, tm = cfg["L"], cfg["tm"]
    proj, small = _rms_matmul(x2, lw["norm1_g"], big["w_in_t"], big["w_in_t_small"], l, tm=tm,
                              main_dtype=cfg["proj_dtype"])
    proj3 = proj.reshape(bsz, t, C_SMALL)
    small3 = small.reshape(bsz, t, -1)

    o_r, rw_s_new, rw_prev_new = _rwkv_mixer(proj3, small3, rw_prev[l], rw_s, l, lw, ones_bd,
                                             ea_wkv, L=cfg["L_rwkv"], tm=cfg["tm_prep"],
                                             bb=cfg["bb_rwkv"])

    o_m, m_c_new, m_n_new, m_m_new = _mlstm_mixer(proj3, small3, m_conv, m_c, m_n, m_m, l, lw,
                                                  ea_c, L=L)
    qk_tail = proj3[:, -min(t, CONV_W - 1):, C_MQK:C_MQK + 2 * BRANCH_W].astype(F32)
    m_conv_new = jnp.concatenate([m_conv[l], qk_tail], axis=1)[:, -(CONV_W - 1):]
    o_g, g_s_new = _gla_mixer(proj3, small3, g_s, l, lw, ea_s, L=L)
    o_m = o_m[:, :t].reshape(m, BRANCH_W)
    o_g = o_g[:, :t].reshape(m, BRANCH_W)

    merged = _merge(o_r, o_m, o_g, big["w_branch"], l, proj, lw["gate_b"],
                    tm=cfg["tm_merge"], tn=512)
    x2 = _matmul_residual(merged, big["w_out"], l, x2, tm=tm, tn=512, tk=D_MODEL)
    hidden = _rms_swiglu(x2, lw["norm2_g"], big["w_gu"], l, tm=tm, tn=512)
    x2 = _matmul_residual(hidden, big["w_down"], l, x2, tm=tm, tn=512, tk=2816)
    return x2, (rw_prev_new, rw_s_new, m_conv_new, m_c_new, m_n_new, m_m_new, g_s_new)


def _trunk(x, states, layer_ws, big, final_g, ones_bd, cfg):
    bsz, t, d = x.shape
    x2 = x.reshape(bsz * t, d)
    per_layer = []
    for l in range(DEPTH):
        is_last = l == DEPTH - 1
        earlier = tuple([st[i] for st in per_layer] if is_last else [] for i in BIG_STATES)
        x2, new = _layer(x2, bsz, t, states, l, layer_ws[l], big, ones_bd, cfg, earlier)
        per_layer.append(new)
    new_states = [per_layer[-1][i] if i in BIG_STATES
                  else jnp.stack([st[i] for st in per_layer], axis=0) for i in range(len(states))]
    y = _rmsnorm(x2, final_g, tm=cfg["tm_norm"]).reshape(bsz, t, d)
    return y, new_states


BIG_STATES = (1, 3, 6)

PROMPT_CFG = dict(L=64, L_rwkv=64, bb_rwkv=1, tm=1024, tm_prep=256, tm_merge=1024, tm_norm=512, proj_dtype=BF16)
SAMPLE_CFG = dict(L=16, L_rwkv=16, bb_rwkv=4, tm=128, tm_prep=128, tm_merge=128, tm_norm=128, proj_dtype=F32)


def _zero_states(bsz):
    return (jnp.zeros((DEPTH, bsz, R_COLS), F32),
            jnp.zeros((DEPTH, bsz, R_HEADS, R_HEAD, R_HEAD), F32),
            jnp.zeros((DEPTH, bsz, CONV_W - 1, 2 * BRANCH_W), F32),
            jnp.zeros((DEPTH, bsz, M_HEADS, M_DK, M_DK), F32),
            jnp.zeros((DEPTH, bsz, M_HEADS, M_DK), F32),
            jnp.zeros((DEPTH, bsz, M_HEADS), F32),
            jnp.zeros((DEPTH, bsz, G_HEADS, G_DK, G_DV), F32))


def kernel(x_prompt, x_sample, state_rwkv_shift, state_rwkv_wkv, state_mlstm_conv, state_mlstm_C, state_mlstm_n, state_mlstm_m, state_gla_S, norm1_g, w_in, gate_b, rwkv_mu, rwkv_w0, rwkv_w2, rwkv_a0, rwkv_a2, rwkv_g2, rwkv_k_k, rwkv_k_a, rwkv_r_k, rwkv_ln_g, rwkv_ln_b, mlstm_conv_w, mlstm_conv_b, mlstm_i_b, mlstm_f_b, mlstm_norm_g, gla_a2, gla_a_b, gla_norm_g, w_branch, w_out, norm2_g, ffn_w_gu, ffn_w_down, final_norm_g):
    P = dict(norm1_g=norm1_g, w_in=w_in, gate_b=gate_b, rwkv_mu=rwkv_mu, rwkv_w0=rwkv_w0,
             rwkv_w2=rwkv_w2, rwkv_a0=rwkv_a0, rwkv_a2=rwkv_a2, rwkv_g2=rwkv_g2,
             rwkv_k_k=rwkv_k_k, rwkv_k_a=rwkv_k_a, rwkv_r_k=rwkv_r_k, rwkv_ln_g=rwkv_ln_g,
             rwkv_ln_b=rwkv_ln_b, mlstm_conv_w=mlstm_conv_w, mlstm_conv_b=mlstm_conv_b,
             mlstm_i_b=mlstm_i_b, mlstm_f_b=mlstm_f_b, mlstm_norm_g=mlstm_norm_g,
             gla_a2=gla_a2, gla_a_b=gla_a_b, gla_norm_g=gla_norm_g, w_branch=w_branch,
             w_out=w_out, norm2_g=norm2_g, ffn_w_gu=ffn_w_gu, ffn_w_down=ffn_w_down)
    layer_ws = [_layer_weights(l, P) for l in range(DEPTH)]
    w_in_t = jnp.swapaxes(w_in, 1, 2)
    big = dict(w_in_t=w_in_t, w_in_t_small=_small_group_rows(w_in_t), w_branch=w_branch,
               w_out=w_out, w_gu=ffn_w_gu, w_down=ffn_w_down)
    head_of_lane = jnp.arange(128) // R_HEAD
    ones_bd = (head_of_lane[:, None] == head_of_lane[None, :]).astype(BF16)

    y_p, p_states = _trunk(x_prompt, _zero_states(x_prompt.shape[0]), layer_ws, big,
                           final_norm_g, ones_bd, PROMPT_CFG)
    s_states = (state_rwkv_shift, state_rwkv_wkv, state_mlstm_conv, state_mlstm_C,
                state_mlstm_n, state_mlstm_m, state_gla_S)
    y_s, s_states = _trunk(x_sample, s_states, layer_ws, big, final_norm_g, ones_bd, SAMPLE_CFG)
    return (y_p, y_s, *p_states, *s_states)
```

```python
import functools

import jax
import jax.numpy as jnp
from jax import lax
from jax.experimental import pallas as pl
from jax.experimental.pallas import tpu as pltpu

F32 = jnp.float32
BF16 = jnp.bfloat16

D_MODEL = 2048
DEPTH = 2
BRANCH_W = 1024
R_HEADS, R_HEAD = 16, 64
R_PAIRS = R_HEADS // 2
R_LORA = 64
R_COLS = 3 * BRANCH_W + 3 * R_LORA
R_GN_EPS = 64e-5
M_HEADS, M_DK = 4, 256
CONV_W = 4
G_HEADS, G_DK, G_DV = 4, 128, 256
G_LR = 16
G_GATE_NORM = 16.0
D_FF = 5632
EPS = 1e-6
NEG = -1e30
GLA_SUB = 16

C_RWKV = 0
C_MQK = 3072
C_MV = 5120
C_MO = 6144
C_GQ = 7168
C_GV = 8192
C_GOG = 9216
C_GATE = 10240
C_SMALL = 16384
SMALL_W = 256
S_I, S_F, S_GXA = 192, 196, 200
MAIN_TN = 512
W_R0, W_M0, W_G0, W_T0 = 0, R_COLS, R_COLS + 4104, R_COLS + 4104 + 3088
MAIN_RUNS = ((C_RWKV, W_R0, 3072), (C_MQK, W_M0, 3072), (C_MO, W_M0 + 3080, 1024),
             (C_GQ, W_G0, 2048), (C_GOG, W_G0 + 2064, 1024 + 3 * D_MODEL))

VMEM_LIMIT = 56 * 1024 * 1024


def _cparams(n_axes):
    return pltpu.CompilerParams(dimension_semantics=("arbitrary",) * n_axes,
                                vmem_limit_bytes=VMEM_LIMIT)


def _dot(a, b):
    return jnp.dot(a.astype(BF16), b.astype(BF16), preferred_element_type=F32)


def _dot_nt(a, b):
    return lax.dot_general(a.astype(BF16), b.astype(BF16), (((1,), (1,)), ((), ())),
                           preferred_element_type=F32)


def _dot_tn(a, b):
    return lax.dot_general(a.astype(BF16), b.astype(BF16), (((0,), (0,)), ((), ())),
                           preferred_element_type=F32)


def _dot_hi(a, b):
    return jnp.dot(a, b, precision=lax.Precision.HIGHEST, preferred_element_type=F32)


def _cumsum_rows(x, single_step):
    n = x.shape[0]
    if single_step:
        row = lax.broadcasted_iota(jnp.int32, x.shape, 0)
        return jnp.where(row == 0, x, x[0:1, :])
    ti = lax.broadcasted_iota(jnp.int32, (n, n), 0)
    si = lax.broadcasted_iota(jnp.int32, (n, n), 1)
    return _dot_hi((si <= ti).astype(F32), x)


def _segsum(y, ones_blockdiag):
    hi = y.astype(BF16)
    lo = (y - hi.astype(F32)).astype(BF16)
    return (jnp.dot(hi, ones_blockdiag, preferred_element_type=F32)
            + jnp.dot(lo, ones_blockdiag, preferred_element_type=F32))


def _sigmoid(x):
    return 1.0 / (1.0 + jnp.exp(-x))


def _silu(x):
    return x * _sigmoid(x)


def _log_sigmoid(x):
    return -_softplus(-x)


def _softplus(x):
    return jnp.maximum(x, 0.0) + jnp.log(1.0 + jnp.exp(-jnp.abs(x)))


def _rms_mm_kernel(start_ref, x_ref, g_ref, wt_ref, wsm_ref, o_ref, osm_ref, h_scr, *, n_main):
    del start_ref
    j = pl.program_id(1)

    @pl.when(j == 0)
    def _():
        x = x_ref[...]
        ms = jnp.mean(x * x, axis=-1, keepdims=True)
        h_scr[...] = (x * lax.rsqrt(ms + EPS) * g_ref[...]).astype(BF16)

    @pl.when(j < n_main)
    def _():
        o_ref[...] = _dot_nt(h_scr[...], wt_ref[0]).astype(o_ref.dtype)

    @pl.when(j >= n_main)
    def _():
        osm_ref[...] = _dot_nt(h_scr[...], wsm_ref[0])


def _rms_matmul(x, g, wt, wt_small, l, *, tm, main_dtype):
    m, k = x.shape
    n_in = wt.shape[1]
    tn = MAIN_TN
    starts = []
    for p0, s0, width in MAIN_RUNS:
        assert p0 == len(starts) * tn and width % tn == 0 and s0 % 8 == 0
        starts += list(range(s0, s0 + width, tn))
    n_main = len(starts)
    assert n_main * tn == C_SMALL and starts[-1] + tn == n_in
    grid_spec = pltpu.PrefetchScalarGridSpec(
        num_scalar_prefetch=1,
        grid=(m // tm, n_main + 1),
        in_specs=[pl.BlockSpec((tm, k), lambda i, j, st: (i, 0)),
                  pl.BlockSpec((1, k), lambda i, j, st: (0, 0)),
                  pl.BlockSpec((pl.Element(1), pl.Element(tn), pl.Element(k)),
                               lambda i, j, st: (l, st[j] * 8, 0)),
                  pl.BlockSpec((1, tn, k), lambda i, j, st: (l, 0, 0))],
        out_specs=[pl.BlockSpec((tm, tn), lambda i, j, st: (i, jnp.minimum(j, n_main - 1))),
                   pl.BlockSpec((tm, tn), lambda i, j, st: (i, 0))],
        scratch_shapes=[pltpu.VMEM((tm, k), BF16)],
    )
    return pl.pallas_call(
        functools.partial(_rms_mm_kernel, n_main=n_main),
        out_shape=[jax.ShapeDtypeStruct((m, C_SMALL), main_dtype),
                   jax.ShapeDtypeStruct((m, tn), F32)],
        grid_spec=grid_spec,
        compiler_params=_cparams(2),
        name="rms_in_proj",
    )(jnp.asarray([s // 8 for s in starts] + [0], jnp.int32), x, g.reshape(1, k), wt, wt_small)


def _rms_swiglu_kernel(x_ref, g_ref, wg_ref, wu_ref, o_ref, h_scr):
    @pl.when(pl.program_id(1) == 0)
    def _():
        x = x_ref[...]
        ms = jnp.mean(x * x, axis=-1, keepdims=True)
        h_scr[...] = (x * lax.rsqrt(ms + EPS) * g_ref[...]).astype(BF16)

    h = h_scr[...]
    gg = jnp.dot(h, wg_ref[0].astype(BF16), preferred_element_type=F32)
    uu = jnp.dot(h, wu_ref[0].astype(BF16), preferred_element_type=F32)
    o_ref[...] = (_silu(gg) * uu).astype(BF16)


def _rms_swiglu(x, g, w_gu, l, *, tm, tn):
    m, k = x.shape
    nj = D_FF // tn
    return pl.pallas_call(
        _rms_swiglu_kernel,
        out_shape=jax.ShapeDtypeStruct((m, D_FF), BF16),
        grid=(m // tm, nj),
        in_specs=[pl.BlockSpec((tm, k), lambda i, j: (i, 0)),
                  pl.BlockSpec((1, k), lambda i, j: (0, 0)),
                  pl.BlockSpec((1, k, tn), lambda i, j: (l, 0, j)),
                  pl.BlockSpec((1, k, tn), lambda i, j: (l, 0, j + nj))],
        out_specs=pl.BlockSpec((tm, tn), lambda i, j: (i, j)),
        scratch_shapes=[pltpu.VMEM((tm, k), BF16)],
        compiler_params=_cparams(2),
        name="rms_ffn_swiglu",
    )(x, g.reshape(1, k), w_gu, w_gu)


def _mm_res_kernel(a_ref, w_ref, res_ref, o_ref, acc_ref, *, nk):
    kk = pl.program_id(2)

    @pl.when(kk == 0)
    def _():
        acc_ref[...] = jnp.zeros_like(acc_ref)

    acc_ref[...] += jnp.dot(a_ref[...], w_ref[0].astype(BF16), preferred_element_type=F32)

    @pl.when(kk == nk - 1)
    def _():
        o_ref[...] = acc_ref[...] + res_ref[...]


def _matmul_residual(a, w, l, res, *, tm, tn, tk):
    m, k = a.shape
    n = w.shape[2]
    nk = k // tk
    return pl.pallas_call(
        functools.partial(_mm_res_kernel, nk=nk),
        out_shape=jax.ShapeDtypeStruct((m, n), F32),
        grid=(m // tm, n // tn, nk),
        in_specs=[pl.BlockSpec((tm, tk), lambda i, j, kk: (i, kk)),
                  pl.BlockSpec((1, tk, tn), lambda i, j, kk: (l, kk, j)),
                  pl.BlockSpec((tm, tn), lambda i, j, kk: (i, j))],
        out_specs=pl.BlockSpec((tm, tn), lambda i, j, kk: (i, j)),
        scratch_shapes=[pltpu.VMEM((tm, tn), F32)],
        compiler_params=_cparams(3),
        name="matmul_residual",
    )(a, w, res)


def _merge_kernel(or_ref, om_ref, og_ref, wr_ref, wm_ref, wg_ref,
                  gr_ref, gm_ref, gg_ref, br_ref, bm_ref, bg_ref, o_ref):
    acc = _sigmoid(gr_ref[...].astype(F32) + br_ref[0]) * jnp.dot(
        or_ref[...], wr_ref[0, 0].astype(BF16), preferred_element_type=F32)
    acc += _sigmoid(gm_ref[...].astype(F32) + bm_ref[0]) * jnp.dot(
        om_ref[...], wm_ref[0, 0].astype(BF16), preferred_element_type=F32)
    acc += _sigmoid(gg_ref[...].astype(F32) + bg_ref[0]) * jnp.dot(
        og_ref[...], wg_ref[0, 0].astype(BF16), preferred_element_type=F32)
    o_ref[...] = acc.astype(BF16)


def _merge(o_r, o_m, o_g, w_branch, l, proj, gate_b, *, tm, tn):
    m = o_r.shape[0]
    gate_blk = C_GATE // tn
    per = D_MODEL // tn
    o_spec = pl.BlockSpec((tm, BRANCH_W), lambda i, j: (i, 0))

    def w_spec(b):
        return pl.BlockSpec((1, 1, BRANCH_W, tn), lambda i, j: (l, b, 0, j))

    def g_spec(b):
        return pl.BlockSpec((tm, tn), lambda i, j: (i, gate_blk + b * per + j))

    def b_spec(b):
        return pl.BlockSpec((1, 1, tn), lambda i, j: (b, 0, j))

    gate_b = gate_b.reshape(3, 1, D_MODEL)

    return pl.pallas_call(
        _merge_kernel,
        out_shape=jax.ShapeDtypeStruct((m, D_MODEL), BF16),
        grid=(m // tm, per),
        in_specs=[o_spec, o_spec, o_spec, w_spec(0), w_spec(1), w_spec(2),
                  g_spec(0), g_spec(1), g_spec(2), b_spec(0), b_spec(1), b_spec(2)],
        out_specs=pl.BlockSpec((tm, tn), lambda i, j: (i, j)),
        compiler_params=_cparams(2),
        name="gated_merge",
    )(o_r, o_m, o_g, w_branch, w_branch, w_branch, proj, proj, proj, gate_b, gate_b, gate_b)


def _rmsnorm_kernel(x_ref, g_ref, o_ref):
    x = x_ref[...]
    ms = jnp.mean(x * x, axis=-1, keepdims=True)
    o_ref[...] = x * lax.rsqrt(ms + EPS) * g_ref[...]


def _rmsnorm(x, g, *, tm):
    m, k = x.shape
    return pl.pallas_call(
        _rmsnorm_kernel,
        out_shape=jax.ShapeDtypeStruct((m, k), F32),
        grid=(m // tm,),
        in_specs=[pl.BlockSpec((tm, k), lambda i: (i, 0)),
                  pl.BlockSpec((1, k), lambda i: (0, 0))],
        out_specs=pl.BlockSpec((tm, k), lambda i: (i, 0)),
        compiler_params=_cparams(1),
        name="final_rmsnorm",
    )(x, g.reshape(1, k))


def _rwkv_prep_kernel(pr_ref, pk_ref, pv_ref, ps_ref, qr_ref, qk_ref, qv_ref, qs_ref,
                      mur_ref, muk_ref, muv_ref, mus_ref,
                      w0_ref, a0_ref, kk_ref, ka_ref, rk_ref,
                      w2_ref, a2_ref, g2_ref, j_ref,
                      r_out, k_out, v_out, kkn_out, b_out, ld_out, g_out, bon_out,
                      *scratch, tm, explicit_prev):
    def shift(p_ref, q_ref, mu_ref, scr):
        p = p_ref[0].astype(F32)
        if explicit_prev:
            prev = q_ref[0]
        else:
            @pl.when(pl.program_id(1) == 0)
            def _():
                scr[7:8, :] = q_ref[0]

            scr[8:8 + tm, :] = p
            prev = scr[7:7 + tm, :]
            scr[7:8, :] = p[tm - 1:tm, :]
        return p + (prev - p) * mu_ref[...]

    scr = scratch if scratch else (None,) * 4
    xr = shift(pr_ref, qr_ref, mur_ref, scr[0])
    xk = shift(pk_ref, qk_ref, muk_ref, scr[1])
    xv = shift(pv_ref, qv_ref, muv_ref, scr[2])
    xs = shift(ps_ref, qs_ref, mus_ref, scr[3])

    w = -_softplus(-(w0_ref[...] + _dot(jnp.tanh(xs), w2_ref[...]))) - 0.5
    ld_out[0] = -jnp.exp(w)
    a = _sigmoid(a0_ref[...] + _dot(xs, a2_ref[...]))
    g_out[0] = _dot(_sigmoid(xs), g2_ref[...])

    ones_bd = j_ref[...]
    kkr = xk * kk_ref[...]
    k2 = xk * (1.0 + (a - 1.0) * ka_ref[...])
    rkk = xr * k2 * rk_ref[...]
    for p in range(R_PAIRS):
        sl = slice(p * 128, (p + 1) * 128)
        kb = kkr[:, sl]
        nrm = jnp.sqrt(_segsum(kb * kb, ones_bd))
        kn = kb / jnp.maximum(nrm, 1e-12)
        kkn_out[0, :, sl] = kn
        b_out[0, :, sl] = kn * a[:, sl]
        bon_out[0, :, sl] = _segsum(rkk[:, sl], ones_bd) * xv[:, sl]
    r_out[0] = xr
    k_out[0] = k2
    v_out[0] = xv


def _rwkv_prep(proj3, small3, prev3, lw, ones_bd, *, tm):
    bsz, t, _ = proj3.shape
    explicit_prev = prev3.shape[1] == t
    tq = tm if explicit_prev else 1
    qmap = (lambda blk: (lambda bi, i: (bi, i, blk))) if explicit_prev else (
        lambda blk: (lambda bi, i: (bi, 0, blk)))
    small_blk = 3 * BRANCH_W // SMALL_W
    big = lambda blk: pl.BlockSpec((1, tm, BRANCH_W), lambda bi, i: (bi, i, blk))
    vec = lambda blk: pl.BlockSpec((1, BRANCH_W), lambda bi, i: (0, blk))
    full = lambda shape: pl.BlockSpec(shape, lambda bi, i: (0, 0))
    out = jax.ShapeDtypeStruct((bsz, t, BRANCH_W), F32)
    scratch = [] if explicit_prev else (
        [pltpu.VMEM((tm + 8, BRANCH_W), F32)] * 3 + [pltpu.VMEM((tm + 8, SMALL_W), F32)])
    return pl.pallas_call(
        functools.partial(_rwkv_prep_kernel, tm=tm, explicit_prev=explicit_prev),
        out_shape=[out] * 8,
        grid=(bsz, t // tm),
        in_specs=[big(0), big(1), big(2),
                  pl.BlockSpec((1, tm, SMALL_W), lambda bi, i: (bi, i, 0)),
                  pl.BlockSpec((1, tq, BRANCH_W), qmap(0)),
                  pl.BlockSpec((1, tq, BRANCH_W), qmap(1)),
                  pl.BlockSpec((1, tq, BRANCH_W), qmap(2)),
                  pl.BlockSpec((1, tq, SMALL_W), qmap(small_blk)),
                  vec(0), vec(1), vec(2),
                  pl.BlockSpec((1, SMALL_W), lambda bi, i: (0, small_blk)),
                  vec(0), vec(0), vec(0), vec(0), vec(0),
                  full((SMALL_W, BRANCH_W)), full((SMALL_W, BRANCH_W)), full((SMALL_W, BRANCH_W)),
                  full((128, 128))],
        out_specs=[pl.BlockSpec((1, tm, BRANCH_W), lambda bi, i: (bi, i, 0))] * 8,
        scratch_shapes=scratch,
        compiler_params=_cparams(2),
        name="rwkv_prep",
    )(proj3, proj3, proj3, small3, prev3, prev3, prev3, prev3,
      lw["mu_p"], lw["mu_p"], lw["mu_p"], lw["mu_p"],
      lw["w0"], lw["a0"], lw["k_k"], lw["k_a"], lw["r_k"],
      lw["w2p"], lw["a2p"], lw["g2p"], ones_bd)


def _rwkv_scan_kernel(r_ref, k_ref, v_ref, kk_ref, b_ref, ld_ref, g_ref, bon_ref,
                      lng_ref, lnb_ref, j_ref, s0_ref, *rest, L, group, n_earlier, single_step):
    earlier = rest[:n_earlier]
    o_ref, st_ref, s_ref = rest[n_earlier:]
    c_id = pl.program_id(1)
    bb = r_ref.shape[0]
    seqs = range(bb)

    @pl.when(c_id == 0)
    def _():
        z = jnp.zeros((R_HEAD, R_HEAD), F32)
        for bi in seqs:
            for p in range(R_PAIRS):
                top = jnp.concatenate([s0_ref[0, bi, 2 * p], z], axis=1)
                bot = jnp.concatenate([z, s0_ref[0, bi, 2 * p + 1]], axis=1)
                s_ref[bi * R_PAIRS + p] = jnp.concatenate([top, bot], axis=0)

    def rows(ref, bi):
        x = ref[bi]
        if x.shape[0] == L:
            return x
        assert x.shape[0] == 1
        return jnp.where(lax.broadcasted_iota(jnp.int32, (L, x.shape[1]), 0) == 0, x, 0.0)

    r_all = [rows(r_ref, bi) for bi in seqs]
    k_all = [rows(k_ref, bi) for bi in seqs]
    v_all = [rows(v_ref, bi) for bi in seqs]
    kk_all = [rows(kk_ref, bi) for bi in seqs]
    b_all = [rows(b_ref, bi) for bi in seqs]
    g_all = [rows(g_ref, bi) for bi in seqs]
    bon_all = [rows(bon_ref, bi) for bi in seqs]
    ld_all = [rows(ld_ref, bi) for bi in seqs]
    cs_all = [_cumsum_rows(x, single_step) for x in ld_all]
    ec_all = [jnp.exp(c) for c in cs_all]
    enc_all = [jnp.exp(-c) for c in cs_all]
    ecm_all = [jnp.exp(c - x) for c, x in zip(cs_all, ld_all)]
    c_last_all = [c[L - 1:L, :] for c in cs_all]
    e_tail_all = [jnp.exp(cl - c) for cl, c in zip(c_last_all, cs_all)]
    g_last_all = [jnp.exp(cl) for cl in c_last_all]

    lane = lax.broadcasted_iota(jnp.int32, (L, 128), 1)
    head_a = lane < R_HEAD

    def stack(x):
        return jnp.concatenate([jnp.where(head_a, x, 0.0), jnp.where(head_a, 0.0, x)], axis=0)

    P2 = 2 * L
    ri = lax.broadcasted_iota(jnp.int32, (P2, P2), 0)
    ci = lax.broadcasted_iota(jnp.int32, (P2, P2), 1)
    strict = ri > ci
    incl = ri >= ci
    eye = jnp.where(ri == ci, 1.0, 0.0)
    ones_bd = j_ref[...]
    inv_n = 1.0 / R_HEAD

    cat0 = lambda a, b: jnp.concatenate([a, b], axis=0)
    cat1 = lambda a, b: jnp.concatenate([a, b], axis=1)

    n_units = bb * R_PAIRS
    assert n_units % group == 0
    for g0 in range(0, n_units, group):
        units = list(range(g0, g0 + group))
        sq = [u // R_PAIRS for u in units]
        sls = [slice((u % R_PAIRS) * 128, (u % R_PAIRS + 1) * 128) for u in units]
        each = lambda f: [f(i) for i in range(group)]
        sel = lambda xs, i: xs[sq[i]][:, sls[i]]

        S = each(lambda i: s_ref[units[i]])
        Rs = each(lambda i: stack(sel(r_all, i) * sel(ec_all, i)))
        Bs = each(lambda i: stack(sel(kk_all, i) * sel(ecm_all, i)))
        Ks = each(lambda i: stack(sel(k_all, i) * sel(enc_all, i)))
        As = each(lambda i: stack(-(sel(b_all, i) * sel(enc_all, i))))
        Vs = each(lambda i: stack(sel(v_all, i)))
        Kt = each(lambda i: stack(sel(k_all, i) * sel(e_tail_all, i)))
        At = each(lambda i: stack(-(sel(b_all, i) * sel(e_tail_all, i))))

        if P2 % 128 == 0:
            sc = each(lambda i: _dot_nt(cat0(Bs[i], Rs[i]), cat0(As[i], Ks[i])))
            s_ba = each(lambda i: sc[i][:P2, :P2])
            s_bk = each(lambda i: sc[i][:P2, P2:])
            s_ra = each(lambda i: sc[i][P2:, :P2])
            s_rk = each(lambda i: sc[i][P2:, P2:])
        else:
            s_ba = each(lambda i: _dot_nt(Bs[i], As[i]))
            s_bk = each(lambda i: _dot_nt(Bs[i], Ks[i]))
            s_ra = each(lambda i: _dot_nt(Rs[i], As[i]))
            s_rk = each(lambda i: _dot_nt(Rs[i], Ks[i]))
        Nm = each(lambda i: jnp.where(strict, s_ba[i], 0.0))
        Mbk = each(lambda i: jnp.where(strict, s_bk[i], 0.0))
        Mra = each(lambda i: jnp.where(incl, s_ra[i], 0.0))
        Mrk = each(lambda i: jnp.where(incl, s_rk[i], 0.0))

        Tm = each(lambda i: eye + Nm[i])
        Pw = Nm
        span = 2
        while span < L and not single_step:
            Pw = [_dot(x, x) for x in Pw]
            Tm = each(lambda i: Tm[i] + _dot(Tm[i], Pw[i]))
            span *= 2

        mv = each(lambda i: _dot(cat0(Mbk[i], Mrk[i]), Vs[i]))
        tb = each(lambda i: _dot(Tm[i], cat1(Bs[i], mv[i][:P2])))
        mu = each(lambda i: _dot(Mra[i], tb[i]))
        Ro = each(lambda i: Rs[i] + mu[i][:, :128])
        uo = each(lambda i: _dot_nt(cat0(tb[i][:, :128], Ro[i]), S[i]))
        U = each(lambda i: uo[i][:P2] + tb[i][:, 128:])
        O = each(lambda i: uo[i][P2:] + mv[i][P2:] + mu[i][:, 128:])
        for i, u in enumerate(units):
            s_ref[u] = S[i] * sel(g_last_all, i) + _dot_tn(
                cat0(U[i], Vs[i]), cat0(At[i], Kt[i]))

        for i in range(group):
            sl = sls[i]
            out = O[i][:L] + O[i][L:]
            mean = _segsum(out, ones_bd) * inv_n
            d = out - mean
            var = _segsum(d * d, ones_bd) * inv_n
            y = d * lax.rsqrt(var + R_GN_EPS) * lng_ref[:, sl] + lnb_ref[:, sl]
            o_ref[sq[i], :, sl] = ((y + sel(bon_all, i)) * sel(g_all, i)).astype(BF16)

    @pl.when(c_id == pl.num_programs(1) - 1)
    def _():
        for bi in seqs:
            for p in range(R_PAIRS):
                sp = s_ref[bi * R_PAIRS + p]
                st_ref[n_earlier, bi, 2 * p] = sp[:R_HEAD, :R_HEAD]
                st_ref[n_earlier, bi, 2 * p + 1] = sp[R_HEAD:, R_HEAD:]
        for i, e_ref in enumerate(earlier):
            st_ref[i] = e_ref[0]


def _rwkv_scan(seqs, g, bonus, ln_g, ln_b, ones_bd, s0, l, earlier_s, *, L, bb):
    bsz, t, _ = seqs[0].shape
    tb = min(t, L)
    nc = -(-t // L)
    nl = len(earlier_s) + 1
    seq_spec = pl.BlockSpec((bb, tb, BRANCH_W), lambda bi, c: (bi, c, 0))
    vec_spec = pl.BlockSpec((1, BRANCH_W), lambda bi, c: (0, 0))
    st_ea = pl.BlockSpec((1, bb, R_HEADS, R_HEAD, R_HEAD), lambda bi, c: (0, bi, 0, 0, 0))
    return pl.pallas_call(
        functools.partial(_rwkv_scan_kernel, L=L, group=bb * R_PAIRS, n_earlier=nl - 1,
                          single_step=t == 1),
        out_shape=[jax.ShapeDtypeStruct((bsz, nc * L, BRANCH_W), BF16),
                   jax.ShapeDtypeStruct((nl, bsz, R_HEADS, R_HEAD, R_HEAD), F32)],
        grid=(bsz // bb, nc),
        in_specs=[seq_spec] * 8 + [
            vec_spec, vec_spec, pl.BlockSpec((128, 128), lambda bi, c: (0, 0)),
            pl.BlockSpec((1, bb, R_HEADS, R_HEAD, R_HEAD), lambda bi, c: (l, bi, 0, 0, 0))]
        + [st_ea] * (nl - 1),
        out_specs=[pl.BlockSpec((bb, L, BRANCH_W), lambda bi, c: (bi, c, 0)),
                   pl.BlockSpec((nl, bb, R_HEADS, R_HEAD, R_HEAD), lambda bi, c: (0, bi, 0, 0, 0))],
        scratch_shapes=[pltpu.VMEM((bb * R_PAIRS, 128, 128), F32)],
        compiler_params=_cparams(2),
        name="rwkv_scan",
    )(*seqs, g, bonus, ln_g, ln_b, ones_bd, s0, *earlier_s)


def _rwkv_mixer(proj3, small3, prev, s0, l, lw, ones_bd, earlier_s, *, L, tm, bb):
    bsz, t, _ = proj3.shape
    new_shift = jnp.concatenate(
        [proj3[:, -1, :3 * BRANCH_W].astype(F32), small3[:, -1, :R_COLS - 3 * BRANCH_W]], axis=-1)
    prev3 = jnp.pad(prev, ((0, 0), (0, 3 * BRANCH_W + SMALL_W - R_COLS)))[:, None, :]
    if t == 1:
        outs = _rwkv_prep(proj3.reshape(1, bsz, -1), small3.reshape(1, bsz, -1),
                          prev3.reshape(1, bsz, -1), lw, ones_bd, tm=tm)
        outs = [o.reshape(bsz, 1, BRANCH_W) for o in outs]
    else:
        outs = _rwkv_prep(proj3, small3, prev3, lw, ones_bd, tm=tm)
    r, k2, v, kkn, b, ld, g, bonus = outs
    o_r, s_new = _rwkv_scan((r, k2, v, kkn, b, ld), g, bonus, lw["ln_g"], lw["ln_b"],
                            ones_bd, s0, l, earlier_s, L=L, bb=bb)
    return o_r[:, :t].reshape(bsz * t, BRANCH_W), s_new, new_shift


def _mlstm_kernel(q_ref, k_ref, v_ref, o_ref, sm_ref, cq_ref, ck_ref, wq_ref, wk_ref,
                  bq_ref, bk_ref, ib_ref, fb_ref, ng_ref, c0_ref, n0_ref, m0_ref,
                  *rest, L, t_valid, n_earlier):
    earlier = rest[:n_earlier]
    out_ref, c_ref, n_ref, m_ref, qs_ref, ks_ref = rest[n_earlier:]
    last = n_earlier
    c_id = pl.program_id(1)

    @pl.when(c_id == 0)
    def _():
        for i, e_ref in enumerate(earlier):
            c_ref[i] = e_ref[0]
        c_ref[last] = c0_ref[0]
        n_ref[...] = n0_ref[0]
        m_ref[...] = m0_ref[0]
        qs_ref[:, 5:8, :] = cq_ref[0]
        ks_ref[:, 5:8, :] = ck_ref[0]

    bb = q_ref.shape[0]
    row = lax.broadcasted_iota(jnp.int32, (L, 1), 0)
    valid = (c_id * L + row) < t_valid
    lane = lax.broadcasted_iota(jnp.int32, (L, SMALL_W), 1)
    head_lane = lane < M_HEADS
    q_seq, k_seq, v_seq, gate_seq, F4_seq, ig4_seq, gmf4_seq = ([] for _ in range(7))
    for s_ in range(bb):
        qs_ref[s_, 8:8 + L, :] = _rows(q_ref, L, s_)
        ks_ref[s_, 8:8 + L, :] = _rows(k_ref, L, s_)
        conv_q = bq_ref[...]
        conv_k = bk_ref[...]
        for j in range(CONV_W):
            conv_q = conv_q + qs_ref[s_, 5 + j:5 + j + L, :] * wq_ref[j:j + 1, :]
            conv_k = conv_k + ks_ref[s_, 5 + j:5 + j + L, :] * wk_ref[j:j + 1, :]
        if L >= CONV_W - 1:
            tail_q = qs_ref[s_, 5 + L:8 + L, :]
            tail_k = ks_ref[s_, 5 + L:8 + L, :]
            qs_ref[s_, 5:8, :] = tail_q
            ks_ref[s_, 5:8, :] = tail_k
        q_seq.append(jnp.where(valid, _silu(conv_q), 0.0))
        k_seq.append(jnp.where(valid, _silu(conv_k) * (M_DK ** -0.5), 0.0))
        v_seq.append(jnp.where(valid, _rows(v_ref, L, s_), 0.0))
        gate_seq.append(_sigmoid(_rows(o_ref, L, s_)))
        sm = _rows(sm_ref, L, s_)
        i_pre = jnp.where(head_lane, pltpu.roll(sm, SMALL_W - S_I, axis=1), 0.0)
        f_pre = jnp.where(head_lane, pltpu.roll(sm, SMALL_W - S_F, axis=1), 0.0)
        ig4 = jnp.where(valid & head_lane, i_pre + ib_ref[...], NEG)
        lf4 = jnp.where(valid & head_lane, _log_sigmoid(f_pre + fb_ref[...]), 0.0)
        F4 = _cumsum_rows(lf4, t_valid == 1)
        F4_seq.append(F4)
        ig4_seq.append(ig4)
        gmf4_seq.append(ig4 - F4)

    ti = lax.broadcasted_iota(jnp.int32, (L, L), 0)
    si = lax.broadcasted_iota(jnp.int32, (L, L), 1)
    causal = si <= ti
    diag = ti == si
    ones_l = jnp.ones((L, L), F32)

    units = [(s_, h) for s_ in range(bb) for h in range(M_HEADS)]
    heads = range(len(units))
    us = [u[0] for u in units]
    uh = [u[1] for u in units]
    each = lambda f: [f(h) for h in heads]
    sls = [slice(uh[u] * M_DK, (uh[u] + 1) * M_DK) for u in heads]
    rowsum = lambda x: jnp.sum(x, axis=-1, keepdims=True)
    q = each(lambda h: q_seq[us[h]][:, sls[h]])
    k = each(lambda h: k_seq[us[h]][:, sls[h]])
    v = each(lambda h: v_seq[us[h]][:, sls[h]])
    F = each(lambda h: rowsum(jnp.where(lane == uh[h], F4_seq[us[h]], 0.0)))
    ig = each(lambda h: rowsum(jnp.where(lane == uh[h], ig4_seq[us[h]], 0.0)))
    gmf = each(lambda h: rowsum(jnp.where(lane == uh[h], gmf4_seq[us[h]], 0.0)))
    if t_valid == 1:
        g_row = each(lambda h: jnp.where(si == 0, gmf[h][0:1, :], NEG))
    else:
        g_row = each(lambda h: _dot_hi(
            ones_l, jnp.where(diag, jnp.broadcast_to(gmf[h], (L, L)), 0.0)))
    Dm = each(lambda h: jnp.where(causal, F[h] + g_row[h], NEG))

    C = each(lambda h: c_ref[last, us[h], uh[h]])
    n = each(lambda h: n_ref[us[h], uh[h]])
    m_prev = each(lambda h: m_ref[us[h], uh[h]])
    inter = each(lambda h: F[h] + m_prev[h])
    m_t = each(lambda h: jnp.maximum(inter[h], jnp.max(Dm[h], axis=-1, keepdims=True)))
    w_inter = each(lambda h: jnp.exp(inter[h] - m_t[h]))
    Sm = each(lambda h: _dot_nt(q[h], k[h]) * jnp.exp(Dm[h] - m_t[h]))
    num = each(lambda h: w_inter[h] * _dot(q[h], C[h]) + _dot(Sm[h], v[h]))
    den = each(lambda h: w_inter[h] * rowsum(q[h] * n[h]) + rowsum(Sm[h]))
    hh = each(lambda h: num[h] / jnp.maximum(jnp.abs(den[h]), jnp.exp(-m_t[h])))

    FL = each(lambda h: F[h][L - 1:L, :])
    g_s = each(lambda h: FL[h] - F[h] + ig[h])
    m_new = each(lambda h: jnp.maximum(FL[h] + m_prev[h], jnp.max(g_s[h], axis=0, keepdims=True)))
    a_c = each(lambda h: jnp.exp(FL[h] + m_prev[h] - m_new[h]))
    kw = each(lambda h: k[h] * jnp.exp(g_s[h] - m_new[h]))
    for h in heads:
        c_ref[last, us[h], uh[h]] = a_c[h] * C[h] + _dot_tn(kw[h], v[h])
        n_ref[us[h], uh[h]] = a_c[h] * n[h] + jnp.sum(kw[h], axis=0, keepdims=True)
        m_ref[us[h], uh[h]] = m_new[h]
        hn = hh[h] * lax.rsqrt(jnp.mean(hh[h] * hh[h], axis=-1, keepdims=True) + EPS)
        out_ref[us[h], :, sls[h]] = (gate_seq[us[h]][:, sls[h]] * hn
                                     * ng_ref[:, sls[h]]).astype(BF16)


def _rows(ref, L, seq=0):
    x = ref[seq].astype(F32)
    if x.shape[0] == L:
        return x
    assert x.shape[0] == 1
    row = lax.broadcasted_iota(jnp.int32, (L, x.shape[1]), 0)
    return jnp.where(row == 0, x, 0.0)


def _mlstm_mixer(proj3, small3, conv_buf, c0, n0, m0, l, lw, earlier_c, *, L, bb):
    bsz, t, _ = proj3.shape
    tb = min(t, L)
    nc = -(-t // L)
    nl = len(earlier_c) + 1
    seq = lambda col: pl.BlockSpec((bb, tb, BRANCH_W), lambda bi, c: (bi, c, col // BRANCH_W))
    st_c = pl.BlockSpec((nl, bb, M_HEADS, M_DK, M_DK), lambda bi, c: (0, bi, 0, 0, 0))
    ea_c = pl.BlockSpec((1, bb, M_HEADS, M_DK, M_DK), lambda bi, c: (0, bi, 0, 0, 0))
    st_n = pl.BlockSpec((bb, M_HEADS, 1, M_DK), lambda bi, c: (bi, 0, 0, 0))
    st_m = pl.BlockSpec((bb, M_HEADS, 1, 1), lambda bi, c: (bi, 0, 0, 0))
    in_c = pl.BlockSpec((1, bb, M_HEADS, M_DK, M_DK), lambda bi, c: (l, bi, 0, 0, 0))
    in_n = pl.BlockSpec((1, bb, M_HEADS, 1, M_DK), lambda bi, c: (l, bi, 0, 0, 0))
    in_m = pl.BlockSpec((1, bb, M_HEADS, 1, 1), lambda bi, c: (l, bi, 0, 0, 0))
    conv = lambda blk: pl.BlockSpec((1, bb, CONV_W - 1, BRANCH_W), lambda bi, c: (l, bi, 0, blk))
    cw = lambda blk: pl.BlockSpec((CONV_W, BRANCH_W), lambda bi, c: (0, blk))
    vec = lambda blk: pl.BlockSpec((1, BRANCH_W), lambda bi, c: (0, blk))
    hb = pl.BlockSpec((1, SMALL_W), lambda bi, c: (0, 0))
    pad_heads = lambda a: jnp.pad(a, ((0, 0), (0, SMALL_W - M_HEADS)))
    out, c_new, n_new, m_new = pl.pallas_call(
        functools.partial(_mlstm_kernel, L=L, t_valid=t, n_earlier=nl - 1),
        out_shape=[jax.ShapeDtypeStruct((bsz, nc * L, BRANCH_W), BF16),
                   jax.ShapeDtypeStruct((nl, bsz, M_HEADS, M_DK, M_DK), F32),
                   jax.ShapeDtypeStruct((bsz, M_HEADS, 1, M_DK), F32),
                   jax.ShapeDtypeStruct((bsz, M_HEADS, 1, 1), F32)],
        grid=(bsz // bb, nc),
        in_specs=[seq(C_MQK), seq(C_MQK + BRANCH_W), seq(C_MV), seq(C_MO),
                  pl.BlockSpec((bb, tb, SMALL_W), lambda bi, c: (bi, c, 0)),
                  conv(0), conv(1), cw(0), cw(1), vec(0), vec(1), hb, hb, vec(0),
                  in_c, in_n, in_m] + [ea_c] * (nl - 1),
        out_specs=[pl.BlockSpec((bb, L, BRANCH_W), lambda bi, c: (bi, c, 0)), st_c, st_n, st_m],
        scratch_shapes=[pltpu.VMEM((bb, L + 8, BRANCH_W), F32),
                        pltpu.VMEM((bb, L + 8, BRANCH_W), F32)],
        compiler_params=_cparams(2),
        name="mlstm_scan",
    )(proj3, proj3, proj3, proj3, small3, conv_buf, conv_buf,
      lw["conv_w"], lw["conv_w"], lw["conv_b"], lw["conv_b"],
      pad_heads(lw["i_b"]), pad_heads(lw["f_b"]),
      lw["m_norm_g"], c0, n0.reshape(DEPTH, bsz, M_HEADS, 1, M_DK),
      m0.reshape(DEPTH, bsz, M_HEADS, 1, 1), *earlier_c)
    return out, c_new, n_new.reshape(bsz, M_HEADS, M_DK), m_new.reshape(bsz, M_HEADS)


def _gla_kernel(q_ref, k_ref, v_ref, og_ref, sm_ref, a2_ref, ab_ref, ng_ref, s0_ref,
                *rest, L, t_valid, n_earlier):
    earlier = rest[:n_earlier]
    out_ref, s_ref, b_scr, q_scr = rest[n_earlier:]
    last = n_earlier
    c_id = pl.program_id(1)

    @pl.when(c_id == 0)
    def _():
        for i, e_ref in enumerate(earlier):
            s_ref[i] = e_ref[0]
        s_ref[last] = s0_ref[0]

    bb = q_ref.shape[0]
    row = lax.broadcasted_iota(jnp.int32, (L, 1), 0)
    valid = (c_id * L + row) < t_valid
    q_seq, k_seq, v_seq, gate_seq, lg_seq, b_seq, eb_seq, e_tail_seq = ([] for _ in range(8))
    for si_ in range(bb):
        q_s = jnp.where(valid, _rows(q_ref, L, si_) * (G_DK ** -0.5), 0.0)
        lg_s = _log_sigmoid(_dot(_rows(sm_ref, L, si_), a2_ref[...]) + ab_ref[...])
        lg_s = jnp.where(valid, lg_s * (1.0 / G_GATE_NORM), 0.0)
        b_s = _cumsum_rows(lg_s, t_valid == 1)
        b_scr[si_] = b_s
        q_scr[si_] = q_s
        q_seq.append(q_s)
        k_seq.append(jnp.where(valid, _rows(k_ref, L, si_), 0.0))
        v_seq.append(jnp.where(valid, _rows(v_ref, L, si_), 0.0))
        gate_seq.append(_silu(_rows(og_ref, L, si_)))
        lg_seq.append(lg_s)
        b_seq.append(b_s)
        eb_seq.append(jnp.exp(b_s))
        e_tail_seq.append(jnp.exp(b_s[L - 1:L, :] - b_s))
    ones_lv = jnp.ones((L, G_DV), F32)

    n_t = L if t_valid >= L else t_valid
    SB = GLA_SUB
    s_col = {rows: lax.broadcasted_iota(jnp.int32, (rows, 1), 0) for rows in (8, SB)}
    t_lane = {rows: lax.broadcasted_iota(jnp.int32, (rows, SB), 1) for rows in (8, SB)}

    heads = range(G_HEADS)
    sls = [slice(h * G_DK, (h + 1) * G_DK) for h in heads]
    svs = [slice(h * G_DV, (h + 1) * G_DV) for h in heads]

    units = [(s_, h) for s_ in range(bb) for h in heads]
    o_parts = [[] for _ in units]
    for r0 in range(0, L, SB):
        n_sub = max(0, min(SB, n_t - r0))
        for u, (s_, h) in enumerate(units):
            sl, sv = sls[h], svs[h]
            k_all, b_all, v_all = k_seq[s_], b_seq[s_], v_seq[s_]
            ki, bi, vi = k_all[r0:r0 + SB, sl], b_all[r0:r0 + SB, sl], v_all[r0:r0 + SB, sv]
            att = jnp.zeros((SB, SB), F32)
            for tl in range(n_sub):
                rows = 8 * (tl // 8 + 1)
                bt = b_scr[s_, r0 + tl:r0 + tl + 1, sl]
                qt = q_scr[s_, r0 + tl:r0 + tl + 1, sl]
                e = jnp.exp(jnp.where(s_col[rows] <= tl, bt - bi[:rows], NEG))
                col = jnp.sum(qt * ki[:rows] * e, axis=-1, keepdims=True)
                top = jnp.where(t_lane[rows] == tl, col, att[:rows])
                att = top if rows == SB else jnp.concatenate([top, att[rows:]], axis=0)
            o_i = _dot_tn(att, vi)
            if r0 > 0 and n_sub > 0:
                ref = b_scr[s_, r0 - 1:r0, sl]
                a_off = _dot_nt(q_seq[s_][r0:r0 + SB, sl] * jnp.exp(bi - ref),
                                k_all[:r0, sl] * jnp.exp(ref - b_all[:r0, sl]))
                o_i = o_i + _dot(a_off, v_all[:r0, sv])
            o_parts[u].append(o_i)

    for u, (s_, h) in enumerate(units):
        sl, sv = sls[h], svs[h]
        q, k, v = q_seq[s_][:, sl], k_seq[s_][:, sl], v_seq[s_][:, sv]
        o_intra = o_parts[u][0] if len(o_parts[u]) == 1 else jnp.concatenate(o_parts[u], axis=0)

        S = s_ref[last, s_, h]
        o = _dot(q * eb_seq[s_][:, sl], S) + o_intra
        decay = jnp.exp(lax.dot_general(lg_seq[s_][:, sl], ones_lv, (((0,), (0,)), ((), ())),
                                        precision=lax.Precision.HIGHEST,
                                        preferred_element_type=F32))
        s_ref[last, s_, h] = S * decay + _dot_tn(k * e_tail_seq[s_][:, sl], v)

        on = o * lax.rsqrt(jnp.mean(o * o, axis=-1, keepdims=True) + EPS) * ng_ref[:, sv]
        out_ref[s_, :, sv] = (on * gate_seq[s_][:, sv]).astype(BF16)


def _gla_mixer(proj3, small3, s0, l, lw, earlier_s, *, L, bb):
    bsz, t, _ = proj3.shape
    tb = min(t, L)
    nc = -(-t // L)
    nl = len(earlier_s) + 1
    gw = G_HEADS * G_DK
    st = pl.BlockSpec((nl, bb, G_HEADS, G_DK, G_DV), lambda bi, c: (0, bi, 0, 0, 0))
    st_ea = pl.BlockSpec((1, bb, G_HEADS, G_DK, G_DV), lambda bi, c: (0, bi, 0, 0, 0))
    st_in = pl.BlockSpec((1, bb, G_HEADS, G_DK, G_DV), lambda bi, c: (l, bi, 0, 0, 0))
    return pl.pallas_call(
        functools.partial(_gla_kernel, L=L, t_valid=t, n_earlier=nl - 1),
        out_shape=[jax.ShapeDtypeStruct((bsz, nc * L, BRANCH_W), BF16),
                   jax.ShapeDtypeStruct((nl, bsz, G_HEADS, G_DK, G_DV), F32)],
        grid=(bsz // bb, nc),
        in_specs=[pl.BlockSpec((bb, tb, gw), lambda bi, c: (bi, c, C_GQ // gw)),
                  pl.BlockSpec((bb, tb, gw), lambda bi, c: (bi, c, C_GQ // gw + 1)),
                  pl.BlockSpec((bb, tb, BRANCH_W), lambda bi, c: (bi, c, C_GV // BRANCH_W)),
                  pl.BlockSpec((bb, tb, BRANCH_W), lambda bi, c: (bi, c, C_GOG // BRANCH_W)),
                  pl.BlockSpec((bb, tb, SMALL_W), lambda bi, c: (bi, c, 0)),
                  pl.BlockSpec((SMALL_W, gw), lambda bi, c: (0, 0)),
                  pl.BlockSpec((1, gw), lambda bi, c: (0, 0)),
                  pl.BlockSpec((1, BRANCH_W), lambda bi, c: (0, 0)),
                  st_in] + [st_ea] * (nl - 1),
        out_specs=[pl.BlockSpec((bb, L, BRANCH_W), lambda bi, c: (bi, c, 0)), st],
        scratch_shapes=[pltpu.VMEM((bb, L, gw), F32), pltpu.VMEM((bb, L, gw), F32)],
        compiler_params=_cparams(2),
        name="gla_scan",
    )(proj3, proj3, proj3, proj3, small3, lw["g_a2p"], lw["g_a_b"], lw["g_norm_g"], s0,
      *earlier_s)


def _small_group_rows(wt):
    depth, _, k = wt.shape
    used = (R_COLS - 3072) + 2 * M_HEADS + G_LR
    return jnp.concatenate(
        [wt[:, W_R0 + 3072:W_R0 + 3264], wt[:, W_M0 + 3072:W_M0 + 3080],
         wt[:, W_G0 + 2048:W_G0 + 2064], jnp.zeros((depth, MAIN_TN - used, k), wt.dtype)], axis=1)


def _rows_padded(w, row0, total):
    return jnp.pad(w, ((row0, total - row0 - w.shape[0]), (0, 0)))


def _layer_weights(l, P):
    mu = P["rwkv_mu"][l]
    mu_p = jnp.concatenate([mu, jnp.zeros((3 * BRANCH_W + SMALL_W - R_COLS,), F32)]).reshape(1, -1)
    row = lambda a: a.reshape(1, -1)
    return {
        "norm1_g": P["norm1_g"][l], "gate_b": P["gate_b"][l],
        "mu_p": mu_p, "w0": row(P["rwkv_w0"][l]), "a0": row(P["rwkv_a0"][l]),
        "k_k": row(P["rwkv_k_k"][l]), "k_a": row(P["rwkv_k_a"][l]), "r_k": row(P["rwkv_r_k"][l]),
        "w2p": _rows_padded(P["rwkv_w2"][l], 0, SMALL_W),
        "a2p": _rows_padded(P["rwkv_a2"][l], R_LORA, SMALL_W),
        "g2p": _rows_padded(P["rwkv_g2"][l], 2 * R_LORA, SMALL_W),
        "ln_g": row(P["rwkv_ln_g"][l]), "ln_b": row(P["rwkv_ln_b"][l]),
        "conv_w": P["mlstm_conv_w"][l], "conv_b": row(P["mlstm_conv_b"][l]),
        "i_b": row(P["mlstm_i_b"][l]), "f_b": row(P["mlstm_f_b"][l]),
        "m_norm_g": row(P["mlstm_norm_g"][l]),
        "g_a2p": _rows_padded(P["gla_a2"][l], S_GXA, SMALL_W), "g_a_b": row(P["gla_a_b"][l]),
        "g_norm_g": row(P["gla_norm_g"][l]),
        "norm2_g": P["norm2_g"][l],
    }


def _layer(x2, bsz, t, states, l, lw, big, ones_bd, cfg, earlier):
    ea_wkv, ea_c, ea_s = earlier
    rw_prev, rw_s, m_conv, m_c, m_n, m_m, g_s = states
    m = bsz * t
    L, tm = cfg["L"], cfg["tm"]
    proj, small = _rms_matmul(x2, lw["norm1_g"], big["w_in_t"], big["w_in_t_small"], l, tm=tm,
                              main_dtype=cfg["proj_dtype"])
    proj3 = proj.reshape(bsz, t, C_SMALL)
    small3 = small.reshape(bsz, t, -1)

    o_r, rw_s_new, rw_prev_new = _rwkv_mixer(proj3, small3, rw_prev[l], rw_s, l, lw, ones_bd,
                                             ea_wkv, L=cfg["L_rwkv"], tm=cfg["tm_prep"],
                                             bb=cfg["bb_rwkv"])

    o_m, m_c_new, m_n_new, m_m_new = _mlstm_mixer(proj3, small3, m_conv, m_c, m_n, m_m, l, lw,
                                                  ea_c, L=L, bb=cfg["bb_mlstm"])
    qk_tail = proj3[:, -min(t, CONV_W - 1):, C_MQK:C_MQK + 2 * BRANCH_W].astype(F32)
    m_conv_new = jnp.concatenate([m_conv[l], qk_tail], axis=1)[:, -(CONV_W - 1):]
    o_g, g_s_new = _gla_mixer(proj3, small3, g_s, l, lw, ea_s, L=L, bb=cfg["bb_gla"])
    o_m = o_m[:, :t].reshape(m, BRANCH_W)
    o_g = o_g[:, :t].reshape(m, BRANCH_W)

    merged = _merge(o_r, o_m, o_g, big["w_branch"], l, proj, lw["gate_b"],
                    tm=cfg["tm_merge"], tn=512)
    x2 = _matmul_residual(merged, big["w_out"], l, x2, tm=tm, tn=512, tk=D_MODEL)
    hidden = _rms_swiglu(x2, lw["norm2_g"], big["w_gu"], l, tm=tm, tn=512)
    x2 = _matmul_residual(hidden, big["w_down"], l, x2, tm=tm, tn=512, tk=2816)
    return x2, (rw_prev_new, rw_s_new, m_conv_new, m_c_new, m_n_new, m_m_new, g_s_new)


def _trunk(x, states, layer_ws, big, final_g, ones_bd, cfg):
    bsz, t, d = x.shape
    x2 = x.reshape(bsz * t, d)
    per_layer = []
    for l in range(DEPTH):
        is_last = l == DEPTH - 1
        earlier = tuple([st[i] for st in per_layer] if is_last else [] for i in BIG_STATES)
        x2, new = _layer(x2, bsz, t, states, l, layer_ws[l], big, ones_bd, cfg, earlier)
        per_layer.append(new)
    new_states = [per_layer[-1][i] if i in BIG_STATES
                  else jnp.stack([st[i] for st in per_layer], axis=0) for i in range(len(states))]
    y = _rmsnorm(x2, final_g, tm=cfg["tm_norm"]).reshape(bsz, t, d)
    return y, new_states


BIG_STATES = (1, 3, 6)

PROMPT_CFG = dict(L=64, L_rwkv=64, bb_rwkv=1, bb_gla=1, bb_mlstm=1, tm=1024, tm_prep=256, tm_merge=1024, tm_norm=512, proj_dtype=BF16)
SAMPLE_CFG = dict(L=16, L_rwkv=16, bb_rwkv=4, bb_gla=4, bb_mlstm=2, tm=128, tm_prep=128, tm_merge=128, tm_norm=128, proj_dtype=F32)


def _zero_states(bsz):
    return (jnp.zeros((DEPTH, bsz, R_COLS), F32),
            jnp.zeros((DEPTH, bsz, R_HEADS, R_HEAD, R_HEAD), F32),
            jnp.zeros((DEPTH, bsz, CONV_W - 1, 2 * BRANCH_W), F32),
            jnp.zeros((DEPTH, bsz, M_HEADS, M_DK, M_DK), F32),
            jnp.zeros((DEPTH, bsz, M_HEADS, M_DK), F32),
            jnp.zeros((DEPTH, bsz, M_HEADS), F32),
            jnp.zeros((DEPTH, bsz, G_HEADS, G_DK, G_DV), F32))


def kernel(x_prompt, x_sample, state_rwkv_shift, state_rwkv_wkv, state_mlstm_conv, state_mlstm_C, state_mlstm_n, state_mlstm_m, state_gla_S, norm1_g, w_in, gate_b, rwkv_mu, rwkv_w0, rwkv_w2, rwkv_a0, rwkv_a2, rwkv_g2, rwkv_k_k, rwkv_k_a, rwkv_r_k, rwkv_ln_g, rwkv_ln_b, mlstm_conv_w, mlstm_conv_b, mlstm_i_b, mlstm_f_b, mlstm_norm_g, gla_a2, gla_a_b, gla_norm_g, w_branch, w_out, norm2_g, ffn_w_gu, ffn_w_down, final_norm_g):
    P = dict(norm1_g=norm1_g, w_in=w_in, gate_b=gate_b, rwkv_mu=rwkv_mu, rwkv_w0=rwkv_w0,
             rwkv_w2=rwkv_w2, rwkv_a0=rwkv_a0, rwkv_a2=rwkv_a2, rwkv_g2=rwkv_g2,
             rwkv_k_k=rwkv_k_k, rwkv_k_a=rwkv_k_a, rwkv_r_k=rwkv_r_k, rwkv_ln_g=rwkv_ln_g,
             rwkv_ln_b=rwkv_ln_b, mlstm_conv_w=mlstm_conv_w, mlstm_conv_b=mlstm_conv_b,
             mlstm_i_b=mlstm_i_b, mlstm_f_b=mlstm_f_b, mlstm_norm_g=mlstm_norm_g,
             gla_a2=gla_a2, gla_a_b=gla_a_b, gla_norm_g=gla_norm_g, w_branch=w_branch,
             w_out=w_out, norm2_g=norm2_g, ffn_w_gu=ffn_w_gu, ffn_w_down=ffn_w_down)
    layer_ws = [_layer_weights(l, P) for l in range(DEPTH)]
    w_in_t = jnp.swapaxes(w_in, 1, 2)
    big = dict(w_in_t=w_in_t, w_in_t_small=_small_group_rows(w_in_t), w_branch=w_branch,
               w_out=w_out, w_gu=ffn_w_gu, w_down=ffn_w_down)
    head_of_lane = jnp.arange(128) // R_HEAD
    ones_bd = (head_of_lane[:, None] == head_of_lane[None, :]).astype(BF16)

    y_p, p_states = _trunk(x_prompt, _zero_states(x_prompt.shape[0]), layer_ws, big,
                           final_norm_g, ones_bd, PROMPT_CFG)
    s_states = (state_rwkv_shift, state_rwkv_wkv, state_mlstm_conv, state_mlstm_C,
                state_mlstm_n, state_mlstm_m, state_gla_S)
    y_s, s_states = _trunk(x_sample, s_states, layer_ws, big, final_norm_g, ones_bd, SAMPLE_CFG)
    return (y_p, y_s, *p_states, *s_states)
```

```python
import functools

import jax
import jax.numpy as jnp
from jax import lax
from jax.experimental import pallas as pl
from jax.experimental.pallas import tpu as pltpu

F32 = jnp.float32
BF16 = jnp.bfloat16

D_MODEL = 2048
DEPTH = 2
BRANCH_W = 1024
R_HEADS, R_HEAD = 16, 64
R_PAIRS = R_HEADS // 2
R_LORA = 64
R_COLS = 3 * BRANCH_W + 3 * R_LORA
R_GN_EPS = 64e-5
M_HEADS, M_DK = 4, 256
CONV_W = 4
G_HEADS, G_DK, G_DV = 4, 128, 256
G_LR = 16
G_GATE_NORM = 16.0
D_FF = 5632
EPS = 1e-6
NEG = -1e30
GLA_SUB = 16

C_RWKV = 0
C_MQK = 3072
C_MV = 5120
C_MO = 6144
C_GQ = 7168
C_GV = 8192
C_GOG = 9216
C_GATE = 10240
C_SMALL = 16384
SMALL_W = 256
S_I, S_F, S_GXA = 192, 196, 200
MAIN_TN = 512
W_R0, W_M0, W_G0, W_T0 = 0, R_COLS, R_COLS + 4104, R_COLS + 4104 + 3088
MAIN_RUNS = ((C_RWKV, W_R0, 3072), (C_MQK, W_M0, 3072), (C_MO, W_M0 + 3080, 1024),
             (C_GQ, W_G0, 2048), (C_GOG, W_G0 + 2064, 1024 + 3 * D_MODEL))

VMEM_LIMIT = 56 * 1024 * 1024


def _cparams(n_axes):
    return pltpu.CompilerParams(dimension_semantics=("arbitrary",) * n_axes,
                                vmem_limit_bytes=VMEM_LIMIT)


def _dot(a, b):
    return jnp.dot(a.astype(BF16), b.astype(BF16), preferred_element_type=F32)


def _dot_nt(a, b):
    return lax.dot_general(a.astype(BF16), b.astype(BF16), (((1,), (1,)), ((), ())),
                           preferred_element_type=F32)


def _dot_tn(a, b):
    return lax.dot_general(a.astype(BF16), b.astype(BF16), (((0,), (0,)), ((), ())),
                           preferred_element_type=F32)


def _dot_hi(a, b):
    return jnp.dot(a, b, precision=lax.Precision.HIGHEST, preferred_element_type=F32)


def _cumsum_rows(x, single_step):
    n = x.shape[0]
    if single_step:
        row = lax.broadcasted_iota(jnp.int32, x.shape, 0)
        return jnp.where(row == 0, x, x[0:1, :])
    ti = lax.broadcasted_iota(jnp.int32, (n, n), 0)
    si = lax.broadcasted_iota(jnp.int32, (n, n), 1)
    return _dot_hi((si <= ti).astype(F32), x)


def _segsum(y, ones_blockdiag):
    hi = y.astype(BF16)
    lo = (y - hi.astype(F32)).astype(BF16)
    return (jnp.dot(hi, ones_blockdiag, preferred_element_type=F32)
            + jnp.dot(lo, ones_blockdiag, preferred_element_type=F32))


def _sigmoid(x):
    return 1.0 / (1.0 + jnp.exp(-x))


def _silu(x):
    return x * _sigmoid(x)


def _log_sigmoid(x):
    return -_softplus(-x)


def _softplus(x):
    return jnp.maximum(x, 0.0) + jnp.log(1.0 + jnp.exp(-jnp.abs(x)))


def _rms_mm_kernel(start_ref, x_ref, g_ref, wt_ref, wsm_ref, o_ref, osm_ref, h_scr, *, n_main):
    del start_ref
    j = pl.program_id(1)

    @pl.when(j == 0)
    def _():
        x = x_ref[...]
        ms = jnp.mean(x * x, axis=-1, keepdims=True)
        h_scr[...] = (x * lax.rsqrt(ms + EPS) * g_ref[...]).astype(BF16)

    @pl.when(j < n_main)
    def _():
        o_ref[...] = _dot_nt(h_scr[...], wt_ref[0]).astype(o_ref.dtype)

    @pl.when(j >= n_main)
    def _():
        osm_ref[...] = _dot_nt(h_scr[...], wsm_ref[0])


def _rms_matmul(x, g, wt, wt_small, l, *, tm, main_dtype):
    m, k = x.shape
    n_in = wt.shape[1]
    tn = MAIN_TN
    starts = []
    for p0, s0, width in MAIN_RUNS:
        assert p0 == len(starts) * tn and width % tn == 0 and s0 % 8 == 0
        starts += list(range(s0, s0 + width, tn))
    n_main = len(starts)
    assert n_main * tn == C_SMALL and starts[-1] + tn == n_in
    grid_spec = pltpu.PrefetchScalarGridSpec(
        num_scalar_prefetch=1,
        grid=(m // tm, n_main + 1),
        in_specs=[pl.BlockSpec((tm, k), lambda i, j, st: (i, 0)),
                  pl.BlockSpec((1, k), lambda i, j, st: (0, 0)),
                  pl.BlockSpec((pl.Element(1), pl.Element(tn), pl.Element(k)),
                               lambda i, j, st: (l, st[j] * 8, 0)),
                  pl.BlockSpec((1, tn, k), lambda i, j, st: (l, 0, 0))],
        out_specs=[pl.BlockSpec((tm, tn), lambda i, j, st: (i, jnp.minimum(j, n_main - 1))),
                   pl.BlockSpec((tm, tn), lambda i, j, st: (i, 0))],
        scratch_shapes=[pltpu.VMEM((tm, k), BF16)],
    )
    return pl.pallas_call(
        functools.partial(_rms_mm_kernel, n_main=n_main),
        out_shape=[jax.ShapeDtypeStruct((m, C_SMALL), main_dtype),
                   jax.ShapeDtypeStruct((m, tn), F32)],
        grid_spec=grid_spec,
        compiler_params=_cparams(2),
        name="rms_in_proj",
    )(jnp.asarray([s // 8 for s in starts] + [0], jnp.int32), x, g.reshape(1, k), wt, wt_small)


def _rms_swiglu_kernel(x_ref, g_ref, wg_ref, wu_ref, o_ref, h_scr):
    @pl.when(pl.program_id(1) == 0)
    def _():
        x = x_ref[...]
        ms = jnp.mean(x * x, axis=-1, keepdims=True)
        h_scr[...] = (x * lax.rsqrt(ms + EPS) * g_ref[...]).astype(BF16)

    h = h_scr[...]
    gg = jnp.dot(h, wg_ref[0].astype(BF16), preferred_element_type=F32)
    uu = jnp.dot(h, wu_ref[0].astype(BF16), preferred_element_type=F32)
    o_ref[...] = (_silu(gg) * uu).astype(BF16)


def _rms_swiglu(x, g, w_gu, l, *, tm, tn):
    m, k = x.shape
    nj = D_FF // tn
    return pl.pallas_call(
        _rms_swiglu_kernel,
        out_shape=jax.ShapeDtypeStruct((m, D_FF), BF16),
        grid=(m // tm, nj),
        in_specs=[pl.BlockSpec((tm, k), lambda i, j: (i, 0)),
                  pl.BlockSpec((1, k), lambda i, j: (0, 0)),
                  pl.BlockSpec((1, k, tn), lambda i, j: (l, 0, j)),
                  pl.BlockSpec((1, k, tn), lambda i, j: (l, 0, j + nj))],
        out_specs=pl.BlockSpec((tm, tn), lambda i, j: (i, j)),
        scratch_shapes=[pltpu.VMEM((tm, k), BF16)],
        compiler_params=_cparams(2),
        name="rms_ffn_swiglu",
    )(x, g.reshape(1, k), w_gu, w_gu)


def _mm_res_kernel(a_ref, w_ref, res_ref, o_ref, acc_ref, *, nk):
    kk = pl.program_id(2)

    @pl.when(kk == 0)
    def _():
        acc_ref[...] = jnp.zeros_like(acc_ref)

    acc_ref[...] += jnp.dot(a_ref[...], w_ref[0].astype(BF16), preferred_element_type=F32)

    @pl.when(kk == nk - 1)
    def _():
        o_ref[...] = acc_ref[...] + res_ref[...]


def _matmul_residual(a, w, l, res, *, tm, tn, tk):
    m, k = a.shape
    n = w.shape[2]
    nk = k // tk
    return pl.pallas_call(
        functools.partial(_mm_res_kernel, nk=nk),
        out_shape=jax.ShapeDtypeStruct((m, n), F32),
        grid=(m // tm, n // tn, nk),
        in_specs=[pl.BlockSpec((tm, tk), lambda i, j, kk: (i, kk)),
                  pl.BlockSpec((1, tk, tn), lambda i, j, kk: (l, kk, j)),
                  pl.BlockSpec((tm, tn), lambda i, j, kk: (i, j))],
        out_specs=pl.BlockSpec((tm, tn), lambda i, j, kk: (i, j)),
        scratch_shapes=[pltpu.VMEM((tm, tn), F32)],
        compiler_params=_cparams(3),
        name="matmul_residual",
    )(a, w, res)


def _merge_kernel(or_ref, om_ref, og_ref, wr_ref, wm_ref, wg_ref,
                  gr_ref, gm_ref, gg_ref, br_ref, bm_ref, bg_ref, o_ref):
    acc = _sigmoid(gr_ref[...].astype(F32) + br_ref[0]) * jnp.dot(
        or_ref[...], wr_ref[0, 0].astype(BF16), preferred_element_type=F32)
    acc += _sigmoid(gm_ref[...].astype(F32) + bm_ref[0]) * jnp.dot(
        om_ref[...], wm_ref[0, 0].astype(BF16), preferred_element_type=F32)
    acc += _sigmoid(gg_ref[...].astype(F32) + bg_ref[0]) * jnp.dot(
        og_ref[...], wg_ref[0, 0].astype(BF16), preferred_element_type=F32)
    o_ref[...] = acc.astype(BF16)


def _merge(o_r, o_m, o_g, w_branch, l, proj, gate_b, *, tm, tn):
    m = o_r.shape[0]
    gate_blk = C_GATE // tn
    per = D_MODEL // tn
    o_spec = pl.BlockSpec((tm, BRANCH_W), lambda i, j: (i, 0))

    def w_spec(b):
        return pl.BlockSpec((1, 1, BRANCH_W, tn), lambda i, j: (l, b, 0, j))

    def g_spec(b):
        return pl.BlockSpec((tm, tn), lambda i, j: (i, gate_blk + b * per + j))

    def b_spec(b):
        return pl.BlockSpec((1, 1, tn), lambda i, j: (b, 0, j))

    gate_b = gate_b.reshape(3, 1, D_MODEL)

    return pl.pallas_call(
        _merge_kernel,
        out_shape=jax.ShapeDtypeStruct((m, D_MODEL), BF16),
        grid=(m // tm, per),
        in_specs=[o_spec, o_spec, o_spec, w_spec(0), w_spec(1), w_spec(2),
                  g_spec(0), g_spec(1), g_spec(2), b_spec(0), b_spec(1), b_spec(2)],
        out_specs=pl.BlockSpec((tm, tn), lambda i, j: (i, j)),
        compiler_params=_cparams(2),
        name="gated_merge",
    )(o_r, o_m, o_g, w_branch, w_branch, w_branch, proj, proj, proj, gate_b, gate_b, gate_b)


def _rmsnorm_kernel(x_ref, g_ref, o_ref):
    x = x_ref[...]
    ms = jnp.mean(x * x, axis=-1, keepdims=True)
    o_ref[...] = x * lax.rsqrt(ms + EPS) * g_ref[...]


def _rmsnorm(x, g, *, tm):
    m, k = x.shape
    return pl.pallas_call(
        _rmsnorm_kernel,
        out_shape=jax.ShapeDtypeStruct((m, k), F32),
        grid=(m // tm,),
        in_specs=[pl.BlockSpec((tm, k), lambda i: (i, 0)),
                  pl.BlockSpec((1, k), lambda i: (0, 0))],
        out_specs=pl.BlockSpec((tm, k), lambda i: (i, 0)),
        compiler_params=_cparams(1),
        name="final_rmsnorm",
    )(x, g.reshape(1, k))


def _rwkv_prep_kernel(pr_ref, pk_ref, pv_ref, ps_ref, qr_ref, qk_ref, qv_ref, qs_ref,
                      mur_ref, muk_ref, muv_ref, mus_ref,
                      w0_ref, a0_ref, kk_ref, ka_ref, rk_ref,
                      w2_ref, a2_ref, g2_ref, j_ref,
                      r_out, k_out, v_out, kkn_out, b_out, ld_out, g_out, bon_out,
                      *scratch, tm, explicit_prev):
    def shift(p_ref, q_ref, mu_ref, scr):
        p = p_ref[0].astype(F32)
        if explicit_prev:
            prev = q_ref[0]
        else:
            @pl.when(pl.program_id(1) == 0)
            def _():
                scr[7:8, :] = q_ref[0]

            scr[8:8 + tm, :] = p
            prev = scr[7:7 + tm, :]
            scr[7:8, :] = p[tm - 1:tm, :]
        return p + (prev - p) * mu_ref[...]

    scr = scratch if scratch else (None,) * 4
    xr = shift(pr_ref, qr_ref, mur_ref, scr[0])
    xk = shift(pk_ref, qk_ref, muk_ref, scr[1])
    xv = shift(pv_ref, qv_ref, muv_ref, scr[2])
    xs = shift(ps_ref, qs_ref, mus_ref, scr[3])

    w = -_softplus(-(w0_ref[...] + _dot(jnp.tanh(xs), w2_ref[...]))) - 0.5
    ld_out[0] = -jnp.exp(w)
    a = _sigmoid(a0_ref[...] + _dot(xs, a2_ref[...]))
    g_out[0] = _dot(_sigmoid(xs), g2_ref[...])

    ones_bd = j_ref[...]
    kkr = xk * kk_ref[...]
    k2 = xk * (1.0 + (a - 1.0) * ka_ref[...])
    rkk = xr * k2 * rk_ref[...]
    for p in range(R_PAIRS):
        sl = slice(p * 128, (p + 1) * 128)
        kb = kkr[:, sl]
        nrm = jnp.sqrt(_segsum(kb * kb, ones_bd))
        kn = kb / jnp.maximum(nrm, 1e-12)
        kkn_out[0, :, sl] = kn
        b_out[0, :, sl] = kn * a[:, sl]
        bon_out[0, :, sl] = _segsum(rkk[:, sl], ones_bd) * xv[:, sl]
    r_out[0] = xr
    k_out[0] = k2
    v_out[0] = xv


def _rwkv_prep(proj3, small3, prev3, lw, ones_bd, *, tm):
    bsz, t, _ = proj3.shape
    explicit_prev = prev3.shape[1] == t
    tq = tm if explicit_prev else 1
    qmap = (lambda blk: (lambda bi, i: (bi, i, blk))) if explicit_prev else (
        lambda blk: (lambda bi, i: (bi, 0, blk)))
    small_blk = 3 * BRANCH_W // SMALL_W
    big = lambda blk: pl.BlockSpec((1, tm, BRANCH_W), lambda bi, i: (bi, i, blk))
    vec = lambda blk: pl.BlockSpec((1, BRANCH_W), lambda bi, i: (0, blk))
    full = lambda shape: pl.BlockSpec(shape, lambda bi, i: (0, 0))
    out = jax.ShapeDtypeStruct((bsz, t, BRANCH_W), F32)
    scratch = [] if explicit_prev else (
        [pltpu.VMEM((tm + 8, BRANCH_W), F32)] * 3 + [pltpu.VMEM((tm + 8, SMALL_W), F32)])
    return pl.pallas_call(
        functools.partial(_rwkv_prep_kernel, tm=tm, explicit_prev=explicit_prev),
        out_shape=[out] * 8,
        grid=(bsz, t // tm),
        in_specs=[big(0), big(1), big(2),
                  pl.BlockSpec((1, tm, SMALL_W), lambda bi, i: (bi, i, 0)),
                  pl.BlockSpec((1, tq, BRANCH_W), qmap(0)),
                  pl.BlockSpec((1, tq, BRANCH_W), qmap(1)),
                  pl.BlockSpec((1, tq, BRANCH_W), qmap(2)),
                  pl.BlockSpec((1, tq, SMALL_W), qmap(small_blk)),
                  vec(0), vec(1), vec(2),
                  pl.BlockSpec((1, SMALL_W), lambda bi, i: (0, small_blk)),
                  vec(0), vec(0), vec(0), vec(0), vec(0),
                  full((SMALL_W, BRANCH_W)), full((SMALL_W, BRANCH_W)), full((SMALL_W, BRANCH_W)),
                  full((128, 128))],
        out_specs=[pl.BlockSpec((1, tm, BRANCH_W), lambda bi, i: (bi, i, 0))] * 8,
        scratch_shapes=scratch,
        compiler_params=_cparams(2),
        name="rwkv_prep",
    )(proj3, proj3, proj3, small3, prev3, prev3, prev3, prev3,
      lw["mu_p"], lw["mu_p"], lw["mu_p"], lw["mu_p"],
      lw["w0"], lw["a0"], lw["k_k"], lw["k_a"], lw["r_k"],
      lw["w2p"], lw["a2p"], lw["g2p"], ones_bd)


def _rwkv_scan_kernel(r_ref, k_ref, v_ref, kk_ref, b_ref, ld_ref, g_ref, bon_ref,
                      lng_ref, lnb_ref, j_ref, s0_ref, *rest, L, group, n_earlier, single_step):
    earlier = rest[:n_earlier]
    o_ref, st_ref, s_ref = rest[n_earlier:]
    c_id = pl.program_id(1)
    bb = r_ref.shape[0]
    seqs = range(bb)

    @pl.when(c_id == 0)
    def _():
        z = jnp.zeros((R_HEAD, R_HEAD), F32)
        for bi in seqs:
            for p in range(R_PAIRS):
                top = jnp.concatenate([s0_ref[0, bi, 2 * p], z], axis=1)
                bot = jnp.concatenate([z, s0_ref[0, bi, 2 * p + 1]], axis=1)
                s_ref[bi * R_PAIRS + p] = jnp.concatenate([top, bot], axis=0)

    def rows(ref, bi):
        x = ref[bi]
        if x.shape[0] == L:
            return x
        assert x.shape[0] == 1
        return jnp.where(lax.broadcasted_iota(jnp.int32, (L, x.shape[1]), 0) == 0, x, 0.0)

    r_all = [rows(r_ref, bi) for bi in seqs]
    k_all = [rows(k_ref, bi) for bi in seqs]
    v_all = [rows(v_ref, bi) for bi in seqs]
    kk_all = [rows(kk_ref, bi) for bi in seqs]
    b_all = [rows(b_ref, bi) for bi in seqs]
    g_all = [rows(g_ref, bi) for bi in seqs]
    bon_all = [rows(bon_ref, bi) for bi in seqs]
    ld_all = [rows(ld_ref, bi) for bi in seqs]
    cs_all = [_cumsum_rows(x, single_step) for x in ld_all]
    ec_all = [jnp.exp(c) for c in cs_all]
    enc_all = [jnp.exp(-c) for c in cs_all]
    ecm_all = [jnp.exp(c - x) for c, x in zip(cs_all, ld_all)]
    c_last_all = [c[L - 1:L, :] for c in cs_all]
    e_tail_all = [jnp.exp(cl - c) for cl, c in zip(c_last_all, cs_all)]
    g_last_all = [jnp.exp(cl) for cl in c_last_all]

    lane = lax.broadcasted_iota(jnp.int32, (L, 128), 1)
    head_a = lane < R_HEAD

    def stack(x):
        return jnp.concatenate([jnp.where(head_a, x, 0.0), jnp.where(head_a, 0.0, x)], axis=0)

    P2 = 2 * L
    ri = lax.broadcasted_iota(jnp.int32, (P2, P2), 0)
    ci = lax.broadcasted_iota(jnp.int32, (P2, P2), 1)
    strict = ri > ci
    incl = ri >= ci
    eye = jnp.where(ri == ci, 1.0, 0.0)
    ones_bd = j_ref[...]
    inv_n = 1.0 / R_HEAD

    cat0 = lambda a, b: jnp.concatenate([a, b], axis=0)
    cat1 = lambda a, b: jnp.concatenate([a, b], axis=1)

    n_units = bb * R_PAIRS
    assert n_units % group == 0
    for g0 in range(0, n_units, group):
        units = list(range(g0, g0 + group))
        sq = [u // R_PAIRS for u in units]
        sls = [slice((u % R_PAIRS) * 128, (u % R_PAIRS + 1) * 128) for u in units]
        each = lambda f: [f(i) for i in range(group)]
        sel = lambda xs, i: xs[sq[i]][:, sls[i]]

        S = each(lambda i: s_ref[units[i]])
        Rs = each(lambda i: stack(sel(r_all, i) * sel(ec_all, i)))
        Bs = each(lambda i: stack(sel(kk_all, i) * sel(ecm_all, i)))
        Ks = each(lambda i: stack(sel(k_all, i) * sel(enc_all, i)))
        As = each(lambda i: stack(-(sel(b_all, i) * sel(enc_all, i))))
        Vs = each(lambda i: stack(sel(v_all, i)))
        Kt = each(lambda i: stack(sel(k_all, i) * sel(e_tail_all, i)))
        At = each(lambda i: stack(-(sel(b_all, i) * sel(e_tail_all, i))))

        if P2 % 128 == 0:
            sc = each(lambda i: _dot_nt(cat0(Bs[i], Rs[i]), cat0(As[i], Ks[i])))
            s_ba = each(lambda i: sc[i][:P2, :P2])
            s_bk = each(lambda i: sc[i][:P2, P2:])
            s_ra = each(lambda i: sc[i][P2:, :P2])
            s_rk = each(lambda i: sc[i][P2:, P2:])
        else:
            s_ba = each(lambda i: _dot_nt(Bs[i], As[i]))
            s_bk = each(lambda i: _dot_nt(Bs[i], Ks[i]))
            s_ra = each(lambda i: _dot_nt(Rs[i], As[i]))
            s_rk = each(lambda i: _dot_nt(Rs[i], Ks[i]))
        Nm = each(lambda i: jnp.where(strict, s_ba[i], 0.0))
        Mbk = each(lambda i: jnp.where(strict, s_bk[i], 0.0))
        Mra = each(lambda i: jnp.where(incl, s_ra[i], 0.0))
        Mrk = each(lambda i: jnp.where(incl, s_rk[i], 0.0))

        Tm = each(lambda i: eye + Nm[i])
        Pw = Nm
        span = 2
        while span < L and not single_step:
            Pw = [_dot(x, x) for x in Pw]
            Tm = each(lambda i: Tm[i] + _dot(Tm[i], Pw[i]))
            span *= 2

        mv = each(lambda i: _dot(cat0(Mbk[i], Mrk[i]), Vs[i]))
        tb = each(lambda i: _dot(Tm[i], cat1(Bs[i], mv[i][:P2])))
        mu = each(lambda i: _dot(Mra[i], tb[i]))
        Ro = each(lambda i: Rs[i] + mu[i][:, :128])
        uo = each(lambda i: _dot_nt(cat0(tb[i][:, :128], Ro[i]), S[i]))
        U = each(lambda i: uo[i][:P2] + tb[i][:, 128:])
        O = each(lambda i: uo[i][P2:] + mv[i][P2:] + mu[i][:, 128:])
        for i, u in enumerate(units):
            s_ref[u] = S[i] * sel(g_last_all, i) + _dot_tn(
                cat0(U[i], Vs[i]), cat0(At[i], Kt[i]))

        for i in range(group):
            sl = sls[i]
            out = O[i][:L] + O[i][L:]
            mean = _segsum(out, ones_bd) * inv_n
            d = out - mean
            var = _segsum(d * d, ones_bd) * inv_n
            y = d * lax.rsqrt(var + R_GN_EPS) * lng_ref[:, sl] + lnb_ref[:, sl]
            o_ref[sq[i], :, sl] = ((y + sel(bon_all, i)) * sel(g_all, i)).astype(BF16)

    @pl.when(c_id == pl.num_programs(1) - 1)
    def _():
        for bi in seqs:
            for p in range(R_PAIRS):
                sp = s_ref[bi * R_PAIRS + p]
                st_ref[n_earlier, bi, 2 * p] = sp[:R_HEAD, :R_HEAD]
                st_ref[n_earlier, bi, 2 * p + 1] = sp[R_HEAD:, R_HEAD:]
        for i, e_ref in enumerate(earlier):
            st_ref[i] = e_ref[0]


def _rwkv_scan(seqs, g, bonus, ln_g, ln_b, ones_bd, s0, l, earlier_s, *, L, bb):
    bsz, t, _ = seqs[0].shape
    tb = min(t, L)
    nc = -(-t // L)
    nl = len(earlier_s) + 1
    seq_spec = pl.BlockSpec((bb, tb, BRANCH_W), lambda bi, c: (bi, c, 0))
    vec_spec = pl.BlockSpec((1, BRANCH_W), lambda bi, c: (0, 0))
    st_ea = pl.BlockSpec((1, bb, R_HEADS, R_HEAD, R_HEAD), lambda bi, c: (0, bi, 0, 0, 0))
    return pl.pallas_call(
        functools.partial(_rwkv_scan_kernel, L=L, group=bb * R_PAIRS, n_earlier=nl - 1,
                          single_step=t == 1),
        out_shape=[jax.ShapeDtypeStruct((bsz, nc * L, BRANCH_W), BF16),
                   jax.ShapeDtypeStruct((nl, bsz, R_HEADS, R_HEAD, R_HEAD), F32)],
        grid=(bsz // bb, nc),
        in_specs=[seq_spec] * 8 + [
            vec_spec, vec_spec, pl.BlockSpec((128, 128), lambda bi, c: (0, 0)),
            pl.BlockSpec((1, bb, R_HEADS, R_HEAD, R_HEAD), lambda bi, c: (l, bi, 0, 0, 0))]
        + [st_ea] * (nl - 1),
        out_specs=[pl.BlockSpec((bb, L, BRANCH_W), lambda bi, c: (bi, c, 0)),
                   pl.BlockSpec((nl, bb, R_HEADS, R_HEAD, R_HEAD), lambda bi, c: (0, bi, 0, 0, 0))],
        scratch_shapes=[pltpu.VMEM((bb * R_PAIRS, 128, 128), F32)],
        compiler_params=_cparams(2),
        name="rwkv_scan",
    )(*seqs, g, bonus, ln_g, ln_b, ones_bd, s0, *earlier_s)


def _rwkv_mixer(proj3, small3, prev, s0, l, lw, ones_bd, earlier_s, *, L, tm, bb):
    bsz, t, _ = proj3.shape
    new_shift = jnp.concatenate(
        [proj3[:, -1, :3 * BRANCH_W].astype(F32), small3[:, -1, :R_COLS - 3 * BRANCH_W]], axis=-1)
    prev3 = jnp.pad(prev, ((0, 0), (0, 3 * BRANCH_W + SMALL_W - R_COLS)))[:, None, :]
    if t == 1:
        outs = _rwkv_prep(proj3.reshape(1, bsz, -1), small3.reshape(1, bsz, -1),
                          prev3.reshape(1, bsz, -1), lw, ones_bd, tm=tm)
        outs = [o.reshape(bsz, 1, BRANCH_W) for o in outs]
    else:
        outs = _rwkv_prep(proj3, small3, prev3, lw, ones_bd, tm=tm)
    r, k2, v, kkn, b, ld, g, bonus = outs
    o_r, s_new = _rwkv_scan((r, k2, v, kkn, b, ld), g, bonus, lw["ln_g"], lw["ln_b"],
                            ones_bd, s0, l, earlier_s, L=L, bb=bb)
    return o_r[:, :t].reshape(bsz * t, BRANCH_W), s_new, new_shift


def _mlstm_kernel(q_ref, k_ref, v_ref, o_ref, sm_ref, cq_ref, ck_ref, wq_ref, wk_ref,
                  bq_ref, bk_ref, ib_ref, fb_ref, ng_ref, c0_ref, n0_ref, m0_ref,
                  *rest, L, t_valid, n_earlier):
    earlier = rest[:n_earlier]
    out_ref, c_ref, n_ref, m_ref, qs_ref, ks_ref = rest[n_earlier:]
    last = n_earlier
    c_id = pl.program_id(1)

    @pl.when(c_id == 0)
    def _():
        for i, e_ref in enumerate(earlier):
            c_ref[i] = e_ref[0]
        c_ref[last] = c0_ref[0]
        n_ref[...] = n0_ref[0]
        m_ref[...] = m0_ref[0]
        qs_ref[:, 5:8, :] = cq_ref[0]
        ks_ref[:, 5:8, :] = ck_ref[0]

    bb = q_ref.shape[0]
    row = lax.broadcasted_iota(jnp.int32, (L, 1), 0)
    valid = (c_id * L + row) < t_valid
    lane = lax.broadcasted_iota(jnp.int32, (L, SMALL_W), 1)
    head_lane = lane < M_HEADS
    q_seq, k_seq, v_seq, gate_seq, F4_seq, ig4_seq, gmf4_seq = ([] for _ in range(7))
    for s_ in range(bb):
        qs_ref[s_, 8:8 + L, :] = _rows(q_ref, L, s_)
        ks_ref[s_, 8:8 + L, :] = _rows(k_ref, L, s_)
        conv_q = bq_ref[...]
        conv_k = bk_ref[...]
        for j in range(CONV_W):
            conv_q = conv_q + qs_ref[s_, 5 + j:5 + j + L, :] * wq_ref[j:j + 1, :]
            conv_k = conv_k + ks_ref[s_, 5 + j:5 + j + L, :] * wk_ref[j:j + 1, :]
        if L >= CONV_W - 1:
            tail_q = qs_ref[s_, 5 + L:8 + L, :]
            tail_k = ks_ref[s_, 5 + L:8 + L, :]
            qs_ref[s_, 5:8, :] = tail_q
            ks_ref[s_, 5:8, :] = tail_k
        q_seq.append(jnp.where(valid, _silu(conv_q), 0.0))
        k_seq.append(jnp.where(valid, _silu(conv_k) * (M_DK ** -0.5), 0.0))
        v_seq.append(jnp.where(valid, _rows(v_ref, L, s_), 0.0))
        gate_seq.append(_sigmoid(_rows(o_ref, L, s_)))
        sm = _rows(sm_ref, L, s_)
        i_pre = jnp.where(head_lane, pltpu.roll(sm, SMALL_W - S_I, axis=1), 0.0)
        f_pre = jnp.where(head_lane, pltpu.roll(sm, SMALL_W - S_F, axis=1), 0.0)
        ig4 = jnp.where(valid & head_lane, i_pre + ib_ref[...], NEG)
        lf4 = jnp.where(valid & head_lane, _log_sigmoid(f_pre + fb_ref[...]), 0.0)
        F4 = _cumsum_rows(lf4, t_valid == 1)
        F4_seq.append(F4)
        ig4_seq.append(ig4)
        gmf4_seq.append(ig4 - F4)

    ti = lax.broadcasted_iota(jnp.int32, (L, L), 0)
    si = lax.broadcasted_iota(jnp.int32, (L, L), 1)
    causal = si <= ti
    diag = ti == si
    ones_l = jnp.ones((L, L), F32)

    units = [(s_, h) for s_ in range(bb) for h in range(M_HEADS)]
    heads = range(len(units))
    us = [u[0] for u in units]
    uh = [u[1] for u in units]
    each = lambda f: [f(h) for h in heads]
    sls = [slice(uh[u] * M_DK, (uh[u] + 1) * M_DK) for u in heads]
    rowsum = lambda x: jnp.sum(x, axis=-1, keepdims=True)
    q = each(lambda h: q_seq[us[h]][:, sls[h]])
    k = each(lambda h: k_seq[us[h]][:, sls[h]])
    v = each(lambda h: v_seq[us[h]][:, sls[h]])
    F = each(lambda h: rowsum(jnp.where(lane == uh[h], F4_seq[us[h]], 0.0)))
    ig = each(lambda h: rowsum(jnp.where(lane == uh[h], ig4_seq[us[h]], 0.0)))
    gmf = each(lambda h: rowsum(jnp.where(lane == uh[h], gmf4_seq[us[h]], 0.0)))
    if t_valid == 1:
        g_row = each(lambda h: jnp.where(si == 0, gmf[h][0:1, :], NEG))
    else:
        g_row = each(lambda h: _dot_hi(
            ones_l, jnp.where(diag, jnp.broadcast_to(gmf[h], (L, L)), 0.0)))
    Dm = each(lambda h: jnp.where(causal, F[h] + g_row[h], NEG))

    C = each(lambda h: c_ref[last, us[h], uh[h]])
    n = each(lambda h: n_ref[us[h], uh[h]])
    m_prev = each(lambda h: m_ref[us[h], uh[h]])
    inter = each(lambda h: F[h] + m_prev[h])
    m_t = each(lambda h: jnp.maximum(inter[h], jnp.max(Dm[h], axis=-1, keepdims=True)))
    w_inter = each(lambda h: jnp.exp(inter[h] - m_t[h]))
    Sm = each(lambda h: _dot_nt(q[h], k[h]) * jnp.exp(Dm[h] - m_t[h]))
    num = each(lambda h: w_inter[h] * _dot(q[h], C[h]) + _dot(Sm[h], v[h]))
    den = each(lambda h: w_inter[h] * rowsum(q[h] * n[h]) + rowsum(Sm[h]))
    hh = each(lambda h: num[h] / jnp.maximum(jnp.abs(den[h]), jnp.exp(-m_t[h])))

    FL = each(lambda h: F[h][L - 1:L, :])
    g_s = each(lambda h: FL[h] - F[h] + ig[h])
    m_new = each(lambda h: jnp.maximum(FL[h] + m_prev[h], jnp.max(g_s[h], axis=0, keepdims=True)))
    a_c = each(lambda h: jnp.exp(FL[h] + m_prev[h] - m_new[h]))
    kw = each(lambda h: k[h] * jnp.exp(g_s[h] - m_new[h]))
    for h in heads:
        c_ref[last, us[h], uh[h]] = a_c[h] * C[h] + _dot_tn(kw[h], v[h])
        n_ref[us[h], uh[h]] = a_c[h] * n[h] + jnp.sum(kw[h], axis=0, keepdims=True)
        m_ref[us[h], uh[h]] = m_new[h]
        hn = hh[h] * lax.rsqrt(jnp.mean(hh[h] * hh[h], axis=-1, keepdims=True) + EPS)
        out_ref[us[h], :, sls[h]] = (gate_seq[us[h]][:, sls[h]] * hn
                                     * ng_ref[:, sls[h]]).astype(BF16)


def _rows(ref, L, seq=0):
    x = ref[seq].astype(F32)
    if x.shape[0] == L:
        return x
    assert x.shape[0] == 1
    row = lax.broadcasted_iota(jnp.int32, (L, x.shape[1]), 0)
    return jnp.where(row == 0, x, 0.0)


def _mlstm_mixer(proj3, small3, conv_buf, c0, n0, m0, l, lw, earlier_c, *, L, bb):
    bsz, t, _ = proj3.shape
    tb = min(t, L)
    nc = -(-t // L)
    nl = len(earlier_c) + 1
    seq = lambda col: pl.BlockSpec((bb, tb, BRANCH_W), lambda bi, c: (bi, c, col // BRANCH_W))
    st_c = pl.BlockSpec((nl, bb, M_HEADS, M_DK, M_DK), lambda bi, c: (0, bi, 0, 0, 0))
    ea_c = pl.BlockSpec((1, bb, M_HEADS, M_DK, M_DK), lambda bi, c: (0, bi, 0, 0, 0))
    st_n = pl.BlockSpec((bb, M_HEADS, 1, M_DK), lambda bi, c: (bi, 0, 0, 0))
    st_m = pl.BlockSpec((bb, M_HEADS, 1, 1), lambda bi, c: (bi, 0, 0, 0))
    in_c = pl.BlockSpec((1, bb, M_HEADS, M_DK, M_DK), lambda bi, c: (l, bi, 0, 0, 0))
    in_n = pl.BlockSpec((1, bb, M_HEADS, 1, M_DK), lambda bi, c: (l, bi, 0, 0, 0))
    in_m = pl.BlockSpec((1, bb, M_HEADS, 1, 1), lambda bi, c: (l, bi, 0, 0, 0))
    conv = lambda blk: pl.BlockSpec((1, bb, CONV_W - 1, BRANCH_W), lambda bi, c: (l, bi, 0, blk))
    cw = lambda blk: pl.BlockSpec((CONV_W, BRANCH_W), lambda bi, c: (0, blk))
    vec = lambda blk: pl.BlockSpec((1, BRANCH_W), lambda bi, c: (0, blk))
    hb = pl.BlockSpec((1, SMALL_W), lambda bi, c: (0, 0))
    pad_heads = lambda a: jnp.pad(a, ((0, 0), (0, SMALL_W - M_HEADS)))
    out, c_new, n_new, m_new = pl.pallas_call(
        functools.partial(_mlstm_kernel, L=L, t_valid=t, n_earlier=nl - 1),
        out_shape=[jax.ShapeDtypeStruct((bsz, nc * L, BRANCH_W), BF16),
                   jax.ShapeDtypeStruct((nl, bsz, M_HEADS, M_DK, M_DK), F32),
                   jax.ShapeDtypeStruct((bsz, M_HEADS, 1, M_DK), F32),
                   jax.ShapeDtypeStruct((bsz, M_HEADS, 1, 1), F32)],
        grid=(bsz // bb, nc),
        in_specs=[seq(C_MQK), seq(C_MQK + BRANCH_W), seq(C_MV), seq(C_MO),
                  pl.BlockSpec((bb, tb, SMALL_W), lambda bi, c: (bi, c, 0)),
                  conv(0), conv(1), cw(0), cw(1), vec(0), vec(1), hb, hb, vec(0),
                  in_c, in_n, in_m] + [ea_c] * (nl - 1),
        out_specs=[pl.BlockSpec((bb, L, BRANCH_W), lambda bi, c: (bi, c, 0)), st_c, st_n, st_m],
        scratch_shapes=[pltpu.VMEM((bb, L + 8, BRANCH_W), F32),
                        pltpu.VMEM((bb, L + 8, BRANCH_W), F32)],
        compiler_params=_cparams(2),
        name="mlstm_scan",
    )(proj3, proj3, proj3, proj3, small3, conv_buf, conv_buf,
      lw["conv_w"], lw["conv_w"], lw["conv_b"], lw["conv_b"],
      pad_heads(lw["i_b"]), pad_heads(lw["f_b"]),
      lw["m_norm_g"], c0, n0.reshape(DEPTH, bsz, M_HEADS, 1, M_DK),
      m0.reshape(DEPTH, bsz, M_HEADS, 1, 1), *earlier_c)
    return out, c_new, n_new.reshape(bsz, M_HEADS, M_DK), m_new.reshape(bsz, M_HEADS)


def _gla_kernel(q_ref, k_ref, v_ref, og_ref, sm_ref, a2_ref, ab_ref, ng_ref, s0_ref,
                *rest, L, t_valid, n_earlier):
    earlier = rest[:n_earlier]
    out_ref, s_ref, b_scr, q_scr = rest[n_earlier:]
    last = n_earlier
    c_id = pl.program_id(1)

    @pl.when(c_id == 0)
    def _():
        for i, e_ref in enumerate(earlier):
            s_ref[i] = e_ref[0]
        s_ref[last] = s0_ref[0]

    bb = q_ref.shape[0]
    row = lax.broadcasted_iota(jnp.int32, (L, 1), 0)
    valid = (c_id * L + row) < t_valid
    q_seq, k_seq, v_seq, gate_seq, lg_seq, b_seq, eb_seq, e_tail_seq = ([] for _ in range(8))
    for si_ in range(bb):
        q_s = jnp.where(valid, _rows(q_ref, L, si_) * (G_DK ** -0.5), 0.0)
        lg_s = _log_sigmoid(_dot(_rows(sm_ref, L, si_), a2_ref[...]) + ab_ref[...])
        lg_s = jnp.where(valid, lg_s * (1.0 / G_GATE_NORM), 0.0)
        b_s = _cumsum_rows(lg_s, t_valid == 1)
        b_scr[si_] = b_s
        q_scr[si_] = q_s
        q_seq.append(q_s)
        k_seq.append(jnp.where(valid, _rows(k_ref, L, si_), 0.0))
        v_seq.append(jnp.where(valid, _rows(v_ref, L, si_), 0.0))
        gate_seq.append(_silu(_rows(og_ref, L, si_)))
        lg_seq.append(lg_s)
        b_seq.append(b_s)
        eb_seq.append(jnp.exp(b_s))
        e_tail_seq.append(jnp.exp(b_s[L - 1:L, :] - b_s))
    ones_lv = jnp.ones((L, G_DV), F32)

    n_t = L if t_valid >= L else t_valid
    SB = GLA_SUB
    s_col = {rows: lax.broadcasted_iota(jnp.int32, (rows, 1), 0) for rows in (8, SB)}
    t_lane = {rows: lax.broadcasted_iota(jnp.int32, (rows, SB), 1) for rows in (8, SB)}

    heads = range(G_HEADS)
    sls = [slice(h * G_DK, (h + 1) * G_DK) for h in heads]
    svs = [slice(h * G_DV, (h + 1) * G_DV) for h in heads]

    units = [(s_, h) for s_ in range(bb) for h in heads]
    o_parts = [[] for _ in units]
    for r0 in range(0, L, SB):
        n_sub = max(0, min(SB, n_t - r0))
        for u, (s_, h) in enumerate(units):
            sl, sv = sls[h], svs[h]
            k_all, b_all, v_all = k_seq[s_], b_seq[s_], v_seq[s_]
            ki, bi, vi = k_all[r0:r0 + SB, sl], b_all[r0:r0 + SB, sl], v_all[r0:r0 + SB, sv]
            att = jnp.zeros((SB, SB), F32)
            for tl in range(n_sub):
                rows = 8 * (tl // 8 + 1)
                bt = b_scr[s_, r0 + tl:r0 + tl + 1, sl]
                qt = q_scr[s_, r0 + tl:r0 + tl + 1, sl]
                e = jnp.exp(jnp.where(s_col[rows] <= tl, bt - bi[:rows], NEG))
                col = jnp.sum(qt * ki[:rows] * e, axis=-1, keepdims=True)
                top = jnp.where(t_lane[rows] == tl, col, att[:rows])
                att = top if rows == SB else jnp.concatenate([top, att[rows:]], axis=0)
            o_i = _dot_tn(att, vi)
            if r0 > 0 and n_sub > 0:
                ref = b_scr[s_, r0 - 1:r0, sl]
                a_off = _dot_nt(q_seq[s_][r0:r0 + SB, sl] * jnp.exp(bi - ref),
                                k_all[:r0, sl] * jnp.exp(ref - b_all[:r0, sl]))
                o_i = o_i + _dot(a_off, v_all[:r0, sv])
            o_parts[u].append(o_i)

    for u, (s_, h) in enumerate(units):
        sl, sv = sls[h], svs[h]
        q, k, v = q_seq[s_][:, sl], k_seq[s_][:, sl], v_seq[s_][:, sv]
        o_intra = o_parts[u][0] if len(o_parts[u]) == 1 else jnp.concatenate(o_parts[u], axis=0)

        S = s_ref[last, s_, h]
        o = _dot(q * eb_seq[s_][:, sl], S) + o_intra
        decay = jnp.exp(lax.dot_general(lg_seq[s_][:, sl], ones_lv, (((0,), (0,)), ((), ())),
                                        precision=lax.Precision.HIGHEST,
                                        preferred_element_type=F32))
        s_ref[last, s_, h] = S * decay + _dot_tn(k * e_tail_seq[s_][:, sl], v)

        on = o * lax.rsqrt(jnp.mean(o * o, axis=-1, keepdims=True) + EPS) * ng_ref[:, sv]
        out_ref[s_, :, sv] = (on * gate_seq[s_][:, sv]).astype(BF16)


def _gla_mixer(proj3, small3, s0, l, lw, earlier_s, *, L, bb):
    bsz, t, _ = proj3.shape
    tb = min(t, L)
    nc = -(-t // L)
    nl = len(earlier_s) + 1
    gw = G_HEADS * G_DK
    st = pl.BlockSpec((nl, bb, G_HEADS, G_DK, G_DV), lambda bi, c: (0, bi, 0, 0, 0))
    st_ea = pl.BlockSpec((1, bb, G_HEADS, G_DK, G_DV), lambda bi, c: (0, bi, 0, 0, 0))
    st_in = pl.BlockSpec((1, bb, G_HEADS, G_DK, G_DV), lambda bi, c: (l, bi, 0, 0, 0))
    return pl.pallas_call(
        functools.partial(_gla_kernel, L=L, t_valid=t, n_earlier=nl - 1),
        out_shape=[jax.ShapeDtypeStruct((bsz, nc * L, BRANCH_W), BF16),
                   jax.ShapeDtypeStruct((nl, bsz, G_HEADS, G_DK, G_DV), F32)],
        grid=(bsz // bb, nc),
        in_specs=[pl.BlockSpec((bb, tb, gw), lambda bi, c: (bi, c, C_GQ // gw)),
                  pl.BlockSpec((bb, tb, gw), lambda bi, c: (bi, c, C_GQ // gw + 1)),
                  pl.BlockSpec((bb, tb, BRANCH_W), lambda bi, c: (bi, c, C_GV // BRANCH_W)),
                  pl.BlockSpec((bb, tb, BRANCH_W), lambda bi, c: (bi, c, C_GOG // BRANCH_W)),
                  pl.BlockSpec((bb, tb, SMALL_W), lambda bi, c: (bi, c, 0)),
                  pl.BlockSpec((SMALL_W, gw), lambda bi, c: (0, 0)),
                  pl.BlockSpec((1, gw), lambda bi, c: (0, 0)),
                  pl.BlockSpec((1, BRANCH_W), lambda bi, c: (0, 0)),
                  st_in] + [st_ea] * (nl - 1),
        out_specs=[pl.BlockSpec((bb, L, BRANCH_W), lambda bi, c: (bi, c, 0)), st],
        scratch_shapes=[pltpu.VMEM((bb, L, gw), F32), pltpu.VMEM((bb, L, gw), F32)],
        compiler_params=_cparams(2),
        name="gla_scan",
    )(proj3, proj3, proj3, proj3, small3, lw["g_a2p"], lw["g_a_b"], lw["g_norm_g"], s0,
      *earlier_s)


def _small_group_rows(wt):
    depth, _, k = wt.shape
    used = (R_COLS - 3072) + 2 * M_HEADS + G_LR
    return jnp.concatenate(
        [wt[:, W_R0 + 3072:W_R0 + 3264], wt[:, W_M0 + 3072:W_M0 + 3080],
         wt[:, W_G0 + 2048:W_G0 + 2064], jnp.zeros((depth, MAIN_TN - used, k), wt.dtype)], axis=1)


def _rows_padded(w, row0, total):
    return jnp.pad(w, ((row0, total - row0 - w.shape[0]), (0, 0)))


def _layer_weights(l, P):
    mu = P["rwkv_mu"][l]
    mu_p = jnp.concatenate([mu, jnp.zeros((3 * BRANCH_W + SMALL_W - R_COLS,), F32)]).reshape(1, -1)
    row = lambda a: a.reshape(1, -1)
    return {
        "norm1_g": P["norm1_g"][l], "gate_b": P["gate_b"][l],
        "mu_p": mu_p, "w0": row(P["rwkv_w0"][l]), "a0": row(P["rwkv_a0"][l]),
        "k_k": row(P["rwkv_k_k"][l]), "k_a": row(P["rwkv_k_a"][l]), "r_k": row(P["rwkv_r_k"][l]),
        "w2p": _rows_padded(P["rwkv_w2"][l], 0, SMALL_W),
        "a2p": _rows_padded(P["rwkv_a2"][l], R_LORA, SMALL_W),
        "g2p": _rows_padded(P["rwkv_g2"][l], 2 * R_LORA, SMALL_W),
        "ln_g": row(P["rwkv_ln_g"][l]), "ln_b": row(P["rwkv_ln_b"][l]),
        "conv_w": P["mlstm_conv_w"][l], "conv_b": row(P["mlstm_conv_b"][l]),
        "i_b": row(P["mlstm_i_b"][l]), "f_b": row(P["mlstm_f_b"][l]),
        "m_norm_g": row(P["mlstm_norm_g"][l]),
        "g_a2p": _rows_padded(P["gla_a2"][l], S_GXA, SMALL_W), "g_a_b": row(P["gla_a_b"][l]),
        "g_norm_g": row(P["gla_norm_g"][l]),
        "norm2_g": P["norm2_g"][l],
    }


def _layer(x2, bsz, t, states, l, lw, big, ones_bd, cfg, earlier):
    ea_wkv, ea_c, ea_s = earlier
    rw_prev, rw_s, m_conv, m_c, m_n, m_m, g_s = states
    m = bsz * t
    L, tm = cfg["L"], cfg["tm"]
    proj, small = _rms_matmul(x2, lw["norm1_g"], big["w_in_t"], big["w_in_t_small"], l, tm=tm,
                              main_dtype=cfg["proj_dtype"])
    proj3 = proj.reshape(bsz, t, C_SMALL)
    small3 = small.reshape(bsz, t, -1)

    o_r, rw_s_new, rw_prev_new = _rwkv_mixer(proj3, small3, rw_prev[l], rw_s, l, lw, ones_bd,
                                             ea_wkv, L=cfg["L_rwkv"], tm=cfg["tm_prep"],
                                             bb=cfg["bb_rwkv"])

    o_m, m_c_new, m_n_new, m_m_new = _mlstm_mixer(proj3, small3, m_conv, m_c, m_n, m_m, l, lw,
                                                  ea_c, L=L, bb=cfg["bb_mlstm"])
    qk_tail = proj3[:, -min(t, CONV_W - 1):, C_MQK:C_MQK + 2 * BRANCH_W].astype(F32)
    m_conv_new = jnp.concatenate([m_conv[l], qk_tail], axis=1)[:, -(CONV_W - 1):]
    o_g, g_s_new = _gla_mixer(proj3, small3, g_s, l, lw, ea_s, L=L, bb=cfg["bb_gla"])
    o_m = o_m[:, :t].reshape(m, BRANCH_W)
    o_g = o_g[:, :t].reshape(m, BRANCH_W)

    merged = _merge(o_r, o_m, o_g, big["w_branch"], l, proj, lw["gate_b"],
                    tm=cfg["tm_merge"], tn=512)
    x2 = _matmul_residual(merged, big["w_out"], l, x2, tm=tm, tn=512, tk=D_MODEL)
    hidden = _rms_swiglu(x2, lw["norm2_g"], big["w_gu"], l, tm=tm, tn=512)
    x2 = _matmul_residual(hidden, big["w_down"], l, x2, tm=tm, tn=512, tk=2816)
    return x2, (rw_prev_new, rw_s_new, m_conv_new, m_c_new, m_n_new, m_m_new, g_s_new)


def _trunk(x, states, layer_ws, big, final_g, ones_bd, cfg):
    bsz, t, d = x.shape
    x2 = x.reshape(bsz * t, d)
    per_layer = []
    for l in range(DEPTH):
        is_last = l == DEPTH - 1
        earlier = tuple([st[i] for st in per_layer] if is_last else [] for i in BIG_STATES)
        x2, new = _layer(x2, bsz, t, states, l, layer_ws[l], big, ones_bd, cfg, earlier)
        per_layer.append(new)
    new_states = [per_layer[-1][i] if i in BIG_STATES
                  else jnp.stack([st[i] for st in per_layer], axis=0) for i in range(len(states))]
    y = _rmsnorm(x2, final_g, tm=cfg["tm_norm"]).reshape(bsz, t, d)
    return y, new_states


BIG_STATES = (1, 3, 6)

PROMPT_CFG = dict(L=64, L_rwkv=64, bb_rwkv=1, bb_gla=1, bb_mlstm=2, tm=1024, tm_prep=256, tm_merge=1024, tm_norm=512, proj_dtype=BF16)
SAMPLE_CFG = dict(L=16, L_rwkv=16, bb_rwkv=4, bb_gla=4, bb_mlstm=4, tm=128, tm_prep=128, tm_merge=128, tm_norm=128, proj_dtype=F32)


def _zero_states(bsz):
    return (jnp.zeros((DEPTH, bsz, R_COLS), F32),
            jnp.zeros((DEPTH, bsz, R_HEADS, R_HEAD, R_HEAD), F32),
            jnp.zeros((DEPTH, bsz, CONV_W - 1, 2 * BRANCH_W), F32),
            jnp.zeros((DEPTH, bsz, M_HEADS, M_DK, M_DK), F32),
            jnp.zeros((DEPTH, bsz, M_HEADS, M_DK), F32),
            jnp.zeros((DEPTH, bsz, M_HEADS), F32),
            jnp.zeros((DEPTH, bsz, G_HEADS, G_DK, G_DV), F32))


def kernel(x_prompt, x_sample, state_rwkv_shift, state_rwkv_wkv, state_mlstm_conv, state_mlstm_C, state_mlstm_n, state_mlstm_m, state_gla_S, norm1_g, w_in, gate_b, rwkv_mu, rwkv_w0, rwkv_w2, rwkv_a0, rwkv_a2, rwkv_g2, rwkv_k_k, rwkv_k_a, rwkv_r_k, rwkv_ln_g, rwkv_ln_b, mlstm_conv_w, mlstm_conv_b, mlstm_i_b, mlstm_f_b, mlstm_norm_g, gla_a2, gla_a_b, gla_norm_g, w_branch, w_out, norm2_g, ffn_w_gu, ffn_w_down, final_norm_g):
    P = dict(norm1_g=norm1_g, w_in=w_in, gate_b=gate_b, rwkv_mu=rwkv_mu, rwkv_w0=rwkv_w0,
             rwkv_w2=rwkv_w2, rwkv_a0=rwkv_a0, rwkv_a2=rwkv_a2, rwkv_g2=rwkv_g2,
             rwkv_k_k=rwkv_k_k, rwkv_k_a=rwkv_k_a, rwkv_r_k=rwkv_r_k, rwkv_ln_g=rwkv_ln_g,
             rwkv_ln_b=rwkv_ln_b, mlstm_conv_w=mlstm_conv_w, mlstm_conv_b=mlstm_conv_b,
             mlstm_i_b=mlstm_i_b, mlstm_f_b=mlstm_f_b, mlstm_norm_g=mlstm_norm_g,
             gla_a2=gla_a2, gla_a_b=gla_a_b, gla_norm_g=gla_norm_g, w_branch=w_branch,
             w_out=w_out, norm2_g=norm2_g, ffn_w_gu=ffn_w_gu, ffn_w_down=ffn_w_down)
    layer_ws = [_layer_weights(l, P) for l in range(DEPTH)]
    w_in_t = jnp.swapaxes(w_in, 1, 2)
    big = dict(w_in_t=w_in_t, w_in_t_small=_small_group_rows(w_in_t), w_branch=w_branch,
               w_out=w_out, w_gu=ffn_w_gu, w_down=ffn_w_down)
    head_of_lane = jnp.arange(128) // R_HEAD
    ones_bd = (head_of_lane[:, None] == head_of_lane[None, :]).astype(BF16)

    y_p, p_states = _trunk(x_prompt, _zero_states(x_prompt.shape[0]), layer_ws, big,
                           final_norm_g, ones_bd, PROMPT_CFG)
    s_states = (state_rwkv_shift, state_rwkv_wkv, state_mlstm_conv, state_mlstm_C,
                state_mlstm_n, state_mlstm_m, state_gla_S)
    y_s, s_states = _trunk(x_sample, s_states, layer_ws, big, final_norm_g, ones_bd, SAMPLE_CFG)
    return (y_p, y_s, *p_states, *s_states)
```

```python
import functools

import jax
import jax.numpy as jnp
from jax import lax
from jax.experimental import pallas as pl
from jax.experimental.pallas import tpu as pltpu

F32 = jnp.float32
BF16 = jnp.bfloat16

D_MODEL = 2048
DEPTH = 2
BRANCH_W = 1024
R_HEADS, R_HEAD = 16, 64
R_PAIRS = R_HEADS // 2
R_LORA = 64
R_COLS = 3 * BRANCH_W + 3 * R_LORA
R_GN_EPS = 64e-5
M_HEADS, M_DK = 4, 256
CONV_W = 4
G_HEADS, G_DK, G_DV = 4, 128, 256
G_LR = 16
G_GATE_NORM = 16.0
D_FF = 5632
EPS = 1e-6
NEG = -1e30
GLA_SUB = 16

C_RWKV = 0
C_MQK = 3072
C_MV = 5120
C_MO = 6144
C_GQ = 7168
C_GV = 8192
C_GOG = 9216
C_GATE = 10240
C_SMALL = 16384
SMALL_W = 256
S_I, S_F, S_GXA = 192, 196, 200
MAIN_TN = 512
W_R0, W_M0, W_G0, W_T0 = 0, R_COLS, R_COLS + 4104, R_COLS + 4104 + 3088
MAIN_RUNS = ((C_RWKV, W_R0, 3072), (C_MQK, W_M0, 3072), (C_MO, W_M0 + 3080, 1024),
             (C_GQ, W_G0, 2048), (C_GOG, W_G0 + 2064, 1024 + 3 * D_MODEL))

VMEM_LIMIT = 56 * 1024 * 1024


def _cparams(n_axes):
    return pltpu.CompilerParams(dimension_semantics=("arbitrary",) * n_axes,
                                vmem_limit_bytes=VMEM_LIMIT)


def _dot(a, b):
    return jnp.dot(a.astype(BF16), b.astype(BF16), preferred_element_type=F32)


def _dot_nt(a, b):
    return lax.dot_general(a.astype(BF16), b.astype(BF16), (((1,), (1,)), ((), ())),
                           preferred_element_type=F32)


def _dot_tn(a, b):
    return lax.dot_general(a.astype(BF16), b.astype(BF16), (((0,), (0,)), ((), ())),
                           preferred_element_type=F32)


def _dot_hi(a, b):
    return jnp.dot(a, b, precision=lax.Precision.HIGHEST, preferred_element_type=F32)


def _cumsum_rows(x, single_step):
    n = x.shape[0]
    if single_step:
        row = lax.broadcasted_iota(jnp.int32, x.shape, 0)
        return jnp.where(row == 0, x, x[0:1, :])
    ti = lax.broadcasted_iota(jnp.int32, (n, n), 0)
    si = lax.broadcasted_iota(jnp.int32, (n, n), 1)
    return _dot_hi((si <= ti).astype(F32), x)


def _segsum(y, ones_blockdiag):
    hi = y.astype(BF16)
    lo = (y - hi.astype(F32)).astype(BF16)
    return (jnp.dot(hi, ones_blockdiag, preferred_element_type=F32)
            + jnp.dot(lo, ones_blockdiag, preferred_element_type=F32))


def _sigmoid(x):
    return 1.0 / (1.0 + jnp.exp(-x))


def _silu(x):
    return x * _sigmoid(x)


def _log_sigmoid(x):
    return -_softplus(-x)


def _softplus(x):
    return jnp.maximum(x, 0.0) + jnp.log(1.0 + jnp.exp(-jnp.abs(x)))


def _rms_mm_kernel(start_ref, x_ref, g_ref, wt_ref, wsm_ref, o_ref, osm_ref, h_scr, *, n_main):
    del start_ref
    j = pl.program_id(1)

    @pl.when(j == 0)
    def _():
        x = x_ref[...]
        ms = jnp.mean(x * x, axis=-1, keepdims=True)
        h_scr[...] = (x * lax.rsqrt(ms + EPS) * g_ref[...]).astype(BF16)

    @pl.when(j < n_main)
    def _():
        o_ref[...] = _dot_nt(h_scr[...], wt_ref[0]).astype(o_ref.dtype)

    @pl.when(j >= n_main)
    def _():
        osm_ref[...] = _dot_nt(h_scr[...], wsm_ref[0])


def _rms_matmul(x, g, wt, wt_small, l, *, tm, main_dtype):
    m, k = x.shape
    n_in = wt.shape[1]
    tn = MAIN_TN
    starts = []
    for p0, s0, width in MAIN_RUNS:
        assert p0 == len(starts) * tn and width % tn == 0 and s0 % 8 == 0
        starts += list(range(s0, s0 + width, tn))
    n_main = len(starts)
    assert n_main * tn == C_SMALL and starts[-1] + tn == n_in
    grid_spec = pltpu.PrefetchScalarGridSpec(
        num_scalar_prefetch=1,
        grid=(m // tm, n_main + 1),
        in_specs=[pl.BlockSpec((tm, k), lambda i, j, st: (i, 0)),
                  pl.BlockSpec((1, k), lambda i, j, st: (0, 0)),
                  pl.BlockSpec((pl.Element(1), pl.Element(tn), pl.Element(k)),
                               lambda i, j, st: (l, st[j] * 8, 0)),
                  pl.BlockSpec((1, tn, k), lambda i, j, st: (l, 0, 0))],
        out_specs=[pl.BlockSpec((tm, tn), lambda i, j, st: (i, jnp.minimum(j, n_main - 1))),
                   pl.BlockSpec((tm, tn), lambda i, j, st: (i, 0))],
        scratch_shapes=[pltpu.VMEM((tm, k), BF16)],
    )
    return pl.pallas_call(
        functools.partial(_rms_mm_kernel, n_main=n_main),
        out_shape=[jax.ShapeDtypeStruct((m, C_SMALL), main_dtype),
                   jax.ShapeDtypeStruct((m, tn), F32)],
        grid_spec=grid_spec,
        compiler_params=_cparams(2),
        name="rms_in_proj",
    )(jnp.asarray([s // 8 for s in starts] + [0], jnp.int32), x, g.reshape(1, k), wt, wt_small)


def _rms_swiglu_kernel(x_ref, g_ref, wg_ref, wu_ref, o_ref, h_scr):
    @pl.when(pl.program_id(1) == 0)
    def _():
        x = x_ref[...]
        ms = jnp.mean(x * x, axis=-1, keepdims=True)
        h_scr[...] = (x * lax.rsqrt(ms + EPS) * g_ref[...]).astype(BF16)

    h = h_scr[...]
    gg = jnp.dot(h, wg_ref[0].astype(BF16), preferred_element_type=F32)
    uu = jnp.dot(h, wu_ref[0].astype(BF16), preferred_element_type=F32)
    o_ref[...] = (_silu(gg) * uu).astype(BF16)


def _rms_swiglu(x, g, w_gu, l, *, tm, tn):
    m, k = x.shape
    nj = D_FF // tn
    return pl.pallas_call(
        _rms_swiglu_kernel,
        out_shape=jax.ShapeDtypeStruct((m, D_FF), BF16),
        grid=(m // tm, nj),
        in_specs=[pl.BlockSpec((tm, k), lambda i, j: (i, 0)),
                  pl.BlockSpec((1, k), lambda i, j: (0, 0)),
                  pl.BlockSpec((1, k, tn), lambda i, j: (l, 0, j)),
                  pl.BlockSpec((1, k, tn), lambda i, j: (l, 0, j + nj))],
        out_specs=pl.BlockSpec((tm, tn), lambda i, j: (i, j)),
        scratch_shapes=[pltpu.VMEM((tm, k), BF16)],
        compiler_params=_cparams(2),
        name="rms_ffn_swiglu",
    )(x, g.reshape(1, k), w_gu, w_gu)


def _mm_res_kernel(a_ref, w_ref, res_ref, o_ref, acc_ref, *, nk):
    kk = pl.program_id(2)

    @pl.when(kk == 0)
    def _():
        acc_ref[...] = jnp.zeros_like(acc_ref)

    acc_ref[...] += jnp.dot(a_ref[...], w_ref[0].astype(BF16), preferred_element_type=F32)

    @pl.when(kk == nk - 1)
    def _():
        o_ref[...] = acc_ref[...] + res_ref[...]


def _matmul_residual(a, w, l, res, *, tm, tn, tk):
    m, k = a.shape
    n = w.shape[2]
    nk = k // tk
    return pl.pallas_call(
        functools.partial(_mm_res_kernel, nk=nk),
        out_shape=jax.ShapeDtypeStruct((m, n), F32),
        grid=(m // tm, n // tn, nk),
        in_specs=[pl.BlockSpec((tm, tk), lambda i, j, kk: (i, kk)),
                  pl.BlockSpec((1, tk, tn), lambda i, j, kk: (l, kk, j)),
                  pl.BlockSpec((tm, tn), lambda i, j, kk: (i, j))],
        out_specs=pl.BlockSpec((tm, tn), lambda i, j, kk: (i, j)),
        scratch_shapes=[pltpu.VMEM((tm, tn), F32)],
        compiler_params=_cparams(3),
        name="matmul_residual",
    )(a, w, res)


def _merge_kernel(or_ref, om_ref, og_ref, wr_ref, wm_ref, wg_ref,
                  gr_ref, gm_ref, gg_ref, br_ref, bm_ref, bg_ref, o_ref):
    acc = _sigmoid(gr_ref[...].astype(F32) + br_ref[0]) * jnp.dot(
        or_ref[...], wr_ref[0, 0].astype(BF16), preferred_element_type=F32)
    acc += _sigmoid(gm_ref[...].astype(F32) + bm_ref[0]) * jnp.dot(
        om_ref[...], wm_ref[0, 0].astype(BF16), preferred_element_type=F32)
    acc += _sigmoid(gg_ref[...].astype(F32) + bg_ref[0]) * jnp.dot(
        og_ref[...], wg_ref[0, 0].astype(BF16), preferred_element_type=F32)
    o_ref[...] = acc.astype(BF16)


def _merge(o_r, o_m, o_g, w_branch, l, proj, gate_b, *, tm, tn):
    m = o_r.shape[0]
    gate_blk = C_GATE // tn
    per = D_MODEL // tn
    o_spec = pl.BlockSpec((tm, BRANCH_W), lambda i, j: (i, 0))

    def w_spec(b):
        return pl.BlockSpec((1, 1, BRANCH_W, tn), lambda i, j: (l, b, 0, j))

    def g_spec(b):
        return pl.BlockSpec((tm, tn), lambda i, j: (i, gate_blk + b * per + j))

    def b_spec(b):
        return pl.BlockSpec((1, 1, tn), lambda i, j: (b, 0, j))

    gate_b = gate_b.reshape(3, 1, D_MODEL)

    return pl.pallas_call(
        _merge_kernel,
        out_shape=jax.ShapeDtypeStruct((m, D_MODEL), BF16),
        grid=(m // tm, per),
        in_specs=[o_spec, o_spec, o_spec, w_spec(0), w_spec(1), w_spec(2),
                  g_spec(0), g_spec(1), g_spec(2), b_spec(0), b_spec(1), b_spec(2)],
        out_specs=pl.BlockSpec((tm, tn), lambda i, j: (i, j)),
        compiler_params=_cparams(2),
        name="gated_merge",
    )(o_r, o_m, o_g, w_branch, w_branch, w_branch, proj, proj, proj, gate_b, gate_b, gate_b)


def _rmsnorm_kernel(x_ref, g_ref, o_ref):
    x = x_ref[...]
    ms = jnp.mean(x * x, axis=-1, keepdims=True)
    o_ref[...] = x * lax.rsqrt(ms + EPS) * g_ref[...]


def _rmsnorm(x, g, *, tm):
    m, k = x.shape
    return pl.pallas_call(
        _rmsnorm_kernel,
        out_shape=jax.ShapeDtypeStruct((m, k), F32),
        grid=(m // tm,),
        in_specs=[pl.BlockSpec((tm, k), lambda i: (i, 0)),
                  pl.BlockSpec((1, k), lambda i: (0, 0))],
        out_specs=pl.BlockSpec((tm, k), lambda i: (i, 0)),
        compiler_params=_cparams(1),
        name="final_rmsnorm",
    )(x, g.reshape(1, k))


def _rwkv_prep_kernel(pr_ref, pk_ref, pv_ref, ps_ref, qr_ref, qk_ref, qv_ref, qs_ref,
                      mur_ref, muk_ref, muv_ref, mus_ref,
                      w0_ref, a0_ref, kk_ref, ka_ref, rk_ref,
                      w2_ref, a2_ref, g2_ref, j_ref,
                      r_out, k_out, v_out, kkn_out, b_out, ld_out, g_out, bon_out,
                      *scratch, tm, explicit_prev):
    def shift(p_ref, q_ref, mu_ref, scr):
        p = p_ref[0].astype(F32)
        if explicit_prev:
            prev = q_ref[0]
        else:
            @pl.when(pl.program_id(1) == 0)
            def _():
                scr[7:8, :] = q_ref[0]

            scr[8:8 + tm, :] = p
            prev = scr[7:7 + tm, :]
            scr[7:8, :] = p[tm - 1:tm, :]
        return p + (prev - p) * mu_ref[...]

    scr = scratch if scratch else (None,) * 4
    xr = shift(pr_ref, qr_ref, mur_ref, scr[0])
    xk = shift(pk_ref, qk_ref, muk_ref, scr[1])
    xv = shift(pv_ref, qv_ref, muv_ref, scr[2])
    xs = shift(ps_ref, qs_ref, mus_ref, scr[3])

    w = -_softplus(-(w0_ref[...] + _dot(jnp.tanh(xs), w2_ref[...]))) - 0.5
    ld_out[0] = -jnp.exp(w)
    a = _sigmoid(a0_ref[...] + _dot(xs, a2_ref[...]))
    g_out[0] = _dot(_sigmoid(xs), g2_ref[...])

    ones_bd = j_ref[...]
    kkr = xk * kk_ref[...]
    k2 = xk * (1.0 + (a - 1.0) * ka_ref[...])
    rkk = xr * k2 * rk_ref[...]
    for p in range(R_PAIRS):
        sl = slice(p * 128, (p + 1) * 128)
        kb = kkr[:, sl]
        nrm = jnp.sqrt(_segsum(kb * kb, ones_bd))
        kn = kb / jnp.maximum(nrm, 1e-12)
        kkn_out[0, :, sl] = kn
        b_out[0, :, sl] = kn * a[:, sl]
        bon_out[0, :, sl] = _segsum(rkk[:, sl], ones_bd) * xv[:, sl]
    r_out[0] = xr
    k_out[0] = k2
    v_out[0] = xv


def _rwkv_prep(proj3, small3, prev3, lw, ones_bd, *, tm):
    bsz, t, _ = proj3.shape
    explicit_prev = prev3.shape[1] == t
    tq = tm if explicit_prev else 1
    qmap = (lambda blk: (lambda bi, i: (bi, i, blk))) if explicit_prev else (
        lambda blk: (lambda bi, i: (bi, 0, blk)))
    small_blk = 3 * BRANCH_W // SMALL_W
    big = lambda blk: pl.BlockSpec((1, tm, BRANCH_W), lambda bi, i: (bi, i, blk))
    vec = lambda blk: pl.BlockSpec((1, BRANCH_W), lambda bi, i: (0, blk))
    full = lambda shape: pl.BlockSpec(shape, lambda bi, i: (0, 0))
    out = jax.ShapeDtypeStruct((bsz, t, BRANCH_W), F32)
    scratch = [] if explicit_prev else (
        [pltpu.VMEM((tm + 8, BRANCH_W), F32)] * 3 + [pltpu.VMEM((tm + 8, SMALL_W), F32)])
    return pl.pallas_call(
        functools.partial(_rwkv_prep_kernel, tm=tm, explicit_prev=explicit_prev),
        out_shape=[out] * 8,
        grid=(bsz, t // tm),
        in_specs=[big(0), big(1), big(2),
                  pl.BlockSpec((1, tm, SMALL_W), lambda bi, i: (bi, i, 0)),
                  pl.BlockSpec((1, tq, BRANCH_W), qmap(0)),
                  pl.BlockSpec((1, tq, BRANCH_W), qmap(1)),
                  pl.BlockSpec((1, tq, BRANCH_W), qmap(2)),
                  pl.BlockSpec((1, tq, SMALL_W), qmap(small_blk)),
                  vec(0), vec(1), vec(2),
                  pl.BlockSpec((1, SMALL_W), lambda bi, i: (0, small_blk)),
                  vec(0), vec(0), vec(0), vec(0), vec(0),
                  full((SMALL_W, BRANCH_W)), full((SMALL_W, BRANCH_W)), full((SMALL_W, BRANCH_W)),
                  full((128, 128))],
        out_specs=[pl.BlockSpec((1, tm, BRANCH_W), lambda bi, i: (bi, i, 0))] * 8,
        scratch_shapes=scratch,
        compiler_params=_cparams(2),
        name="rwkv_prep",
    )(proj3, proj3, proj3, small3, prev3, prev3, prev3, prev3,
      lw["mu_p"], lw["mu_p"], lw["mu_p"], lw["mu_p"],
      lw["w0"], lw["a0"], lw["k_k"], lw["k_a"], lw["r_k"],
      lw["w2p"], lw["a2p"], lw["g2p"], ones_bd)


def _rwkv_scan_kernel(r_ref, k_ref, v_ref, kk_ref, b_ref, ld_ref, g_ref, bon_ref,
                      lng_ref, lnb_ref, j_ref, s0_ref, *rest, L, group, n_earlier, single_step):
    earlier = rest[:n_earlier]
    o_ref, st_ref, s_ref = rest[n_earlier:]
    c_id = pl.program_id(1)
    bb = r_ref.shape[0]
    seqs = range(bb)

    @pl.when(c_id == 0)
    def _():
        z = jnp.zeros((R_HEAD, R_HEAD), F32)
        for bi in seqs:
            for p in range(R_PAIRS):
                top = jnp.concatenate([s0_ref[0, bi, 2 * p], z], axis=1)
                bot = jnp.concatenate([z, s0_ref[0, bi, 2 * p + 1]], axis=1)
                s_ref[bi * R_PAIRS + p] = jnp.concatenate([top, bot], axis=0)

    def rows(ref, bi):
        x = ref[bi]
        if x.shape[0] == L:
            return x
        assert x.shape[0] == 1
        return jnp.where(lax.broadcasted_iota(jnp.int32, (L, x.shape[1]), 0) == 0, x, 0.0)

    r_all = [rows(r_ref, bi) for bi in seqs]
    k_all = [rows(k_ref, bi) for bi in seqs]
    v_all = [rows(v_ref, bi) for bi in seqs]
    kk_all = [rows(kk_ref, bi) for bi in seqs]
    b_all = [rows(b_ref, bi) for bi in seqs]
    g_all = [rows(g_ref, bi) for bi in seqs]
    bon_all = [rows(bon_ref, bi) for bi in seqs]
    ld_all = [rows(ld_ref, bi) for bi in seqs]
    cs_all = [_cumsum_rows(x, single_step) for x in ld_all]
    ec_all = [jnp.exp(c) for c in cs_all]
    enc_all = [jnp.exp(-c) for c in cs_all]
    ecm_all = [jnp.exp(c - x) for c, x in zip(cs_all, ld_all)]
    c_last_all = [c[L - 1:L, :] for c in cs_all]
    e_tail_all = [jnp.exp(cl - c) for cl, c in zip(c_last_all, cs_all)]
    g_last_all = [jnp.exp(cl) for cl in c_last_all]

    lane = lax.broadcasted_iota(jnp.int32, (L, 128), 1)
    head_a = lane < R_HEAD

    def stack(x):
        return jnp.concatenate([jnp.where(head_a, x, 0.0), jnp.where(head_a, 0.0, x)], axis=0)

    P2 = 2 * L
    ri = lax.broadcasted_iota(jnp.int32, (P2, P2), 0)
    ci = lax.broadcasted_iota(jnp.int32, (P2, P2), 1)
    strict = ri > ci
    incl = ri >= ci
    eye = jnp.where(ri == ci, 1.0, 0.0)
    ones_bd = j_ref[...]
    inv_n = 1.0 / R_HEAD

    cat0 = lambda a, b: jnp.concatenate([a, b], axis=0)
    cat1 = lambda a, b: jnp.concatenate([a, b], axis=1)

    n_units = bb * R_PAIRS
    assert n_units % group == 0
    for g0 in range(0, n_units, group):
        units = list(range(g0, g0 + group))
        sq = [u // R_PAIRS for u in units]
        sls = [slice((u % R_PAIRS) * 128, (u % R_PAIRS + 1) * 128) for u in units]
        each = lambda f: [f(i) for i in range(group)]
        sel = lambda xs, i: xs[sq[i]][:, sls[i]]

        S = each(lambda i: s_ref[units[i]])
        Rs = each(lambda i: stack(sel(r_all, i) * sel(ec_all, i)))
        Bs = each(lambda i: stack(sel(kk_all, i) * sel(ecm_all, i)))
        Ks = each(lambda i: stack(sel(k_all, i) * sel(enc_all, i)))
        As = each(lambda i: stack(-(sel(b_all, i) * sel(enc_all, i))))
        Vs = each(lambda i: stack(sel(v_all, i)))
        Kt = each(lambda i: stack(sel(k_all, i) * sel(e_tail_all, i)))
        At = each(lambda i: stack(-(sel(b_all, i) * sel(e_tail_all, i))))

        if P2 % 128 == 0:
            sc = each(lambda i: _dot_nt(cat0(Bs[i], Rs[i]), cat0(As[i], Ks[i])))
            s_ba = each(lambda i: sc[i][:P2, :P2])
            s_bk = each(lambda i: sc[i][:P2, P2:])
            s_ra = each(lambda i: sc[i][P2:, :P2])
            s_rk = each(lambda i: sc[i][P2:, P2:])
        else:
            s_ba = each(lambda i: _dot_nt(Bs[i], As[i]))
            s_bk = each(lambda i: _dot_nt(Bs[i], Ks[i]))
            s_ra = each(lambda i: _dot_nt(Rs[i], As[i]))
            s_rk = each(lambda i: _dot_nt(Rs[i], Ks[i]))
        Nm = each(lambda i: jnp.where(strict, s_ba[i], 0.0))
        Mbk = each(lambda i: jnp.where(strict, s_bk[i], 0.0))
        Mra = each(lambda i: jnp.where(incl, s_ra[i], 0.0))
        Mrk = each(lambda i: jnp.where(incl, s_rk[i], 0.0))

        Tm = each(lambda i: eye + Nm[i])
        Pw = Nm
        span = 2
        while span < L and not single_step:
            Pw = [_dot(x, x) for x in Pw]
            Tm = each(lambda i: Tm[i] + _dot(Tm[i], Pw[i]))
            span *= 2

        mv = each(lambda i: _dot(cat0(Mbk[i], Mrk[i]), Vs[i]))
        tb = each(lambda i: _dot(Tm[i], cat1(Bs[i], mv[i][:P2])))
        mu = each(lambda i: _dot(Mra[i], tb[i]))
        Ro = each(lambda i: Rs[i] + mu[i][:, :128])
        uo = each(lambda i: _dot_nt(cat0(tb[i][:, :128], Ro[i]), S[i]))
        U = each(lambda i: uo[i][:P2] + tb[i][:, 128:])
        O = each(lambda i: uo[i][P2:] + mv[i][P2:] + mu[i][:, 128:])
        for i, u in enumerate(units):
            s_ref[u] = S[i] * sel(g_last_all, i) + _dot_tn(
                cat0(U[i], Vs[i]), cat0(At[i], Kt[i]))

        for i in range(group):
            sl = sls[i]
            out = O[i][:L] + O[i][L:]
            mean = _segsum(out, ones_bd) * inv_n
            d = out - mean
            var = _segsum(d * d, ones_bd) * inv_n
            y = d * lax.rsqrt(var + R_GN_EPS) * lng_ref[:, sl] + lnb_ref[:, sl]
            o_ref[sq[i], :, sl] = ((y + sel(bon_all, i)) * sel(g_all, i)).astype(BF16)

    @pl.when(c_id == pl.num_programs(1) - 1)
    def _():
        for bi in seqs:
            for p in range(R_PAIRS):
                sp = s_ref[bi * R_PAIRS + p]
                st_ref[n_earlier, bi, 2 * p] = sp[:R_HEAD, :R_HEAD]
                st_ref[n_earlier, bi, 2 * p + 1] = sp[R_HEAD:, R_HEAD:]
        for i, e_ref in enumerate(earlier):
            st_ref[i] = e_ref[0]


def _rwkv_scan(seqs, g, bonus, ln_g, ln_b, ones_bd, s0, l, earlier_s, *, L, bb):
    bsz, t, _ = seqs[0].shape
    tb = min(t, L)
    nc = -(-t // L)
    nl = len(earlier_s) + 1
    seq_spec = pl.BlockSpec((bb, tb, BRANCH_W), lambda bi, c: (bi, c, 0))
    vec_spec = pl.BlockSpec((1, BRANCH_W), lambda bi, c: (0, 0))
    st_ea = pl.BlockSpec((1, bb, R_HEADS, R_HEAD, R_HEAD), lambda bi, c: (0, bi, 0, 0, 0))
    return pl.pallas_call(
        functools.partial(_rwkv_scan_kernel, L=L, group=bb * R_PAIRS, n_earlier=nl - 1,
                          single_step=t == 1),
        out_shape=[jax.ShapeDtypeStruct((bsz, nc * L, BRANCH_W), BF16),
                   jax.ShapeDtypeStruct((nl, bsz, R_HEADS, R_HEAD, R_HEAD), F32)],
        grid=(bsz // bb, nc),
        in_specs=[seq_spec] * 8 + [
            vec_spec, vec_spec, pl.BlockSpec((128, 128), lambda bi, c: (0, 0)),
            pl.BlockSpec((1, bb, R_HEADS, R_HEAD, R_HEAD), lambda bi, c: (l, bi, 0, 0, 0))]
        + [st_ea] * (nl - 1),
        out_specs=[pl.BlockSpec((bb, L, BRANCH_W), lambda bi, c: (bi, c, 0)),
                   pl.BlockSpec((nl, bb, R_HEADS, R_HEAD, R_HEAD), lambda bi, c: (0, bi, 0, 0, 0))],
        scratch_shapes=[pltpu.VMEM((bb * R_PAIRS, 128, 128), F32)],
        compiler_params=_cparams(2),
        name="rwkv_scan",
    )(*seqs, g, bonus, ln_g, ln_b, ones_bd, s0, *earlier_s)


def _rwkv_mixer(proj3, small3, prev, s0, l, lw, ones_bd, earlier_s, *, L, tm, bb):
    bsz, t, _ = proj3.shape
    new_shift = jnp.concatenate(
        [proj3[:, -1, :3 * BRANCH_W].astype(F32), small3[:, -1, :R_COLS - 3 * BRANCH_W]], axis=-1)
    prev3 = jnp.pad(prev, ((0, 0), (0, 3 * BRANCH_W + SMALL_W - R_COLS)))[:, None, :]
    if t == 1:
        outs = _rwkv_prep(proj3.reshape(1, bsz, -1), small3.reshape(1, bsz, -1),
                          prev3.reshape(1, bsz, -1), lw, ones_bd, tm=tm)
        outs = [o.reshape(bsz, 1, BRANCH_W) for o in outs]
    else:
        outs = _rwkv_prep(proj3, small3, prev3, lw, ones_bd, tm=tm)
    r, k2, v, kkn, b, ld, g, bonus = outs
    o_r, s_new = _rwkv_scan((r, k2, v, kkn, b, ld), g, bonus, lw["ln_g"], lw["ln_b"],
                            ones_bd, s0, l, earlier_s, L=L, bb=bb)
    return o_r[:, :t].reshape(bsz * t, BRANCH_W), s_new, new_shift


def _mlstm_kernel(q_ref, k_ref, v_ref, o_ref, sm_ref, cq_ref, ck_ref, wq_ref, wk_ref,
                  bq_ref, bk_ref, ib_ref, fb_ref, ng_ref, c0_ref, n0_ref, m0_ref,
                  *rest, L, t_valid, n_earlier):
    earlier = rest[:n_earlier]
    out_ref, c_ref, n_ref, m_ref, qs_ref, ks_ref = rest[n_earlier:]
    last = n_earlier
    c_id = pl.program_id(1)

    @pl.when(c_id == 0)
    def _():
        for i, e_ref in enumerate(earlier):
            c_ref[i] = e_ref[0]
        c_ref[last] = c0_ref[0]
        n_ref[...] = n0_ref[0]
        m_ref[...] = m0_ref[0]
        qs_ref[:, 5:8, :] = cq_ref[0]
        ks_ref[:, 5:8, :] = ck_ref[0]

    bb = q_ref.shape[0]
    row = lax.broadcasted_iota(jnp.int32, (L, 1), 0)
    valid = (c_id * L + row) < t_valid
    lane = lax.broadcasted_iota(jnp.int32, (L, SMALL_W), 1)
    head_lane = lane < M_HEADS
    q_seq, k_seq, v_seq, gate_seq, F4_seq, ig4_seq, gmf4_seq = ([] for _ in range(7))
    for s_ in range(bb):
        qs_ref[s_, 8:8 + L, :] = _rows(q_ref, L, s_)
        ks_ref[s_, 8:8 + L, :] = _rows(k_ref, L, s_)
        conv_q = bq_ref[...]
        conv_k = bk_ref[...]
        for j in range(CONV_W):
            conv_q = conv_q + qs_ref[s_, 5 + j:5 + j + L, :] * wq_ref[j:j + 1, :]
            conv_k = conv_k + ks_ref[s_, 5 + j:5 + j + L, :] * wk_ref[j:j + 1, :]
        if L >= CONV_W - 1:
            tail_q = qs_ref[s_, 5 + L:8 + L, :]
            tail_k = ks_ref[s_, 5 + L:8 + L, :]
            qs_ref[s_, 5:8, :] = tail_q
            ks_ref[s_, 5:8, :] = tail_k
        q_seq.append(jnp.where(valid, _silu(conv_q), 0.0))
        k_seq.append(jnp.where(valid, _silu(conv_k) * (M_DK ** -0.5), 0.0))
        v_seq.append(jnp.where(valid, _rows(v_ref, L, s_), 0.0))
        gate_seq.append(_sigmoid(_rows(o_ref, L, s_)))
        sm = _rows(sm_ref, L, s_)
        i_pre = jnp.where(head_lane, pltpu.roll(sm, SMALL_W - S_I, axis=1), 0.0)
        f_pre = jnp.where(head_lane, pltpu.roll(sm, SMALL_W - S_F, axis=1), 0.0)
        ig4 = jnp.where(valid & head_lane, i_pre + ib_ref[...], NEG)
        lf4 = jnp.where(valid & head_lane, _log_sigmoid(f_pre + fb_ref[...]), 0.0)
        F4 = _cumsum_rows(lf4, t_valid == 1)
        F4_seq.append(F4)
        ig4_seq.append(ig4)
        gmf4_seq.append(ig4 - F4)

    ti = lax.broadcasted_iota(jnp.int32, (L, L), 0)
    si = lax.broadcasted_iota(jnp.int32, (L, L), 1)
    causal = si <= ti
    diag = ti == si
    ones_l = jnp.ones((L, L), F32)

    units = [(s_, h) for s_ in range(bb) for h in range(M_HEADS)]
    heads = range(len(units))
    us = [u[0] for u in units]
    uh = [u[1] for u in units]
    each = lambda f: [f(h) for h in heads]
    sls = [slice(uh[u] * M_DK, (uh[u] + 1) * M_DK) for u in heads]
    rowsum = lambda x: jnp.sum(x, axis=-1, keepdims=True)
    q = each(lambda h: q_seq[us[h]][:, sls[h]])
    k = each(lambda h: k_seq[us[h]][:, sls[h]])
    v = each(lambda h: v_seq[us[h]][:, sls[h]])
    F = each(lambda h: rowsum(jnp.where(lane == uh[h], F4_seq[us[h]], 0.0)))
    ig = each(lambda h: rowsum(jnp.where(lane == uh[h], ig4_seq[us[h]], 0.0)))
    gmf = each(lambda h: rowsum(jnp.where(lane == uh[h], gmf4_seq[us[h]], 0.0)))
    if t_valid == 1:
        g_row = each(lambda h: jnp.where(si == 0, gmf[h][0:1, :], NEG))
    else:
        g_row = each(lambda h: _dot_hi(
            ones_l, jnp.where(diag, jnp.broadcast_to(gmf[h], (L, L)), 0.0)))
    Dm = each(lambda h: jnp.where(causal, F[h] + g_row[h], NEG))

    C = each(lambda h: c_ref[last, us[h], uh[h]])
    n = each(lambda h: n_ref[us[h], uh[h]])
    m_prev = each(lambda h: m_ref[us[h], uh[h]])
    inter = each(lambda h: F[h] + m_prev[h])
    m_t = each(lambda h: jnp.maximum(inter[h], jnp.max(Dm[h], axis=-1, keepdims=True)))
    w_inter = each(lambda h: jnp.exp(inter[h] - m_t[h]))
    Sm = each(lambda h: _dot_nt(q[h], k[h]) * jnp.exp(Dm[h] - m_t[h]))
    num = each(lambda h: w_inter[h] * _dot(q[h], C[h]) + _dot(Sm[h], v[h]))
    den = each(lambda h: w_inter[h] * rowsum(q[h] * n[h]) + rowsum(Sm[h]))
    hh = each(lambda h: num[h] / jnp.maximum(jnp.abs(den[h]), jnp.exp(-m_t[h])))

    FL = each(lambda h: F[h][L - 1:L, :])
    g_s = each(lambda h: FL[h] - F[h] + ig[h])
    m_new = each(lambda h: jnp.maximum(FL[h] + m_prev[h], jnp.max(g_s[h], axis=0, keepdims=True)))
    a_c = each(lambda h: jnp.exp(FL[h] + m_prev[h] - m_new[h]))
    kw = each(lambda h: k[h] * jnp.exp(g_s[h] - m_new[h]))
    for h in heads:
        c_ref[last, us[h], uh[h]] = a_c[h] * C[h] + _dot_tn(kw[h], v[h])
        n_ref[us[h], uh[h]] = a_c[h] * n[h] + jnp.sum(kw[h], axis=0, keepdims=True)
        m_ref[us[h], uh[h]] = m_new[h]
        hn = hh[h] * lax.rsqrt(jnp.mean(hh[h] * hh[h], axis=-1, keepdims=True) + EPS)
        out_ref[us[h], :, sls[h]] = (gate_seq[us[h]][:, sls[h]] * hn
                                     * ng_ref[:, sls[h]]).astype(BF16)


def _rows(ref, L, seq=0):
    x = ref[seq].astype(F32)
    if x.shape[0] == L:
        return x
    assert x.shape[0] == 1
    row = lax.broadcasted_iota(jnp.int32, (L, x.shape[1]), 0)
    return jnp.where(row == 0, x, 0.0)


def _mlstm_mixer(proj3, small3, conv_buf, c0, n0, m0, l, lw, earlier_c, *, L, bb):
    bsz, t, _ = proj3.shape
    tb = min(t, L)
    nc = -(-t // L)
    nl = len(earlier_c) + 1
    seq = lambda col: pl.BlockSpec((bb, tb, BRANCH_W), lambda bi, c: (bi, c, col // BRANCH_W))
    st_c = pl.BlockSpec((nl, bb, M_HEADS, M_DK, M_DK), lambda bi, c: (0, bi, 0, 0, 0))
    ea_c = pl.BlockSpec((1, bb, M_HEADS, M_DK, M_DK), lambda bi, c: (0, bi, 0, 0, 0))
    st_n = pl.BlockSpec((bb, M_HEADS, 1, M_DK), lambda bi, c: (bi, 0, 0, 0))
    st_m = pl.BlockSpec((bb, M_HEADS, 1, 1), lambda bi, c: (bi, 0, 0, 0))
    in_c = pl.BlockSpec((1, bb, M_HEADS, M_DK, M_DK), lambda bi, c: (l, bi, 0, 0, 0))
    in_n = pl.BlockSpec((1, bb, M_HEADS, 1, M_DK), lambda bi, c: (l, bi, 0, 0, 0))
    in_m = pl.BlockSpec((1, bb, M_HEADS, 1, 1), lambda bi, c: (l, bi, 0, 0, 0))
    conv = lambda blk: pl.BlockSpec((1, bb, CONV_W - 1, BRANCH_W), lambda bi, c: (l, bi, 0, blk))
    cw = lambda blk: pl.BlockSpec((CONV_W, BRANCH_W), lambda bi, c: (0, blk))
    vec = lambda blk: pl.BlockSpec((1, BRANCH_W), lambda bi, c: (0, blk))
    hb = pl.BlockSpec((1, SMALL_W), lambda bi, c: (0, 0))
    pad_heads = lambda a: jnp.pad(a, ((0, 0), (0, SMALL_W - M_HEADS)))
    out, c_new, n_new, m_new = pl.pallas_call(
        functools.partial(_mlstm_kernel, L=L, t_valid=t, n_earlier=nl - 1),
        out_shape=[jax.ShapeDtypeStruct((bsz, nc * L, BRANCH_W), BF16),
                   jax.ShapeDtypeStruct((nl, bsz, M_HEADS, M_DK, M_DK), F32),
                   jax.ShapeDtypeStruct((bsz, M_HEADS, 1, M_DK), F32),
                   jax.ShapeDtypeStruct((bsz, M_HEADS, 1, 1), F32)],
        grid=(bsz // bb, nc),
        in_specs=[seq(C_MQK), seq(C_MQK + BRANCH_W), seq(C_MV), seq(C_MO),
                  pl.BlockSpec((bb, tb, SMALL_W), lambda bi, c: (bi, c, 0)),
                  conv(0), conv(1), cw(0), cw(1), vec(0), vec(1), hb, hb, vec(0),
                  in_c, in_n, in_m] + [ea_c] * (nl - 1),
        out_specs=[pl.BlockSpec((bb, L, BRANCH_W), lambda bi, c: (bi, c, 0)), st_c, st_n, st_m],
        scratch_shapes=[pltpu.VMEM((bb, L + 8, BRANCH_W), F32),
                        pltpu.VMEM((bb, L + 8, BRANCH_W), F32)],
        compiler_params=_cparams(2),
        name="mlstm_scan",
    )(proj3, proj3, proj3, proj3, small3, conv_buf, conv_buf,
      lw["conv_w"], lw["conv_w"], lw["conv_b"], lw["conv_b"],
      pad_heads(lw["i_b"]), pad_heads(lw["f_b"]),
      lw["m_norm_g"], c0, n0.reshape(DEPTH, bsz, M_HEADS, 1, M_DK),
      m0.reshape(DEPTH, bsz, M_HEADS, 1, 1), *earlier_c)
    return out, c_new, n_new.reshape(bsz, M_HEADS, M_DK), m_new.reshape(bsz, M_HEADS)


def _gla_kernel(q_ref, k_ref, v_ref, og_ref, sm_ref, a2_ref, ab_ref, ng_ref, s0_ref,
                *rest, L, t_valid, n_earlier):
    earlier = rest[:n_earlier]
    out_ref, s_ref, b_scr, q_scr = rest[n_earlier:]
    last = n_earlier
    c_id = pl.program_id(1)

    @pl.when(c_id == 0)
    def _():
        for i, e_ref in enumerate(earlier):
            s_ref[i] = e_ref[0]
        s_ref[last] = s0_ref[0]

    bb = q_ref.shape[0]
    row = lax.broadcasted_iota(jnp.int32, (L, 1), 0)
    valid = (c_id * L + row) < t_valid
    q_seq, k_seq, v_seq, gate_seq, lg_seq, b_seq, eb_seq, e_tail_seq = ([] for _ in range(8))
    for si_ in range(bb):
        q_s = jnp.where(valid, _rows(q_ref, L, si_) * (G_DK ** -0.5), 0.0)
        lg_s = _log_sigmoid(_dot(_rows(sm_ref, L, si_), a2_ref[...]) + ab_ref[...])
        lg_s = jnp.where(valid, lg_s * (1.0 / G_GATE_NORM), 0.0)
        b_s = _cumsum_rows(lg_s, t_valid == 1)
        b_scr[si_] = b_s
        q_scr[si_] = q_s
        q_seq.append(q_s)
        k_seq.append(jnp.where(valid, _rows(k_ref, L, si_), 0.0))
        v_seq.append(jnp.where(valid, _rows(v_ref, L, si_), 0.0))
        gate_seq.append(_silu(_rows(og_ref, L, si_)))
        lg_seq.append(lg_s)
        b_seq.append(b_s)
        eb_seq.append(jnp.exp(b_s))
        e_tail_seq.append(jnp.exp(b_s[L - 1:L, :] - b_s))
    ones_lv = jnp.ones((L, G_DV), F32)

    n_t = L if t_valid >= L else t_valid
    SB = GLA_SUB
    s_col = {rows: lax.broadcasted_iota(jnp.int32, (rows, 1), 0) for rows in (8, SB)}
    t_lane = {rows: lax.broadcasted_iota(jnp.int32, (rows, SB), 1) for rows in (8, SB)}

    heads = range(G_HEADS)
    sls = [slice(h * G_DK, (h + 1) * G_DK) for h in heads]
    svs = [slice(h * G_DV, (h + 1) * G_DV) for h in heads]

    units = [(s_, h) for s_ in range(bb) for h in heads]
    o_parts = [[] for _ in units]
    for r0 in range(0, L, SB):
        n_sub = max(0, min(SB, n_t - r0))
        for u, (s_, h) in enumerate(units):
            sl, sv = sls[h], svs[h]
            k_all, b_all, v_all = k_seq[s_], b_seq[s_], v_seq[s_]
            ki, bi, vi = k_all[r0:r0 + SB, sl], b_all[r0:r0 + SB, sl], v_all[r0:r0 + SB, sv]
            att = jnp.zeros((SB, SB), F32)
            for tl in range(n_sub):
                rows = 8 * (tl // 8 + 1)
                bt = b_scr[s_, r0 + tl:r0 + tl + 1, sl]
                qt = q_scr[s_, r0 + tl:r0 + tl + 1, sl]
                e = jnp.exp(jnp.where(s_col[rows] <= tl, bt - bi[:rows], NEG))
                col = jnp.sum(qt * ki[:rows] * e, axis=-1, keepdims=True)
                top = jnp.where(t_lane[rows] == tl, col, att[:rows])
                att = top if rows == SB else jnp.concatenate([top, att[rows:]], axis=0)
            o_i = _dot_tn(att, vi)
            if r0 > 0 and n_sub > 0:
                ref = b_scr[s_, r0 - 1:r0, sl]
                a_off = _dot_nt(q_seq[s_][r0:r0 + SB, sl] * jnp.exp(bi - ref),
                                k_all[:r0, sl] * jnp.exp(ref - b_all[:r0, sl]))
                o_i = o_i + _dot(a_off, v_all[:r0, sv])
            o_parts[u].append(o_i)

    for u, (s_, h) in enumerate(units):
        sl, sv = sls[h], svs[h]
        q, k, v = q_seq[s_][:, sl], k_seq[s_][:, sl], v_seq[s_][:, sv]
        o_intra = o_parts[u][0] if len(o_parts[u]) == 1 else jnp.concatenate(o_parts[u], axis=0)

        S = s_ref[last, s_, h]
        o = _dot(q * eb_seq[s_][:, sl], S) + o_intra
        decay = jnp.exp(lax.dot_general(lg_seq[s_][:, sl], ones_lv, (((0,), (0,)), ((), ())),
                                        precision=lax.Precision.HIGHEST,
                                        preferred_element_type=F32))
        s_ref[last, s_, h] = S * decay + _dot_tn(k * e_tail_seq[s_][:, sl], v)

        on = o * lax.rsqrt(jnp.mean(o * o, axis=-1, keepdims=True) + EPS) * ng_ref[:, sv]
        out_ref[s_, :, sv] = (on * gate_seq[s_][:, sv]).astype(BF16)


def _gla_mixer(proj3, small3, s0, l, lw, earlier_s, *, L, bb):
    bsz, t, _ = proj3.shape
    tb = min(t, L)
    nc = -(-t // L)
    nl = len(earlier_s) + 1
    gw = G_HEADS * G_DK
    st = pl.BlockSpec((nl, bb, G_HEADS, G_DK, G_DV), lambda bi, c: (0, bi, 0, 0, 0))
    st_ea = pl.BlockSpec((1, bb, G_HEADS, G_DK, G_DV), lambda bi, c: (0, bi, 0, 0, 0))
    st_in = pl.BlockSpec((1, bb, G_HEADS, G_DK, G_DV), lambda bi, c: (l, bi, 0, 0, 0))
    return pl.pallas_call(
        functools.partial(_gla_kernel, L=L, t_valid=t, n_earlier=nl - 1),
        out_shape=[jax.ShapeDtypeStruct((bsz, nc * L, BRANCH_W), BF16),
                   jax.ShapeDtypeStruct((nl, bsz, G_HEADS, G_DK, G_DV), F32)],
        grid=(bsz // bb, nc),
        in_specs=[pl.BlockSpec((bb, tb, gw), lambda bi, c: (bi, c, C_GQ // gw)),
                  pl.BlockSpec((bb, tb, gw), lambda bi, c: (bi, c, C_GQ // gw + 1)),
                  pl.BlockSpec((bb, tb, BRANCH_W), lambda bi, c: (bi, c, C_GV // BRANCH_W)),
                  pl.BlockSpec((bb, tb, BRANCH_W), lambda bi, c: (bi, c, C_GOG // BRANCH_W)),
                  pl.BlockSpec((bb, tb, SMALL_W), lambda bi, c: (bi, c, 0)),
                  pl.BlockSpec((SMALL_W, gw), lambda bi, c: (0, 0)),
                  pl.BlockSpec((1, gw), lambda bi, c: (0, 0)),
                  pl.BlockSpec((1, BRANCH_W), lambda bi, c: (0, 0)),
                  st_in] + [st_ea] * (nl - 1),
        out_specs=[pl.BlockSpec((bb, L, BRANCH_W), lambda bi, c: (bi, c, 0)), st],
        scratch_shapes=[pltpu.VMEM((bb, L, gw), F32), pltpu.VMEM((bb, L, gw), F32)],
        compiler_params=_cparams(2),
        name="gla_scan",
    )(proj3, proj3, proj3, proj3, small3, lw["g_a2p"], lw["g_a_b"], lw["g_norm_g"], s0,
      *earlier_s)


def _small_group_rows(wt):
    depth, _, k = wt.shape
    used = (R_COLS - 3072) + 2 * M_HEADS + G_LR
    return jnp.concatenate(
        [wt[:, W_R0 + 3072:W_R0 + 3264], wt[:, W_M0 + 3072:W_M0 + 3080],
         wt[:, W_G0 + 2048:W_G0 + 2064], jnp.zeros((depth, MAIN_TN - used, k), wt.dtype)], axis=1)


def _rows_padded(w, row0, total):
    return jnp.pad(w, ((row0, total - row0 - w.shape[0]), (0, 0)))


def _layer_weights(l, P):
    mu = P["rwkv_mu"][l]
    mu_p = jnp.concatenate([mu, jnp.zeros((3 * BRANCH_W + SMALL_W - R_COLS,), F32)]).reshape(1, -1)
    row = lambda a: a.reshape(1, -1)
    return {
        "norm1_g": P["norm1_g"][l], "gate_b": P["gate_b"][l],
        "mu_p": mu_p, "w0": row(P["rwkv_w0"][l]), "a0": row(P["rwkv_a0"][l]),
        "k_k": row(P["rwkv_k_k"][l]), "k_a": row(P["rwkv_k_a"][l]), "r_k": row(P["rwkv_r_k"][l]),
        "w2p": _rows_padded(P["rwkv_w2"][l], 0, SMALL_W),
        "a2p": _rows_padded(P["rwkv_a2"][l], R_LORA, SMALL_W),
        "g2p": _rows_padded(P["rwkv_g2"][l], 2 * R_LORA, SMALL_W),
        "ln_g": row(P["rwkv_ln_g"][l]), "ln_b": row(P["rwkv_ln_b"][l]),
        "conv_w": P["mlstm_conv_w"][l], "conv_b": row(P["mlstm_conv_b"][l]),
        "i_b": row(P["mlstm_i_b"][l]), "f_b": row(P["mlstm_f_b"][l]),
        "m_norm_g": row(P["mlstm_norm_g"][l]),
        "g_a2p": _rows_padded(P["gla_a2"][l], S_GXA, SMALL_W), "g_a_b": row(P["gla_a_b"][l]),
        "g_norm_g": row(P["gla_norm_g"][l]),
        "norm2_g": P["norm2_g"][l],
    }


def _layer(x2, bsz, t, states, l, lw, big, ones_bd, cfg, earlier):
    ea_wkv, ea_c, ea_s = earlier
    rw_prev, rw_s, m_conv, m_c, m_n, m_m, g_s = states
    m = bsz * t
    L, tm = cfg["L"], cfg["tm"]
    proj, small = _rms_matmul(x2, lw["norm1_g"], big["w_in_t"], big["w_in_t_small"], l, tm=tm,
                              main_dtype=cfg["proj_dtype"])
    proj3 = proj.reshape(bsz, t, C_SMALL)
    small3 = small.reshape(bsz, t, -1)

    o_r, rw_s_new, rw_prev_new = _rwkv_mixer(proj3, small3, rw_prev[l], rw_s, l, lw, ones_bd,
                                             ea_wkv, L=cfg["L_rwkv"], tm=cfg["tm_prep"],
                                             bb=cfg["bb_rwkv"])

    o_m, m_c_new, m_n_new, m_m_new = _mlstm_mixer(proj3, small3, m_conv, m_c, m_n, m_m, l, lw,
                                                  ea_c, L=L, bb=cfg["bb_mlstm"])
    qk_tail = proj3[:, -min(t, CONV_W - 1):, C_MQK:C_MQK + 2 * BRANCH_W].astype(F32)
    m_conv_new = jnp.concatenate([m_conv[l], qk_tail], axis=1)[:, -(CONV_W - 1):]
    o_g, g_s_new = _gla_mixer(proj3, small3, g_s, l, lw, ea_s, L=L, bb=cfg["bb_gla"])
    o_m = o_m[:, :t].reshape(m, BRANCH_W)
    o_g = o_g[:, :t].reshape(m, BRANCH_W)

    merged = _merge(o_r, o_m, o_g, big["w_branch"], l, proj, lw["gate_b"],
                    tm=cfg["tm_merge"], tn=512)
    x2 = _matmul_residual(merged, big["w_out"], l, x2, tm=tm, tn=512, tk=D_MODEL)
    hidden = _rms_swiglu(x2, lw["norm2_g"], big["w_gu"], l, tm=tm, tn=512)
    x2 = _matmul_residual(hidden, big["w_down"], l, x2, tm=tm, tn=512, tk=2816)
    return x2, (rw_prev_new, rw_s_new, m_conv_new, m_c_new, m_n_new, m_m_new, g_s_new)


def _trunk(x, states, layer_ws, big, final_g, ones_bd, cfg):
    bsz, t, d = x.shape
    x2 = x.reshape(bsz * t, d)
    per_layer = []
    for l in range(DEPTH):
        is_last = l == DEPTH - 1
        earlier = tuple([st[i] for st in per_layer] if is_last else [] for i in BIG_STATES)
        x2, new = _layer(x2, bsz, t, states, l, layer_ws[l], big, ones_bd, cfg, earlier)
        per_layer.append(new)
    new_states = [per_layer[-1][i] if i in BIG_STATES
                  else jnp.stack([st[i] for st in per_layer], axis=0) for i in range(len(states))]
    y = _rmsnorm(x2, final_g, tm=cfg["tm_norm"]).reshape(bsz, t, d)
    return y, new_states


BIG_STATES = (1, 3, 6)

PROMPT_CFG = dict(L=64, L_rwkv=64, bb_rwkv=2, bb_gla=1, bb_mlstm=2, tm=1024, tm_prep=256, tm_merge=1024, tm_norm=512, proj_dtype=BF16)
SAMPLE_CFG = dict(L=16, L_rwkv=16, bb_rwkv=4, bb_gla=4, bb_mlstm=4, tm=128, tm_prep=128, tm_merge=128, tm_norm=128, proj_dtype=F32)


def _zero_states(bsz):
    return (jnp.zeros((DEPTH, bsz, R_COLS), F32),
            jnp.zeros((DEPTH, bsz, R_HEADS, R_HEAD, R_HEAD), F32),
            jnp.zeros((DEPTH, bsz, CONV_W - 1, 2 * BRANCH_W), F32),
            jnp.zeros((DEPTH, bsz, M_HEADS, M_DK, M_DK), F32),
            jnp.zeros((DEPTH, bsz, M_HEADS, M_DK), F32),
            jnp.zeros((DEPTH, bsz, M_HEADS), F32),
            jnp.zeros((DEPTH, bsz, G_HEADS, G_DK, G_DV), F32))


def kernel(x_prompt, x_sample, state_rwkv_shift, state_rwkv_wkv, state_mlstm_conv, state_mlstm_C, state_mlstm_n, state_mlstm_m, state_gla_S, norm1_g, w_in, gate_b, rwkv_mu, rwkv_w0, rwkv_w2, rwkv_a0, rwkv_a2, rwkv_g2, rwkv_k_k, rwkv_k_a, rwkv_r_k, rwkv_ln_g, rwkv_ln_b, mlstm_conv_w, mlstm_conv_b, mlstm_i_b, mlstm_f_b, mlstm_norm_g, gla_a2, gla_a_b, gla_norm_g, w_branch, w_out, norm2_g, ffn_w_gu, ffn_w_down, final_norm_g):
    P = dict(norm1_g=norm1_g, w_in=w_in, gate_b=gate_b, rwkv_mu=rwkv_mu, rwkv_w0=rwkv_w0,
             rwkv_w2=rwkv_w2, rwkv_a0=rwkv_a0, rwkv_a2=rwkv_a2, rwkv_g2=rwkv_g2,
             rwkv_k_k=rwkv_k_k, rwkv_k_a=rwkv_k_a, rwkv_r_k=rwkv_r_k, rwkv_ln_g=rwkv_ln_g,
             rwkv_ln_b=rwkv_ln_b, mlstm_conv_w=mlstm_conv_w, mlstm_conv_b=mlstm_conv_b,
             mlstm_i_b=mlstm_i_b, mlstm_f_b=mlstm_f_b, mlstm_norm_g=mlstm_norm_g,
             gla_a2=gla_a2, gla_a_b=gla_a_b, gla_norm_g=gla_norm_g, w_branch=w_branch,
             w_out=w_out, norm2_g=norm2_g, ffn_w_gu=ffn_w_gu, ffn_w_down=ffn_w_down)
    layer_ws = [_layer_weights(l, P) for l in range(DEPTH)]
    w_in_t = jnp.swapaxes(w_in, 1, 2)
    big = dict(w_in_t=w_in_t, w_in_t_small=_small_group_rows(w_in_t), w_branch=w_branch,
               w_out=w_out, w_gu=ffn_w_gu, w_down=ffn_w_down)
    head_of_lane = jnp.arange(128) // R_HEAD
    ones_bd = (head_of_lane[:, None] == head_of_lane[None, :]).astype(BF16)

    y_p, p_states = _trunk(x_prompt, _zero_states(x_prompt.shape[0]), layer_ws, big,
                           final_norm_g, ones_bd, PROMPT_CFG)
    s_states = (state_rwkv_shift, state_rwkv_wkv, state_mlstm_conv, state_mlstm_C,
                state_mlstm_n, state_mlstm_m, state_gla_S)
    y_s, s_states = _trunk(x_sample, s_states, layer_ws, big, final_norm_g, ones_bd, SAMPLE_CFG)
    return (y_p, y_s, *p_states, *s_states)
```
